```python
import jax
import jax.numpy as jnp
from jax import lax
import numpy as np

D_MODEL = 4096
BATCH = 2
SEQ = 4096
DEPTH = 2

Q_BLOCK = 128
NEG_INF = -1e30
NORM_EPS = 1e-6

D_MIX = D_MODEL
GROUP_WIDTH = D_MIX // 4

NSA_HEADS = 8
NSA_HEAD_DIM = GROUP_WIDTH // NSA_HEADS
NSA_KV_HEADS = 2
NSA_CMP_STRIDE = 16
NSA_CMP_LEN = 2 * NSA_CMP_STRIDE
NSA_SEL_BLOCK = 64
NSA_TOP_N = 16
NSA_WINDOW = 512

FOX_HEADS = 8
FOX_HEAD_DIM = GROUP_WIDTH // FOX_HEADS

MLA_HEADS = 8
MLA_Q_RANK = 768
MLA_KV_RANK = 512
MLA_NOPE_DIM = 128
MLA_ROPE_DIM = 64
MLA_V_DIM = GROUP_WIDTH // MLA_HEADS
ROPE_THETA = 10000.0

SWA_HEADS = 16
SWA_KV_HEADS = 2
SWA_HEAD_DIM = GROUP_WIDTH // SWA_HEADS
SWA_WINDOW = 128

N_GROUPS = 8
EXPERTS_PER_GROUP = 8
N_EXPERTS = N_GROUPS * EXPERTS_PER_GROUP
TOP_K = 2
D_EXPERT = 384
MOE_BLOCK = 128

IN_SPLITS = (
    NSA_HEADS * NSA_HEAD_DIM,
    3 * 2 * NSA_KV_HEADS * NSA_HEAD_DIM,
    3 * NSA_HEADS,
    3 * FOX_HEADS * FOX_HEAD_DIM,
    FOX_HEADS,
    MLA_Q_RANK,
    MLA_KV_RANK,
    MLA_ROPE_DIM,
    SWA_HEADS * SWA_HEAD_DIM,
    2 * SWA_KV_HEADS * SWA_HEAD_DIM,
)
IN_COLS = sum(IN_SPLITS)
OUT_SPLITS = (NSA_HEADS * NSA_HEAD_DIM, FOX_HEADS * FOX_HEAD_DIM, MLA_HEADS * MLA_V_DIM, SWA_HEADS * SWA_HEAD_DIM)

kernel_name = 'hybrid_nsa_fox_mla_swa_hmoe'


def split_cols(x, sizes):
    out, off = [], 0
    for s in sizes:
        out.append(x[..., off:off + s])
        off += s
    return out


def rms_norm(x, g):
    xf = x.astype(jnp.float32)
    y = xf * lax.rsqrt(jnp.mean(xf * xf, axis=-1, keepdims=True) + NORM_EPS)
    return (y * g.astype(jnp.float32)).astype(x.dtype)


def alibi_slopes(n_heads):
    return jnp.exp2(-8.0 * jnp.arange(1, n_heads + 1, dtype=jnp.float32) / n_heads)


def rope_tables(seq):
    pos = jnp.arange(seq, dtype=jnp.float32)
    inv = ROPE_THETA ** (-jnp.arange(0, MLA_ROPE_DIM, 2, dtype=jnp.float32) / MLA_ROPE_DIM)
    ang = pos[:, None] * inv[None, :]
    return jnp.cos(ang), jnp.sin(ang)


def apply_rope(x, cos, sin):
    xf = x.astype(jnp.float32)
    x1, x2 = jnp.split(xf, 2, axis=-1)
    return jnp.concatenate([x1 * cos - x2 * sin, x2 * cos + x1 * sin], axis=-1).astype(x.dtype)


def causal_attention_blocks(q, k, v, cum_log_f=None):
    bsz, seq, nh, dk = q.shape
    nb = seq // Q_BLOCK
    scale = dk ** -0.5
    k_pos = jnp.arange(seq)
    qb = q.reshape(bsz, nb, Q_BLOCK, nh, dk).swapaxes(0, 1)
    xs = (qb, jnp.arange(nb))
    if cum_log_f is not None:
        c_keys = cum_log_f.transpose(0, 2, 1)
        xs = xs + (cum_log_f.reshape(bsz, nb, Q_BLOCK, nh).swapaxes(0, 1),)

    def body(args):
        qi, blk = args[0], args[1]
        s = jnp.einsum('bqhd,bkhd->bhqk', qi, k).astype(jnp.float32) * scale
        if cum_log_f is not None:
            s = s + args[2].transpose(0, 2, 1)[..., None] - c_keys[:, :, None, :]
        q_pos = blk * Q_BLOCK + jnp.arange(Q_BLOCK)
        s = jnp.where(k_pos[None, :] <= q_pos[:, None], s, NEG_INF)
        p = jax.nn.softmax(s, axis=-1).astype(v.dtype)
        return jnp.einsum('bhqk,bkhd->bqhd', p, v)

    out = lax.map(body, xs)
    return out.swapaxes(0, 1).reshape(bsz, seq, nh, v.shape[-1])


def band_blocks(t, n_prev):
    bsz, seq = t.shape[:2]
    nb = seq // Q_BLOCK
    pad = jnp.pad(t, ((0, 0), (n_prev * Q_BLOCK, 0)) + ((0, 0),) * (t.ndim - 2))
    blocks = pad.reshape((bsz, nb + n_prev, Q_BLOCK) + t.shape[2:])
    return jnp.concatenate([blocks[:, j:j + nb] for j in range(n_prev + 1)], axis=2)


def banded_attention(q, k, v, window, slopes, sinks=None):
    bsz, seq, ng, nr, hd = q.shape
    nb = seq // Q_BLOCK
    n_prev = -(-(window - 1) // Q_BLOCK)
    kb = band_blocks(k, n_prev)
    vb = band_blocks(v, n_prev)
    kbl = kb.shape[2]
    qb = q.reshape(bsz, nb, Q_BLOCK, ng, nr, hd)
    s = jnp.einsum('bnqgrd,bnkgd->bngrqk', qb, kb).astype(jnp.float32) * (hd ** -0.5)
    q_pos = jnp.arange(nb)[:, None] * Q_BLOCK + jnp.arange(Q_BLOCK)[None, :]
    k_pos = (jnp.arange(nb)[:, None] - n_prev) * Q_BLOCK + jnp.arange(kbl)[None, :]
    dist = q_pos[:, :, None] - k_pos[:, None, :]
    mask = (dist >= 0) & (dist < window) & (k_pos[:, None, :] >= 0)
    s = s - slopes[None, None, :, :, None, None] * dist[None, :, None, None].astype(jnp.float32)
    s = jnp.where(mask[None, :, None, None], s, NEG_INF)
    if sinks is None:
        p = jax.nn.softmax(s, axis=-1)
    else:
        sk = sinks.astype(jnp.float32)[None, None, :, :, None, None]
        m = jnp.maximum(jnp.max(s, axis=-1, keepdims=True), sk)
        e = jnp.exp(s - m)
        p = e / (jnp.sum(e, axis=-1, keepdims=True) + jnp.exp(sk - m))
    out = jnp.einsum('bngrqk,bnkgd->bnqgrd', p.astype(v.dtype), vb)
    return out.reshape(bsz, seq, ng, nr, hd)


def selected_attention(q, k, v, sel_idx, slopes):
    bsz, seq, ng, nr, hd = q.shape
    n_sel = sel_idx.shape[-1]
    nb = seq // Q_BLOCK
    n_keys = n_sel * NSA_SEL_BLOCK
    kblk = k.transpose(0, 2, 1, 3).reshape(bsz, ng, seq // NSA_SEL_BLOCK, NSA_SEL_BLOCK, hd)
    vblk = v.transpose(0, 2, 1, 3).reshape(bsz, ng, seq // NSA_SEL_BLOCK, NSA_SEL_BLOCK, hd)
    qb = q.reshape(bsz, nb, Q_BLOCK, ng, nr, hd).swapaxes(0, 1)
    ib = sel_idx.reshape(bsz, ng, nb, Q_BLOCK, n_sel).transpose(2, 0, 1, 3, 4)
    b_ix = jnp.arange(bsz)[:, None, None, None]
    g_ix = jnp.arange(ng)[None, :, None, None]
    scale = hd ** -0.5

    def body(args):
        qi, ii, blk = args
        ks = kblk[b_ix, g_ix, ii].reshape(bsz, ng, Q_BLOCK, n_keys, hd)
        vs = vblk[b_ix, g_ix, ii].reshape(bsz, ng, Q_BLOCK, n_keys, hd)
        s_pos = (ii[..., None] * NSA_SEL_BLOCK + jnp.arange(NSA_SEL_BLOCK)).reshape(bsz, ng, Q_BLOCK, n_keys)
        t_pos = blk * Q_BLOCK + jnp.arange(Q_BLOCK)
        dist = t_pos[None, None, :, None] - s_pos
        s = jnp.einsum('bqgrd,bgqkd->bgrqk', qi, ks).astype(jnp.float32) * scale
        s = s - slopes[None, :, :, None, None] * dist[:, :, None].astype(jnp.float32)
        s = jnp.where(dist[:, :, None] >= 0, s, NEG_INF)
        p = jax.nn.softmax(s, axis=-1).astype(v.dtype)
        return jnp.einsum('bgrqk,bgqkd->bqgrd', p, vs)

    out = lax.map(body, (qb, ib, jnp.arange(nb)))
    return out.swapaxes(0, 1).reshape(bsz, seq, ng, nr, hd)


def compress_blocks(k, pos, w1, w2):
    bsz, seq, ng, hd = k.shape
    half = k.reshape(bsz, seq // NSA_CMP_STRIDE, NSA_CMP_STRIDE, ng, hd)
    win = jnp.concatenate([half[:, :-1], half[:, 1:]], axis=2) + pos[None, None, :, None, :]
    nc = win.shape[1]
    flat = win.transpose(0, 1, 3, 2, 4).reshape(bsz, nc, ng, NSA_CMP_LEN * hd)
    return jax.nn.gelu(flat @ w1) @ w2


def nsa_mixer(q, kv, gate_logits, kc_pos, kc_w1, kc_w2, vc_pos, vc_w1, vc_w2):
    bsz, seq = q.shape[:2]
    ng, nr, hd = NSA_KV_HEADS, NSA_HEADS // NSA_KV_HEADS, NSA_HEAD_DIM
    q = q.reshape(bsz, seq, ng, nr, hd)
    kv = kv.reshape(bsz, seq, 3, 2, ng, hd)
    slopes = alibi_slopes(NSA_HEADS).reshape(ng, nr)
    t_pos = jnp.arange(seq)
    kc = compress_blocks(kv[:, :, 0, 0], kc_pos, kc_w1, kc_w2)
    vc = compress_blocks(kv[:, :, 0, 1], vc_pos, vc_w1, vc_w2)
    nc = kc.shape[1]
    c_start = jnp.arange(nc) * NSA_CMP_STRIDE
    c_end = c_start + NSA_CMP_LEN - 1
    dist = t_pos[:, None] - c_end[None, :]
    valid = dist >= 0
    s = jnp.einsum('bsgrd,bcgd->bgrsc', q, kc).astype(jnp.float32) * (hd ** -0.5)
    s = s - slopes[:, :, None, None] * dist.astype(jnp.float32)
    s = jnp.where(valid, s, NEG_INF)
    p_cmp = jax.nn.softmax(s, axis=-1) * valid
    o_cmp = jnp.einsum('bgrsc,bcgd->bsgrd', p_cmp.astype(q.dtype), vc)
    n_slc = seq // NSA_SEL_BLOCK
    j = jnp.arange(n_slc)
    overlap = ((c_start[:, None] <= j[None, :] * NSA_SEL_BLOCK + NSA_SEL_BLOCK - 1)
               & (c_end[:, None] >= j[None, :] * NSA_SEL_BLOCK)).astype(jnp.float32)
    imp = jnp.einsum('bgrsc,cn->bgsn', p_cmp, overlap)
    cur = (t_pos // NSA_SEL_BLOCK)[:, None]
    forced = (j[None, :] == 0) | (j[None, :] == cur) | (j[None, :] == cur - 1)
    imp = jnp.where(forced, 1e6, imp)
    imp = jnp.where(j[None, :] > cur, -1e6, imp)
    _, sel_idx = lax.top_k(imp, min(NSA_TOP_N, n_slc))
    o_slc = selected_attention(q, kv[:, :, 1, 0], kv[:, :, 1, 1], sel_idx, slopes)
    o_win = banded_attention(q, kv[:, :, 2, 0], kv[:, :, 2, 1], NSA_WINDOW, slopes)
    g = jax.nn.sigmoid(gate_logits.astype(jnp.float32)).reshape(bsz, seq, 3, ng, nr, 1).astype(q.dtype)
    out = g[:, :, 0] * o_cmp + g[:, :, 1] * o_slc + g[:, :, 2] * o_win
    return out.reshape(bsz, seq, NSA_HEADS * hd)


def fox_mixer(qkv, f_logit, f_bias):
    bsz, seq = qkv.shape[:2]
    qkv = qkv.reshape(bsz, seq, 3, FOX_HEADS, FOX_HEAD_DIM)
    log_f = jax.nn.log_sigmoid(f_logit.astype(jnp.float32) + f_bias.astype(jnp.float32))
    cum = jnp.cumsum(log_f, axis=1)
    out = causal_attention_blocks(qkv[:, :, 0], qkv[:, :, 1], qkv[:, :, 2], cum)
    return out.reshape(bsz, seq, FOX_HEADS * FOX_HEAD_DIM)


def mla_mixer(c_q, c_kv, k_r, q_norm_g, kv_norm_g, w_uq, w_ukv):
    bsz, seq = c_q.shape[:2]
    q = (rms_norm(c_q, q_norm_g) @ w_uq).reshape(bsz, seq, MLA_HEADS, MLA_NOPE_DIM + MLA_ROPE_DIM)
    kv = (rms_norm(c_kv, kv_norm_g) @ w_ukv).reshape(bsz, seq, MLA_HEADS, MLA_NOPE_DIM + MLA_V_DIM)
    cos, sin = rope_tables(seq)
    q_rope = apply_rope(q[..., MLA_NOPE_DIM:], cos[:, None, :], sin[:, None, :])
    k_rope = apply_rope(k_r, cos, sin)
    q = jnp.concatenate([q[..., :MLA_NOPE_DIM], q_rope], axis=-1)
    k = jnp.concatenate([kv[..., :MLA_NOPE_DIM],
                         jnp.broadcast_to(k_rope[:, :, None, :], (bsz, seq, MLA_HEADS, MLA_ROPE_DIM))], axis=-1)
    out = causal_attention_blocks(q, k, kv[..., MLA_NOPE_DIM:])
    return out.reshape(bsz, seq, MLA_HEADS * MLA_V_DIM)


def swa_mixer(q, kv, sinks):
    bsz, seq = q.shape[:2]
    nr = SWA_HEADS // SWA_KV_HEADS
    q = q.reshape(bsz, seq, SWA_KV_HEADS, nr, SWA_HEAD_DIM)
    kv = kv.reshape(bsz, seq, 2, SWA_KV_HEADS, SWA_HEAD_DIM)
    slopes = alibi_slopes(SWA_HEADS).reshape(SWA_KV_HEADS, nr)
    out = banded_attention(q, kv[:, :, 0], kv[:, :, 1], SWA_WINDOW, slopes, sinks.reshape(SWA_KV_HEADS, nr))
    return out.reshape(bsz, seq, SWA_HEADS * SWA_HEAD_DIM)


def moe_ffn(xn, rg_w, rg_b, re_w, re_b, w_gate, w_up, w_down):
    bsz, seq, d = xn.shape
    xt = xn.reshape(-1, d)
    n_tok = xt.shape[0]
    pg = jax.nn.softmax((xt @ rg_w).astype(jnp.float32) + rg_b.astype(jnp.float32), axis=-1)
    pg_top, g_idx = lax.top_k(pg, 1)
    le = ((xt @ re_w).astype(jnp.float32) + re_b.astype(jnp.float32)).reshape(n_tok, N_GROUPS, EXPERTS_PER_GROUP)
    le_g = jnp.take_along_axis(le, g_idx[:, :, None], axis=1)[:, 0]
    ev, ei = lax.top_k(le_g, TOP_K)
    gate = pg_top * jax.nn.softmax(ev, axis=-1)
    expert = g_idx * EXPERTS_PER_GROUP + ei
    n_assign = n_tok * TOP_K
    flat_e = expert.reshape(-1)
    flat_tok = jnp.repeat(jnp.arange(n_tok, dtype=jnp.int32), TOP_K)
    flat_w = gate.reshape(-1)
    order = jnp.argsort(flat_e)
    se, st, sw = flat_e[order], flat_tok[order], flat_w[order]
    counts = jnp.zeros((N_EXPERTS,), jnp.int32).at[flat_e].add(1)
    starts = jnp.cumsum(counts) - counts
    pcounts = (counts + MOE_BLOCK - 1) // MOE_BLOCK * MOE_BLOCK
    pends = jnp.cumsum(pcounts)
    pstarts = pends - pcounts
    dest = pstarts[se] + jnp.arange(n_assign, dtype=jnp.int32) - starts[se]
    n_blocks = (n_assign + N_EXPERTS * (MOE_BLOCK - 1)) // MOE_BLOCK
    n_pad = n_blocks * MOE_BLOCK
    tok_pad = jnp.full((n_pad,), n_tok, jnp.int32).at[dest].set(st)
    w_pad = jnp.zeros((n_pad,), xt.dtype).at[dest].set(sw.astype(xt.dtype))
    blk_e = jnp.minimum(jnp.searchsorted(pends, jnp.arange(n_blocks) * MOE_BLOCK, side='right'), N_EXPERTS - 1)
    x_ext = jnp.concatenate([xt, jnp.zeros((1, d), xt.dtype)], axis=0)

    def run_block(args):
        tok, e = args
        xb = x_ext[tok]
        h = jax.nn.silu(xb @ w_gate[e]) * (xb @ w_up[e])
        return h @ w_down[e]

    yb = lax.map(run_block, (tok_pad.reshape(n_blocks, MOE_BLOCK), blk_e))
    y = jnp.zeros((n_tok + 1, d), xt.dtype).at[tok_pad].add(yb.reshape(n_pad, d) * w_pad[:, None])
    return y[:n_tok].reshape(bsz, seq, d)


def setup_inputs(seed: int = 0) -> dict:
    key = jax.random.key(seed)
    ks = jax.random.split(key, 26)
    f32 = jnp.float32
    L = DEPTH
    hd = NSA_HEAD_DIM

    def nrm(k, shape, scale):
        return jax.random.normal(k, shape, f32) * scale

    def gain(k, shape):
        return 1.0 + 0.01 * jax.random.normal(k, shape, f32)

    return {
        'x': nrm(ks[0], (BATCH, SEQ, D_MODEL), 1.0),
        'norm_mix_g': gain(ks[1], (L, D_MODEL)),
        'w_in': nrm(ks[2], (L, D_MODEL, IN_COLS), D_MODEL ** -0.5),
        'nsa_kc_pos': nrm(ks[3], (L, NSA_CMP_LEN, hd), 0.02),
        'nsa_kc_w1': nrm(ks[4], (L, NSA_CMP_LEN * hd, hd), (NSA_CMP_LEN * hd) ** -0.5),
        'nsa_kc_w2': nrm(ks[5], (L, hd, hd), hd ** -0.5),
        'nsa_vc_pos': nrm(ks[6], (L, NSA_CMP_LEN, hd), 0.02),
        'nsa_vc_w1': nrm(ks[7], (L, NSA_CMP_LEN * hd, hd), (NSA_CMP_LEN * hd) ** -0.5),
        'nsa_vc_w2': nrm(ks[8], (L, hd, hd), hd ** -0.5),
        'fox_f_bias': 3.0 + 0.1 * jax.random.normal(ks[9], (L, FOX_HEADS), f32),
        'mla_q_norm_g': gain(ks[10], (L, MLA_Q_RANK)),
        'mla_kv_norm_g': gain(ks[11], (L, MLA_KV_RANK)),
        'mla_w_uq': nrm(ks[12], (L, MLA_Q_RANK, MLA_HEADS * (MLA_NOPE_DIM + MLA_ROPE_DIM)), MLA_Q_RANK ** -0.5),
        'mla_w_ukv': nrm(ks[13], (L, MLA_KV_RANK, MLA_HEADS * (MLA_NOPE_DIM + MLA_V_DIM)), MLA_KV_RANK ** -0.5),
        'swa_sinks': nrm(ks[14], (L, SWA_HEADS), 0.5),
        'out_norm_g': gain(ks[15], (L, D_MIX)),
        'w_out': nrm(ks[16], (L, D_MIX, D_MODEL), D_MIX ** -0.5),
        'norm_ffn_g': gain(ks[17], (L, D_MODEL)),
        'router_group_w': nrm(ks[18], (L, D_MODEL, N_GROUPS), D_MODEL ** -0.5),
        'router_group_b': nrm(ks[19], (L, N_GROUPS), 0.01),
        'router_expert_w': nrm(ks[20], (L, D_MODEL, N_EXPERTS), D_MODEL ** -0.5),
        'router_expert_b': nrm(ks[21], (L, N_EXPERTS), 0.01),
        'exp_w_gate': nrm(ks[22], (L, N_EXPERTS, D_MODEL, D_EXPERT), D_MODEL ** -0.5),
        'exp_w_up': nrm(ks[23], (L, N_EXPERTS, D_MODEL, D_EXPERT), D_MODEL ** -0.5),
        'exp_w_down': nrm(ks[24], (L, N_EXPERTS, D_EXPERT, D_MODEL), D_EXPERT ** -0.5),
        'final_norm_g': gain(ks[25], (D_MODEL,)),
    }


def reference(x, norm_mix_g, w_in, nsa_kc_pos, nsa_kc_w1, nsa_kc_w2, nsa_vc_pos, nsa_vc_w1, nsa_vc_w2,
              fox_f_bias, mla_q_norm_g, mla_kv_norm_g, mla_w_uq, mla_w_ukv, swa_sinks, out_norm_g, w_out,
              norm_ffn_g, router_group_w, router_group_b, router_expert_w, router_expert_b,
              exp_w_gate, exp_w_up, exp_w_down, final_norm_g):
    for l in range(DEPTH):
        xn = rms_norm(x, norm_mix_g[l])
        proj = xn @ w_in[l]
        a_q, a_kv, a_g, b_qkv, b_f, c_q, c_kv, c_kr, d_q, d_kv = split_cols(proj, IN_SPLITS)
        outs = (
            nsa_mixer(a_q, a_kv, a_g, nsa_kc_pos[l], nsa_kc_w1[l], nsa_kc_w2[l],
                      nsa_vc_pos[l], nsa_vc_w1[l], nsa_vc_w2[l]),
            fox_mixer(b_qkv, b_f, fox_f_bias[l]),
            mla_mixer(c_q, c_kv, c_kr, mla_q_norm_g[l], mla_kv_norm_g[l], mla_w_uq[l], mla_w_ukv[l]),
            swa_mixer(d_q, d_kv, swa_sinks[l]),
        )
        gains = split_cols(out_norm_g[l], OUT_SPLITS)
        mixed = jnp.concatenate([rms_norm(o, g) for o, g in zip(outs, gains)], axis=-1)
        x = x + mixed @ w_out[l]
        x = x + moe_ffn(rms_norm(x, norm_ffn_g[l]), router_group_w[l], router_group_b[l],
                        router_expert_w[l], router_expert_b[l], exp_w_gate[l], exp_w_up[l], exp_w_down[l])
    return rms_norm(x, final_norm_g)
```

```python
import functools
import math

import numpy as np
import jax
import jax.numpy as jnp
from jax import lax
from jax.experimental import pallas as pl
from jax.experimental.pallas import tpu as pltpu

F32 = jnp.float32
BF16 = jnp.bfloat16
I32 = jnp.int32

NEG_INF = -1e30
NORM_EPS = 1e-6
LANES = 128

Q_BLOCK = 128
GROUP_WIDTH = 1024
NSA_HEADS, NSA_KV_HEADS, NSA_HEAD_DIM = 8, 2, 128
NSA_CMP_STRIDE, NSA_CMP_LEN, NSA_SEL_BLOCK, NSA_TOP_N, NSA_WINDOW = 16, 32, 64, 16, 512
FOX_HEADS, FOX_HEAD_DIM = 8, 128
MLA_HEADS, MLA_Q_RANK, MLA_KV_RANK, MLA_NOPE_DIM, MLA_ROPE_DIM, MLA_V_DIM = 8, 768, 512, 128, 64, 128
ROPE_THETA = 10000.0
SWA_HEADS, SWA_KV_HEADS, SWA_HEAD_DIM, SWA_WINDOW = 16, 2, 64, 128
N_GROUPS, EXPERTS_PER_GROUP, TOP_K, D_EXPERT = 8, 8, 2, 384
N_EXPERTS = N_GROUPS * EXPERTS_PER_GROUP
IN_SPLITS = (1024, 1536, 24, 3072, 8, 768, 512, 64, 1024, 256)

MAIN_AQ, MAIN_AKV, MAIN_BQKV, MAIN_DQ, MAIN_DKV, MAIN_COLS = 0, 1024, 2560, 5632, 6656, 6912
SMALL_CQ, SMALL_CKV, SMALL_MISC, SMALL_COLS = 0, 768, 1280, 1536
MISC_GATE_LANE, MISC_FORGET_LANE = 64, 88

MOE_ROWS = 256
DE_CHUNK = 128


def _dot(a, b):
    return jnp.dot(a, b, preferred_element_type=F32)


def _dot_nt(a, b):
    return lax.dot_general(a, b, (((1,), (1,)), ((), ())), preferred_element_type=F32)


def _alibi_slopes(n_heads):
    return jnp.exp2(-8.0 * jnp.arange(1, n_heads + 1, dtype=F32) / n_heads)


def _split3(x):
    hi = x.astype(BF16)
    r1 = x - hi.astype(F32)
    mid = r1.astype(BF16)
    lo = (r1 - mid.astype(F32)).astype(BF16)
    return hi, mid, lo


def _params(sem, vmem_mb=None):
    kw = dict(dimension_semantics=sem)
    if vmem_mb is not None:
        kw["vmem_limit_bytes"] = vmem_mb * 1024 * 1024
    return pltpu.CompilerParams(**kw)


def _norm_matmul_kernel(x_ref, g_ref, w_ref, o_ref, xn_ref, *, col_off, k):
    @pl.when(pl.program_id(1) == 0)
    def _():
        x = x_ref[:, col_off:col_off + k].astype(F32)
        ms = jnp.mean(x * x, axis=-1, keepdims=True)
        xn_ref[...] = (x * lax.rsqrt(ms + NORM_EPS) * g_ref[...]).astype(BF16)

    o_ref[...] = _dot(xn_ref[...], w_ref[...]).astype(o_ref.dtype)


def _norm_matmul(x, gain, w, *, out_dtype, col_off=0, k=None, tm=512, tn=512, name="norm_matmul"):
    m, kfull = x.shape
    k = kfull if k is None else k
    n = w.shape[1]
    tm = min(tm, m)
    assert m % tm == 0 and n % tn == 0 and w.shape[0] == k
    return pl.pallas_call(
        functools.partial(_norm_matmul_kernel, col_off=col_off, k=k),
        grid=(m // tm, n // tn),
        in_specs=[
            pl.BlockSpec((tm, kfull), lambda i, j: (i, 0)),
            pl.BlockSpec((1, k), lambda i, j: (0, 0)),
            pl.BlockSpec((k, tn), lambda i, j: (0, j)),
        ],
        out_specs=pl.BlockSpec((tm, tn), lambda i, j: (i, j)),
        out_shape=jax.ShapeDtypeStruct((m, n), out_dtype),
        scratch_shapes=[pltpu.VMEM((tm, k), BF16)],
        compiler_params=_params(("parallel", "arbitrary"), 56),
        name=name,
    )(x, gain.reshape(1, k).astype(F32), w)


def _out_proj_kernel(a_ref, b_ref, c_ref, d_ref, g_ref, w_ref, r_ref, o_ref, xn_ref):
    @pl.when(pl.program_id(1) == 0)
    def _():
        for i, ref in enumerate((a_ref, b_ref, c_ref, d_ref)):
            x = ref[...]
            ms = jnp.mean(x * x, axis=-1, keepdims=True)
            g = g_ref[:, i * GROUP_WIDTH:(i + 1) * GROUP_WIDTH]
            xn_ref[:, i * GROUP_WIDTH:(i + 1) * GROUP_WIDTH] = (x * lax.rsqrt(ms + NORM_EPS) * g).astype(BF16)

    o_ref[...] = r_ref[...] + _dot(xn_ref[...], w_ref[...])


def _out_proj(outs, gain, w, resid, *, tm=512, tn=512):
    m, d = resid.shape
    tm = min(tm, m)
    k = 4 * GROUP_WIDTH
    grp = pl.BlockSpec((tm, GROUP_WIDTH), lambda i, j: (i, 0))
    return pl.pallas_call(
        _out_proj_kernel,
        grid=(m // tm, d // tn),
        in_specs=[grp, grp, grp, grp,
                  pl.BlockSpec((1, k), lambda i, j: (0, 0)),
                  pl.BlockSpec((k, tn), lambda i, j: (0, j)),
                  pl.BlockSpec((tm, tn), lambda i, j: (i, j))],
        out_specs=pl.BlockSpec((tm, tn), lambda i, j: (i, j)),
        out_shape=jax.ShapeDtypeStruct((m, d), F32),
        scratch_shapes=[pltpu.VMEM((tm, k), BF16)],
        compiler_params=_params(("parallel", "arbitrary"), 56),
        name="out_proj",
    )(*outs, gain.reshape(1, k).astype(F32), w, resid)


def _fox_prep_kernel(x_ref, b_ref, cum_ref, cumt_ref, carry_ref, *, ts):
    @pl.when(pl.program_id(1) == 0)
    def _():
        carry_ref[...] = jnp.zeros_like(carry_ref)

    z = x_ref[...] + b_ref[...]
    lf = jnp.minimum(z, 0.0) - jnp.log1p(jnp.exp(-jnp.abs(z)))
    row = lax.broadcasted_iota(I32, (ts, ts), 0)
    col = lax.broadcasted_iota(I32, (ts, ts), 1)
    tri = jnp.where(col <= row, 1.0, 0.0).astype(BF16)
    hi, mid, lo = _split3(lf)
    cum = (_dot(tri, hi) + _dot(tri, mid)) + _dot(tri, lo) + carry_ref[...]
    cum_ref[...] = cum
    carry_ref[...] = cum[ts - 1:ts, :]
    cum_t = jnp.concatenate([cum[i * LANES:(i + 1) * LANES, :].T for i in range(ts // LANES)], axis=1)
    cumt_ref[0] = cum_t[MISC_FORGET_LANE:MISC_FORGET_LANE + FOX_HEADS, :]


def _fox_prep(small, f_bias, bsz, seq, *, ts=512):
    ts = min(ts, seq)
    n = bsz * seq
    ns = seq // ts
    bias = jnp.zeros((1, LANES), F32).at[0, MISC_FORGET_LANE:MISC_FORGET_LANE + FOX_HEADS].set(f_bias.astype(F32))
    misc_blk = SMALL_MISC // LANES
    return pl.pallas_call(
        functools.partial(_fox_prep_kernel, ts=ts),
        grid=(bsz, ns),
        in_specs=[pl.BlockSpec((ts, LANES), lambda b, i: (b * ns + i, misc_blk)),
                  pl.BlockSpec((1, LANES), lambda b, i: (0, 0))],
        out_specs=[pl.BlockSpec((ts, LANES), lambda b, i: (b * ns + i, 0)),
                   pl.BlockSpec((1, FOX_HEADS, ts), lambda b, i: (b, 0, i))],
        out_shape=[jax.ShapeDtypeStruct((n, LANES), F32),
                   jax.ShapeDtypeStruct((bsz, FOX_HEADS, seq), F32)],
        scratch_shapes=[pltpu.VMEM((1, LANES), F32)],
        compiler_params=_params(("parallel", "arbitrary")),
        name="fox_prep",
    )(small, bias)


def _flash_update(s, v, m_sc, l_sc, acc_sc):
    m_prev = m_sc[...]
    m_new = jnp.maximum(m_prev, jnp.max(s, axis=1, keepdims=True))
    alpha = jnp.exp(m_prev - m_new)
    p = jnp.exp(s - m_new)
    l_sc[...] = alpha * l_sc[...] + jnp.sum(p, axis=1, keepdims=True)
    acc_sc[...] = alpha * acc_sc[...] + _dot(p.astype(BF16), v)
    m_sc[...] = m_new


def _flash_init(m_sc, l_sc, acc_sc):
    m_sc[...] = jnp.full(m_sc.shape, NEG_INF, F32)
    l_sc[...] = jnp.zeros(l_sc.shape, F32)
    acc_sc[...] = jnp.zeros(acc_sc.shape, F32)


def _fox_attn_kernel(q_ref, k_ref, v_ref, cq_ref, ck_ref, o_ref, m_sc, l_sc, acc_sc, *, tq, scale):
    h = pl.program_id(1)
    qi = pl.program_id(2)
    q = q_ref[...]
    lane = lax.broadcasted_iota(I32, (tq, LANES), 1)
    cq = jnp.sum(jnp.where(lane == MISC_FORGET_LANE + h, cq_ref[...], 0.0), axis=1, keepdims=True)
    _flash_init(m_sc, l_sc, acc_sc)

    def step(ki, masked):
        k0 = pl.multiple_of(ki * tq, tq)
        k = k_ref[pl.ds(k0, tq), :]
        v = v_ref[pl.ds(k0, tq), :]
        ck = ck_ref[0, pl.ds(ki, 1), :]
        s = _dot_nt(q, k) * scale + cq - ck
        if masked:
            row = lax.broadcasted_iota(I32, (tq, tq), 0)
            col = lax.broadcasted_iota(I32, (tq, tq), 1)
            s = jnp.where(col <= row, s, NEG_INF)
        _flash_update(s, v, m_sc, l_sc, acc_sc)

    def body(ki, carry):
        step(ki, False)
        return carry

    lax.fori_loop(0, qi, body, 0)
    step(qi, True)
    o_ref[...] = acc_sc[...] / l_sc[...]


def _fox_attn(main, cum, cumt, bsz, seq, *, tq=512):
    tq = min(tq, seq)
    nq = seq // tq
    n = bsz * seq
    d = FOX_HEAD_DIM
    qb, kb, vb = MAIN_BQKV // d, MAIN_BQKV // d + FOX_HEADS, MAIN_BQKV // d + 2 * FOX_HEADS
    cumt3 = cumt.reshape(bsz * FOX_HEADS, nq, tq)
    return pl.pallas_call(
        functools.partial(_fox_attn_kernel, tq=tq, scale=d ** -0.5),
        grid=(bsz, FOX_HEADS, nq),
        in_specs=[pl.BlockSpec((tq, d), lambda b, h, i: (b * nq + i, qb + h)),
                  pl.BlockSpec((seq, d), lambda b, h, i: (b, kb + h)),
                  pl.BlockSpec((seq, d), lambda b, h, i: (b, vb + h)),
                  pl.BlockSpec((tq, LANES), lambda b, h, i: (b * nq + i, 0)),
                  pl.BlockSpec((1, nq, tq), lambda b, h, i: (b * FOX_HEADS + h, 0, 0))],
        out_specs=pl.BlockSpec((tq, d), lambda b, h, i: (b * nq + i, h)),
        out_shape=jax.ShapeDtypeStruct((n, FOX_HEADS * d), F32),
        scratch_shapes=[pltpu.VMEM((tq, 1), F32), pltpu.VMEM((tq, 1), F32), pltpu.VMEM((tq, d), F32)],
        compiler_params=_params(("parallel", "parallel", "arbitrary")),
        name="fox_attn",
    )(main, main, main, cum, cumt3)


def _mla_attn_kernel(qn_ref, qr_ref, kn_ref, kr_ref, v_ref, o_ref, m_sc, l_sc, acc_sc, *, tq, scale):
    qi = pl.program_id(2)
    q = jnp.concatenate([qn_ref[...].astype(BF16), qr_ref[...]], axis=1)
    _flash_init(m_sc, l_sc, acc_sc)

    def step(ki, masked):
        k0 = pl.multiple_of(ki * tq, tq)
        k = jnp.concatenate([kn_ref[pl.ds(k0, tq), :], kr_ref[pl.ds(k0, tq), :]], axis=1)
        v = v_ref[pl.ds(k0, tq), :]
        s = _dot_nt(q, k) * scale
        if masked:
            row = lax.broadcasted_iota(I32, (tq, tq), 0)
            col = lax.broadcasted_iota(I32, (tq, tq), 1)
            s = jnp.where(col <= row, s, NEG_INF)
        _flash_update(s, v, m_sc, l_sc, acc_sc)

    def body(ki, carry):
        step(ki, False)
        return carry

    lax.fori_loop(0, qi, body, 0)
    step(qi, True)
    o_ref[...] = acc_sc[...] / l_sc[...]


def _mla_attn(qup, qrope, kvup, krope, bsz, seq, *, tq=512):
    tq = min(tq, seq)
    nq = seq // tq
    n = bsz * seq
    d = LANES
    return pl.pallas_call(
        functools.partial(_mla_attn_kernel, tq=tq, scale=(MLA_NOPE_DIM + MLA_ROPE_DIM) ** -0.5),
        grid=(bsz, MLA_HEADS, nq),
        in_specs=[pl.BlockSpec((tq, d), lambda b, h, i: (b * nq + i, h)),
                  pl.BlockSpec((tq, d), lambda b, h, i: (b * nq + i, h)),
                  pl.BlockSpec((seq, d), lambda b, h, i: (b, 2 * h)),
                  pl.BlockSpec((seq, d), lambda b, h, i: (b, 0)),
                  pl.BlockSpec((seq, d), lambda b, h, i: (b, 2 * h + 1))],
        out_specs=pl.BlockSpec((tq, d), lambda b, h, i: (b * nq + i, h)),
        out_shape=jax.ShapeDtypeStruct((n, MLA_HEADS * MLA_V_DIM), F32),
        scratch_shapes=[pltpu.VMEM((tq, 1), F32), pltpu.VMEM((tq, 1), F32), pltpu.VMEM((tq, d), F32)],
        compiler_params=_params(("parallel", "parallel", "arbitrary")),
        name="mla_attn",
    )(qup, qrope, kvup, krope, kvup)


def _rope_kernel(q_ref, k_ref, cos_ref, sin_ref, qo_ref, ko_ref):
    half = MLA_ROPE_DIM // 2

    def rope(x):
        lane = lax.broadcasted_iota(I32, x.shape, 1)
        first = (lane % MLA_ROPE_DIM) < half
        swapped = jnp.where(first, pltpu.roll(x, LANES - half, 1), pltpu.roll(x, half, 1))
        return x * cos_ref[...] + swapped * sin_ref[...]

    for t in range(q_ref.shape[1] // LANES):
        qo_ref[:, t * LANES:(t + 1) * LANES] = rope(q_ref[:, t * LANES:(t + 1) * LANES]).astype(BF16)
    k = rope(k_ref[...])
    lane = lax.broadcasted_iota(I32, k.shape, 1)
    ko_ref[...] = jnp.where(lane < MLA_ROPE_DIM, k, 0.0).astype(BF16)


def _rope(qup, small, bsz, seq, *, ts=512):
    ts = min(ts, seq)
    ns = seq // ts
    n = bsz * seq
    pos = jnp.arange(seq, dtype=F32)
    inv = ROPE_THETA ** (-jnp.arange(0, MLA_ROPE_DIM, 2, dtype=F32) / MLA_ROPE_DIM)
    ang = pos[:, None] * inv[None, :]
    cos, sin = jnp.cos(ang), jnp.sin(ang)
    cos_t = jnp.concatenate([cos, cos, cos, cos], axis=1)
    sin_t = jnp.concatenate([-sin, sin, -sin, sin], axis=1)
    w = MLA_HEADS * LANES
    return pl.pallas_call(
        _rope_kernel,
        grid=(bsz, ns),
        in_specs=[pl.BlockSpec((ts, w), lambda b, i: (b * ns + i, 1)),
                  pl.BlockSpec((ts, LANES), lambda b, i: (b * ns + i, SMALL_MISC // LANES)),
                  pl.BlockSpec((ts, LANES), lambda b, i: (i, 0)),
                  pl.BlockSpec((ts, LANES), lambda b, i: (i, 0))],
        out_specs=[pl.BlockSpec((ts, w), lambda b, i: (b * ns + i, 0)),
                   pl.BlockSpec((ts, LANES), lambda b, i: (b * ns + i, 0))],
        out_shape=[jax.ShapeDtypeStruct((n, w), BF16), jax.ShapeDtypeStruct((n, LANES), BF16)],
        compiler_params=_params(("parallel", "parallel")),
        name="mla_rope",
    )(qup, small, cos_t, sin_t)


def _banded_kernel(*refs, tq, nr, hd, window, slab, scale, has_sinks, seq):
    if has_sinks:
        slopes_ref, sinks_ref, q_ref, k_ref, v_ref, o_ref = refs
    else:
        slopes_ref, q_ref, k_ref, v_ref, o_ref = refs
    g = pl.program_id(1)
    qi = pl.program_id(2)
    q0 = qi * tq
    start = pl.multiple_of(jnp.minimum(jnp.maximum(q0 + tq - slab, 0), seq - slab), tq)
    kt = k_ref[pl.ds(start, slab), :]
    vt = v_ref[pl.ds(start, slab), :]
    q = q_ref[...]
    if hd == LANES:
        qs = jnp.concatenate([q[:, r * hd:(r + 1) * hd] for r in range(nr)], axis=0)
    else:
        lane = lax.broadcasted_iota(I32, (tq, LANES), 1)
        mine = (lane // hd) == g
        parts = []
        for r in range(nr):
            qr = q[:, r * hd:(r + 1) * hd]
            parts.append(jnp.where(mine, jnp.concatenate([qr, qr], axis=1), jnp.zeros((), q.dtype)))
        qs = jnp.concatenate(parts, axis=0)
    s = _dot_nt(qs, kt) * scale
    qpos = q0 + lax.broadcasted_iota(I32, (tq, slab), 0)
    kpos = start + lax.broadcasted_iota(I32, (tq, slab), 1)
    dist = qpos - kpos
    valid = (dist >= 0) & (dist < window)
    distf = dist.astype(F32)
    es, dens = [], []
    for r in range(nr):
        sr = s[r * tq:(r + 1) * tq] - slopes_ref[g * nr + r] * distf
        sr = jnp.where(valid, sr, NEG_INF)
        m = jnp.max(sr, axis=1, keepdims=True)
        if has_sinks:
            sk = sinks_ref[g * nr + r]
            m = jnp.maximum(m, sk)
        e = jnp.exp(sr - m)
        den = jnp.sum(e, axis=1, keepdims=True)
        if has_sinks:
            den = den + jnp.exp(sk - m)
        es.append(e.astype(BF16))
        dens.append(den)
    o = _dot(jnp.concatenate(es, axis=0), vt) / jnp.concatenate(dens, axis=0)
    if hd == LANES:
        o_ref[...] = jnp.concatenate([o[r * tq:(r + 1) * tq] for r in range(nr)], axis=1)
    else:
        og = jnp.where(g == 0, o[:, :hd], o[:, hd:])
        o_ref[...] = jnp.concatenate([og[r * tq:(r + 1) * tq] for r in range(nr)], axis=1)


def _banded_attn(main, slopes, sinks, bsz, seq, *, ng, nr, hd, window, q_col, k_col, v_col, name, tq=128):
    tq = min(tq, seq)
    nq = seq // tq
    n = bsz * seq
    slab = min(-(-(window - 1) // tq) * tq + tq, seq)
    qw = nr * hd
    has_sinks = sinks is not None
    smem = pl.BlockSpec(memory_space=pltpu.SMEM)
    if hd == LANES:
        k_spec = pl.BlockSpec((seq, LANES), lambda b, g, i: (b, k_col // LANES + g))
        v_spec = pl.BlockSpec((seq, LANES), lambda b, g, i: (b, v_col // LANES + g))
    else:
        k_spec = pl.BlockSpec((seq, LANES), lambda b, g, i: (b, k_col // LANES))
        v_spec = pl.BlockSpec((seq, LANES), lambda b, g, i: (b, v_col // LANES))
    in_specs = [smem] + ([smem] if has_sinks else []) + [
        pl.BlockSpec((tq, qw), lambda b, g, i: (b * nq + i, q_col // qw + g)), k_spec, v_spec]
    args = [slopes] + ([sinks.astype(F32)] if has_sinks else []) + [main, main, main]
    return pl.pallas_call(
        functools.partial(_banded_kernel, tq=tq, nr=nr, hd=hd, window=window, slab=slab,
                          scale=hd ** -0.5, has_sinks=has_sinks, seq=seq),
        grid=(bsz, ng, nq),
        in_specs=in_specs,
        out_specs=pl.BlockSpec((tq, qw), lambda b, g, i: (b * nq + i, g)),
        out_shape=jax.ShapeDtypeStruct((n, ng * qw), F32),
        compiler_params=_params(("parallel", "parallel", "parallel")),
        name=name,
    )(*args)


def _compress_kernel(x_ref, pos_ref, w1_ref, w2_ref, o_ref, xf_ref, *, seq):
    nc = seq // NSA_CMP_STRIDE
    hd = NSA_HEAD_DIM
    xf_ref[0:seq, :] = x_ref[...].astype(F32)
    xf_ref[seq:seq + NSA_CMP_STRIDE, :] = jnp.zeros((NSA_CMP_STRIDE, hd), F32)
    acc = jnp.zeros((nc, hd), F32)
    for j in range(NSA_CMP_LEN):
        rows = xf_ref[pl.ds(j, nc, stride=NSA_CMP_STRIDE), :] + pos_ref[0, j:j + 1, :]
        acc = acc + _dot(rows.astype(BF16), w1_ref[0, j * hd:(j + 1) * hd, :])
    hid = jax.nn.gelu(acc)
    o_ref[0, 0, 0] = _dot(hid.astype(BF16), w2_ref[0]).astype(BF16)


def _compress(main, pos, w1, w2, bsz, seq):
    nc = seq // NSA_CMP_STRIDE
    hd = NSA_HEAD_DIM
    ng = NSA_KV_HEADS
    col0 = MAIN_AKV // hd
    return pl.pallas_call(
        functools.partial(_compress_kernel, seq=seq),
        grid=(bsz, ng, 2),
        in_specs=[pl.BlockSpec((seq, hd), lambda b, g, t: (b, col0 + t * ng + g)),
                  pl.BlockSpec((1, NSA_CMP_LEN, hd), lambda b, g, t: (t, 0, 0)),
                  pl.BlockSpec((1, NSA_CMP_LEN * hd, hd), lambda b, g, t: (t, 0, 0)),
                  pl.BlockSpec((1, hd, hd), lambda b, g, t: (t, 0, 0))],
        out_specs=pl.BlockSpec((1, 1, 1, nc, hd), lambda b, g, t: (b, g, t, 0, 0)),
        out_shape=jax.ShapeDtypeStruct((bsz, ng, 2, nc, hd), BF16),
        scratch_shapes=[pltpu.VMEM((seq + NSA_CMP_STRIDE, hd), F32)],
        compiler_params=_params(("parallel", "parallel", "parallel")),
        name="nsa_compress",
    )(main, pos, w1, w2)


def _nsa_cmp_kernel(slopes_ref, q_ref, kc_ref, vc_ref, o_ref, sel_ref, *, tq, nc, nr):
    g = pl.program_id(1)
    qi = pl.program_id(2)
    hd = NSA_HEAD_DIM
    q0 = qi * tq
    q = q_ref[...]
    qs = jnp.concatenate([q[:, r * hd:(r + 1) * hd] for r in range(nr)], axis=0)
    kc = kc_ref[0, 0, 0]
    vc = vc_ref[0, 0, 0]
    s = _dot_nt(qs, kc) * (hd ** -0.5)
    tpos = q0 + lax.broadcasted_iota(I32, (tq, nc), 0)
    cend = lax.broadcasted_iota(I32, (tq, nc), 1) * NSA_CMP_STRIDE + (NSA_CMP_LEN - 1)
    dist = tpos - cend
    valid = dist >= 0
    distf = dist.astype(F32)
    ps = []
    psum = jnp.zeros((tq, nc), F32)
    for r in range(nr):
        sr = s[r * tq:(r + 1) * tq] - slopes_ref[g * nr + r] * distf
        sr = jnp.where(valid, sr, NEG_INF)
        m = jnp.max(sr, axis=1, keepdims=True)
        e = jnp.exp(sr - m)
        p = jnp.where(valid, e / jnp.sum(e, axis=1, keepdims=True), 0.0)
        psum = psum + p
        ps.append(p.astype(BF16))
    o = _dot(jnp.concatenate(ps, axis=0), vc)
    o_ref[...] = jnp.concatenate([o[r * tq:(r + 1) * tq] for r in range(nr)], axis=1)

    nb = LANES
    n_slc = nc * NSA_CMP_STRIDE // NSA_SEL_BLOCK
    per = NSA_SEL_BLOCK // NSA_CMP_STRIDE
    blk = lax.broadcasted_iota(I32, (nb, nc), 0)
    cidx = lax.broadcasted_iota(I32, (nb, nc), 1)
    overlap = (cidx <= per * blk + per - 1) & (cidx >= per * blk - 1) & (cidx < nc - 1) & (blk < n_slc)
    ov = jnp.where(overlap, 1.0, 0.0).astype(BF16)
    hi, mid, lo = _split3(psum)
    imp = (_dot_nt(ov, hi) + _dot_nt(ov, mid)) + _dot_nt(ov, lo)
    j = lax.broadcasted_iota(I32, (nb, tq), 0)
    cur = (q0 + lax.broadcasted_iota(I32, (nb, tq), 1)) // NSA_SEL_BLOCK
    forced = (j == 0) | (j == cur) | (j == cur - 1)
    imp = jnp.where(forced, 1e6, imp)
    imp = jnp.where(j > cur, -1e6, imp)
    imp = jnp.where(j >= n_slc, -3e38, imp)
    rank = jnp.zeros((nb, tq), F32)
    for i in range(n_slc):
        vi = imp[i:i + 1, :]
        ahead = (vi > imp) | ((vi == imp) & (j > i))
        rank = rank + jnp.where(ahead, 1.0, 0.0)
    sel = jnp.where(rank < float(min(NSA_TOP_N, n_slc)), 1.0, 0.0)
    sel_ref[...] = sel.T.astype(BF16)


def _nsa_cmp(main, kvc, slopes, bsz, seq, *, tq=128):
    tq = min(tq, seq)
    nq = seq // tq
    n = bsz * seq
    ng, nr, hd = NSA_KV_HEADS, NSA_HEADS // NSA_KV_HEADS, NSA_HEAD_DIM
    nc = seq // NSA_CMP_STRIDE
    qw = nr * hd
    smem = pl.BlockSpec(memory_space=pltpu.SMEM)
    return pl.pallas_call(
        functools.partial(_nsa_cmp_kernel, tq=tq, nc=nc, nr=nr),
        grid=(bsz, ng, nq),
        in_specs=[smem,
                  pl.BlockSpec((tq, qw), lambda b, g, i: (b * nq + i, g)),
                  pl.BlockSpec((1, 1, 1, nc, hd), lambda b, g, i: (b, g, 0, 0, 0)),
                  pl.BlockSpec((1, 1, 1, nc, hd), lambda b, g, i: (b, g, 1, 0, 0))],
        out_specs=[pl.BlockSpec((tq, qw), lambda b, g, i: (b * nq + i, g)),
                   pl.BlockSpec((tq, LANES), lambda b, g, i: ((b * ng + g) * nq + i, 0))],
        out_shape=[jax.ShapeDtypeStruct((n, ng * qw), F32),
                   jax.ShapeDtypeStruct((bsz * ng * seq, LANES), BF16)],
        compiler_params=_params(("parallel", "parallel", "parallel")),
        name="nsa_cmp",
    )(slopes, main, kvc, kvc)


def _nsa_slc_kernel(slopes_ref, q_ref, k_ref, v_ref, sel_ref, o_ref, m_sc, l_sc, acc_sc, *, tq, tk, nr):
    g = pl.program_id(1)
    qi = pl.program_id(2)
    hd = NSA_HEAD_DIM
    q0 = qi * tq
    q = q_ref[...]
    qs = jnp.concatenate([q[:, r * hd:(r + 1) * hd] for r in range(nr)], axis=0)
    sel = sel_ref[...]
    _flash_init(m_sc, l_sc, acc_sc)
    n_tiles = (q0 + tq - 1) // tk + 1

    def body(ki, carry):
        k0 = pl.multiple_of(ki * tk, tk)
        k = k_ref[pl.ds(k0, tk), :]
        v = v_ref[pl.ds(k0, tk), :]
        s = _dot_nt(qs, k) * (hd ** -0.5)
        blk = lax.broadcasted_iota(I32, (LANES, tk), 0)
        kblk = (k0 + lax.broadcasted_iota(I32, (LANES, tk), 1)) // NSA_SEL_BLOCK
        expand = jnp.where(blk == kblk, 1.0, 0.0).astype(BF16)
        picked = _dot(sel, expand) > 0.5
        dist = (q0 + lax.broadcasted_iota(I32, (tq, tk), 0)) - (k0 + lax.broadcasted_iota(I32, (tq, tk), 1))
        keep = picked & (dist >= 0)
        distf = dist.astype(F32)
        parts = []
        for r in range(nr):
            sr = s[r * tq:(r + 1) * tq] - slopes_ref[g * nr + r] * distf
            parts.append(jnp.where(keep, sr, NEG_INF))
        _flash_update(jnp.concatenate(parts, axis=0), v, m_sc, l_sc, acc_sc)
        return carry

    lax.fori_loop(0, n_tiles, body, 0)
    o = acc_sc[...] / l_sc[...]
    o_ref[...] = jnp.concatenate([o[r * tq:(r + 1) * tq] for r in range(nr)], axis=1)


def _nsa_slc(main, sel, slopes, bsz, seq, *, tq=128, tk=512):
    tq = min(tq, seq)
    tk = min(tk, seq)
    nq = seq // tq
    n = bsz * seq
    ng, nr, hd = NSA_KV_HEADS, NSA_HEADS // NSA_KV_HEADS, NSA_HEAD_DIM
    qw = nr * hd
    kb = MAIN_AKV // hd + 2 * ng
    vb = MAIN_AKV // hd + 3 * ng
    smem = pl.BlockSpec(memory_space=pltpu.SMEM)
    return pl.pallas_call(
        functools.partial(_nsa_slc_kernel, tq=tq, tk=tk, nr=nr),
        grid=(bsz, ng, nq),
        in_specs=[smem,
                  pl.BlockSpec((tq, qw), lambda b, g, i: (b * nq + i, g)),
                  pl.BlockSpec((seq, hd), lambda b, g, i: (b, kb + g)),
                  pl.BlockSpec((seq, hd), lambda b, g, i: (b, vb + g)),
                  pl.BlockSpec((tq, LANES), lambda b, g, i: ((b * ng + g) * nq + i, 0))],
        out_specs=pl.BlockSpec((tq, qw), lambda b, g, i: (b * nq + i, g)),
        out_shape=jax.ShapeDtypeStruct((n, ng * qw), F32),
        scratch_shapes=[pltpu.VMEM((nr * tq, 1), F32), pltpu.VMEM((nr * tq, 1), F32),
                        pltpu.VMEM((nr * tq, hd), F32)],
        compiler_params=_params(("parallel", "parallel", "arbitrary")),
        name="nsa_slc",
    )(slopes, main, main, main, sel)


def _nsa_gate_kernel(c_ref, s_ref, w_ref, g_ref, o_ref):
    gates = jax.nn.sigmoid(g_ref[...])
    hd = NSA_HEAD_DIM
    for h in range(NSA_HEADS):
        sl = slice(h * hd, (h + 1) * hd)
        lane = MISC_GATE_LANE + h
        o_ref[:, sl] = (gates[:, lane:lane + 1] * c_ref[:, sl]
                        + gates[:, lane + NSA_HEADS:lane + NSA_HEADS + 1] * s_ref[:, sl]
                        + gates[:, lane + 2 * NSA_HEADS:lane + 2 * NSA_HEADS + 1] * w_ref[:, sl])


def _nsa_gate(o_cmp, o_slc, o_win, small, *, tm=512):
    n, w = o_cmp.shape
    tm = min(tm, n)
    row = pl.BlockSpec((tm, w), lambda i: (i, 0))
    return pl.pallas_call(
        _nsa_gate_kernel,
        grid=(n // tm,),
        in_specs=[row, row, row, pl.BlockSpec((tm, LANES), lambda i: (i, SMALL_MISC // LANES))],
        out_specs=row,
        out_shape=jax.ShapeDtypeStruct((n, w), F32),
        compiler_params=_params(("parallel",)),
        name="nsa_gate",
    )(o_cmp, o_slc, o_win, small)


def _router_kernel(x_ref, g_ref, wr_ref, br_ref, xn_ref, route_ref, gate_ref, *, tm):
    x = x_ref[...]
    ms = jnp.mean(x * x, axis=-1, keepdims=True)
    xn = x * lax.rsqrt(ms + NORM_EPS) * g_ref[...]
    xn_ref[...] = xn
    logits = lax.dot_general(wr_ref[...], xn, (((1,), (1,)), ((), ())), precision=lax.Precision.HIGHEST,
                             preferred_element_type=F32) + br_ref[:, 0:1]
    ng, ne = N_GROUPS, EXPERTS_PER_GROUP
    lg = logits[0:ng, :]
    sub = lax.broadcasted_iota(I32, (ng, tm), 0)
    mg = jnp.max(lg, axis=0, keepdims=True)
    eg = jnp.exp(lg - mg)
    pg = eg / jnp.sum(eg, axis=0, keepdims=True)
    pg_top = jnp.max(pg, axis=0, keepdims=True)
    g_idx = jnp.min(jnp.where(pg == pg_top, sub, ng), axis=0, keepdims=True)
    le = jnp.zeros((ne, tm), F32)
    for gi in range(ng):
        le = jnp.where(g_idx == gi, logits[ng + gi * ne:ng + (gi + 1) * ne, :], le)
    v1 = jnp.max(le, axis=0, keepdims=True)
    i1 = jnp.min(jnp.where(le == v1, sub, ne), axis=0, keepdims=True)
    rest = jnp.where(sub == i1, -jnp.inf, le)
    v2 = jnp.max(rest, axis=0, keepdims=True)
    i2 = jnp.min(jnp.where(rest == v2, sub, ne), axis=0, keepdims=True)
    e2 = jnp.exp(v2 - v1)
    den = 1.0 + e2
    w1 = pg_top * (1.0 / den)
    w2 = pg_top * (e2 / den)
    ex1 = (g_idx * ne + i1).astype(F32)
    ex2 = (g_idx * ne + i2).astype(F32)
    zero = jnp.zeros((1, tm), F32)
    route = jnp.concatenate([ex1, ex2, w1, w2, zero, zero, zero, zero], axis=0)
    route_ref[...] = route
    pad = jnp.concatenate([route, jnp.zeros((LANES - 8, tm), F32)], axis=0)
    gate_ref[...] = jnp.concatenate([pad[:, i * LANES:(i + 1) * LANES].T for i in range(tm // LANES)], axis=0)


def _router(x, gain, rg_w, rg_b, re_w, re_b, *, tm=256):
    n, d = x.shape
    tm = min(tm, n)
    nl = N_GROUPS + N_EXPERTS
    wr = jnp.zeros((LANES, d), F32).at[:nl].set(jnp.concatenate([rg_w, re_w], axis=1).T.astype(F32))
    br = jnp.zeros((LANES, LANES), F32).at[:nl, :].set(
        jnp.concatenate([rg_b, re_b]).astype(F32)[:, None] * jnp.ones((1, LANES), F32))
    return pl.pallas_call(
        functools.partial(_router_kernel, tm=tm),
        grid=(n // tm,),
        in_specs=[pl.BlockSpec((tm, d), lambda i: (i, 0)),
                  pl.BlockSpec((1, d), lambda i: (0, 0)),
                  pl.BlockSpec((LANES, d), lambda i: (0, 0)),
                  pl.BlockSpec((LANES, LANES), lambda i: (0, 0))],
        out_specs=[pl.BlockSpec((tm, d), lambda i: (i, 0)),
                   pl.BlockSpec((8, tm), lambda i: (0, i)),
                   pl.BlockSpec((tm, LANES), lambda i: (i, 0))],
        out_shape=[jax.ShapeDtypeStruct((n, d), F32),
                   jax.ShapeDtypeStruct((8, n), F32),
                   jax.ShapeDtypeStruct((n, LANES), F32)],
        compiler_params=_params(("parallel",), 48),
        name="moe_router",
    )(x, gain.reshape(1, d).astype(F32), wr, br)


def _rank_kernel(route_ref, rank_ref, cnt_ref, carry_ref, *, tm):
    @pl.when(pl.program_id(0) == 0)
    def _():
        carry_ref[...] = jnp.zeros_like(carry_ref)

    e1 = route_ref[0:1, :].astype(I32)
    e2 = route_ref[1:2, :].astype(I32)
    sub = lax.broadcasted_iota(I32, (N_EXPERTS, tm), 0)
    oh1 = jnp.where(sub == e1, 1.0, 0.0)
    oh2 = jnp.where(sub == e2, 1.0, 0.0)
    ohs = oh1 + oh2
    row = lax.broadcasted_iota(I32, (tm, tm), 0)
    col = lax.broadcasted_iota(I32, (tm, tm), 1)
    upper = jnp.where(row < col, 1.0, 0.0).astype(BF16)
    before = _dot(ohs.astype(BF16), upper) + carry_ref[:, 0:1]
    r1 = jnp.sum(oh1 * before, axis=0, keepdims=True)
    r2 = jnp.sum(oh2 * before, axis=0, keepdims=True)
    zero = jnp.zeros((1, tm), F32)
    rank_ref[...] = jnp.concatenate([r1, r2, zero, zero, zero, zero, zero, zero], axis=0)
    carry_ref[...] = carry_ref[...] + jnp.sum(ohs, axis=1, keepdims=True)
    cnt_ref[...] = carry_ref[...]


def _rank(route, *, tm=512):
    n = route.shape[1]
    tm = min(tm, n)
    return pl.pallas_call(
        functools.partial(_rank_kernel, tm=tm),
        grid=(n // tm,),
        in_specs=[pl.BlockSpec((8, tm), lambda i: (0, i))],
        out_specs=[pl.BlockSpec((8, tm), lambda i: (0, i)),
                   pl.BlockSpec((N_EXPERTS, LANES), lambda i: (0, 0))],
        out_shape=[jax.ShapeDtypeStruct((8, n), F32), jax.ShapeDtypeStruct((N_EXPERTS, LANES), F32)],
        scratch_shapes=[pltpu.VMEM((N_EXPERTS, LANES), F32)],
        compiler_params=_params(("arbitrary",)),
        name="moe_rank",
    )(route)


def _experts_kernel(be_ref, br_ref, src_ref, srcn_ref, x_hbm, wg_ref, wu_ref, wd_ref, o_ref,
                    xbuf, xb16, sem, *, nblk):
    i = pl.program_id(0)
    c = pl.program_id(1)
    slot = i % 2
    rows = br_ref[i]

    def row_copy(src, tok, r, s):
        return pltpu.make_async_copy(x_hbm.at[pl.ds(tok, 1), :], xbuf.at[s, pl.ds(r, 1), :], sem.at[s])

    def start_rows(src, n_rows, s):
        def body(r, carry):
            row_copy(src, src[0, 0, r], r, s).start()
            return carry
        lax.fori_loop(0, n_rows, body, 0)

    def wait_rows(n_rows, s):
        def body(r, carry):
            row_copy(src_ref, 0, r, s).wait()
            return carry
        lax.fori_loop(0, n_rows, body, 0)

    @pl.when((i == 0) & (c == 0))
    def _():
        xbuf[...] = jnp.zeros(xbuf.shape, F32)
        start_rows(src_ref, rows, 0)

    @pl.when(c == 0)
    def _():
        wait_rows(rows, slot)
        xb16[...] = xbuf[slot].astype(BF16)
        o_ref[...] = jnp.zeros(o_ref.shape, F32)

    @pl.when((c == 1) & (i + 1 < nblk))
    def _():
        start_rows(srcn_ref, br_ref[jnp.minimum(i + 1, nblk - 1)], 1 - slot)

    @pl.when(rows > 0)
    def _():
        xb = xb16[...]
        wgu = jnp.concatenate([wg_ref[0].astype(BF16), wu_ref[0].astype(BF16)], axis=1)
        hgu = _dot(xb, wgu)
        h = jax.nn.silu(hgu[:, :DE_CHUNK]) * hgu[:, DE_CHUNK:]
        o_ref[...] += _dot(h.astype(BF16), wd_ref[0].astype(BF16))


def _experts(xn, src_tok, blk_e, blk_rows, w_gate, w_up, w_down):
    n, d = xn.shape
    nblk = blk_e.shape[0]
    nch = D_EXPERT // DE_CHUNK
    src3 = src_tok.reshape(nblk, 1, MOE_ROWS)
    grid_spec = pltpu.PrefetchScalarGridSpec(
        num_scalar_prefetch=2,
        grid=(nblk, nch),
        in_specs=[
            pl.BlockSpec((1, 1, MOE_ROWS), lambda i, c, be, br: (i, 0, 0), memory_space=pltpu.SMEM),
            pl.BlockSpec((1, 1, MOE_ROWS), lambda i, c, be, br: (jnp.minimum(i + 1, nblk - 1), 0, 0),
                         memory_space=pltpu.SMEM),
            pl.BlockSpec(memory_space=pl.ANY),
            pl.BlockSpec((1, d, DE_CHUNK), lambda i, c, be, br: (be[i], 0, c)),
            pl.BlockSpec((1, d, DE_CHUNK), lambda i, c, be, br: (be[i], 0, c)),
            pl.BlockSpec((1, DE_CHUNK, d), lambda i, c, be, br: (be[i], c, 0)),
        ],
        out_specs=pl.BlockSpec((MOE_ROWS, d), lambda i, c, be, br: (i, 0)),
        scratch_shapes=[pltpu.VMEM((2, MOE_ROWS, d), F32), pltpu.VMEM((MOE_ROWS, d), BF16),
                        pltpu.SemaphoreType.DMA((2,))],
    )
    return pl.pallas_call(
        functools.partial(_experts_kernel, nblk=nblk),
        grid_spec=grid_spec,
        out_shape=jax.ShapeDtypeStruct((nblk * MOE_ROWS, d), F32),
        compiler_params=_params(("arbitrary", "arbitrary"), 56),
        name="moe_experts",
    )(blk_e, blk_rows, src3, src3, xn, w_gate, w_up, w_down)


def _combine_kernel(pos_ref, y_hbm, x_ref, gate_ref, o_ref, ybuf, sem, *, tm):
    def row_copy(k, r, p):
        return pltpu.make_async_copy(y_hbm.at[pl.ds(p, 1), :], ybuf.at[k, pl.ds(r, 1), :], sem.at[0])

    def start(r, carry):
        row_copy(0, r, pos_ref[0, 0, r]).start()
        row_copy(1, r, pos_ref[0, 0, tm + r]).start()
        return carry

    def wait(r, carry):
        row_copy(0, r, 0).wait()
        row_copy(1, r, 0).wait()
        return carry

    lax.fori_loop(0, tm, start, 0)
    lax.fori_loop(0, tm, wait, 0)
    g = gate_ref[...]
    o_ref[...] = x_ref[...] + (g[:, 2:3] * ybuf[0] + g[:, 3:4] * ybuf[1])


def _combine(yb, pos, x, gates, *, tm=256):
    n, d = x.shape
    tm = min(tm, n)
    nt = n // tm
    pos3 = jnp.concatenate([pos[0].reshape(nt, 1, tm), pos[1].reshape(nt, 1, tm)], axis=2)
    return pl.pallas_call(
        functools.partial(_combine_kernel, tm=tm),
        grid=(nt,),
        in_specs=[pl.BlockSpec((1, 1, 2 * tm), lambda i: (i, 0, 0), memory_space=pltpu.SMEM),
                  pl.BlockSpec(memory_space=pl.ANY),
                  pl.BlockSpec((tm, d), lambda i: (i, 0)),
                  pl.BlockSpec((tm, LANES), lambda i: (i, 0))],
        out_specs=pl.BlockSpec((tm, d), lambda i: (i, 0)),
        out_shape=jax.ShapeDtypeStruct((n, d), F32),
        scratch_shapes=[pltpu.VMEM((2, tm, d), F32), pltpu.SemaphoreType.DMA((1,))],
        compiler_params=_params(("arbitrary",), 48),
        name="moe_combine",
    )(pos3, yb, x, gates)


def _final_norm_kernel(x_ref, g_ref, o_ref):
    x = x_ref[...]
    ms = jnp.mean(x * x, axis=-1, keepdims=True)
    o_ref[...] = x * lax.rsqrt(ms + NORM_EPS) * g_ref[...]


def _final_norm(x, gain, *, tm=512):
    n, d = x.shape
    tm = min(tm, n)
    return pl.pallas_call(
        _final_norm_kernel,
        grid=(n // tm,),
        in_specs=[pl.BlockSpec((tm, d), lambda i: (i, 0)), pl.BlockSpec((1, d), lambda i: (0, 0))],
        out_specs=pl.BlockSpec((tm, d), lambda i: (i, 0)),
        out_shape=jax.ShapeDtypeStruct((n, d), F32),
        compiler_params=_params(("parallel",)),
        name="final_norm",
    )(x, gain.reshape(1, d).astype(F32))


def _in_proj_weights(w_in):
    off = np.concatenate([[0], np.cumsum(IN_SPLITS)])
    seg = lambda i: w_in[:, off[i]:off[i + 1]]
    w_main = jnp.concatenate([seg(0), seg(1), seg(3), seg(8), seg(9)], axis=1).astype(BF16)
    pad = jnp.zeros((w_in.shape[0], SMALL_COLS - (SMALL_MISC + 64 + 24 + 8)), w_in.dtype)
    w_small = jnp.concatenate([seg(5), seg(6), seg(7), seg(2), seg(4), pad], axis=1).astype(BF16)
    return w_main, w_small


def _mla_weights(w_uq, w_ukv):
    per = MLA_NOPE_DIM + MLA_ROPE_DIM
    w3 = w_uq.reshape(MLA_Q_RANK, MLA_HEADS, per)
    nope = w3[:, :, :MLA_NOPE_DIM].reshape(MLA_Q_RANK, MLA_HEADS * MLA_NOPE_DIM)
    rope = jnp.pad(w3[:, :, MLA_NOPE_DIM:], ((0, 0), (0, 0), (0, LANES - MLA_ROPE_DIM)))
    rope = rope.reshape(MLA_Q_RANK, MLA_HEADS * LANES)
    return jnp.concatenate([nope, rope], axis=1).astype(BF16), w_ukv.astype(BF16)


def _moe_tables(route, rank, counts, n_tok):
    n_assign = n_tok * TOP_K
    nblk = (n_assign + N_EXPERTS * (MOE_ROWS - 1)) // MOE_ROWS
    cnt = counts[:, 0].astype(I32)
    pcnt = (cnt + MOE_ROWS - 1) // MOE_ROWS * MOE_ROWS
    pend = jnp.cumsum(pcnt)
    pstart = pend - pcnt
    expert = route[0:2].astype(I32)
    pos = pstart[expert] + rank[0:2].astype(I32)
    tok = jnp.broadcast_to(jnp.arange(n_tok, dtype=I32)[None, :], (2, n_tok))
    src_tok = jnp.zeros((nblk * MOE_ROWS,), I32).at[pos.reshape(-1)].set(tok.reshape(-1))
    row0 = jnp.arange(nblk, dtype=I32) * MOE_ROWS
    blk_e = jnp.minimum(jnp.searchsorted(pend, row0, side="right"), N_EXPERTS - 1).astype(I32)
    blk_rows = jnp.clip(cnt[blk_e] - (row0 - pstart[blk_e]), 0, MOE_ROWS).astype(I32)
    return pos, src_tok, blk_e, blk_rows


def kernel(x, norm_mix_g, w_in, nsa_kc_pos, nsa_kc_w1, nsa_kc_w2, nsa_vc_pos, nsa_vc_w1, nsa_vc_w2, fox_f_bias,
           mla_q_norm_g, mla_kv_norm_g, mla_w_uq, mla_w_ukv, swa_sinks, out_norm_g, w_out, norm_ffn_g,
           router_group_w, router_group_b, router_expert_w, router_expert_b, exp_w_gate, exp_w_up, exp_w_down,
           final_norm_g):
    bsz, seq, d_model = x.shape
    n = bsz * seq
    depth = w_in.shape[0]
    xs = x.reshape(n, d_model).astype(F32)
    nsa_slopes = _alibi_slopes(NSA_HEADS)
    swa_slopes = _alibi_slopes(SWA_HEADS)
    nsa_nr = NSA_HEADS // NSA_KV_HEADS
    for l in range(depth):
        w_main, w_small = _in_proj_weights(w_in[l])
        main = _norm_matmul(xs, norm_mix_g[l], w_main, out_dtype=BF16, tn=768, name="in_proj_main")
        small = _norm_matmul(xs, norm_mix_g[l], w_small, out_dtype=F32, tn=768, name="in_proj_small")

        pos = jnp.stack([nsa_kc_pos[l], nsa_vc_pos[l]]).astype(F32)
        w1 = jnp.stack([nsa_kc_w1[l], nsa_vc_w1[l]]).astype(BF16)
        w2 = jnp.stack([nsa_kc_w2[l], nsa_vc_w2[l]]).astype(BF16)
        kvc = _compress(main, pos, w1, w2, bsz, seq)
        o_cmp, sel = _nsa_cmp(main, kvc, nsa_slopes, bsz, seq)
        o_slc = _nsa_slc(main, sel, nsa_slopes, bsz, seq)
        o_win = _banded_attn(main, nsa_slopes, None, bsz, seq, ng=NSA_KV_HEADS, nr=nsa_nr, hd=NSA_HEAD_DIM,
                             window=NSA_WINDOW, q_col=MAIN_AQ, k_col=MAIN_AKV + 8 * NSA_HEAD_DIM,
                             v_col=MAIN_AKV + 10 * NSA_HEAD_DIM, name="nsa_win")
        out_a = _nsa_gate(o_cmp, o_slc, o_win, small)

        cum, cumt = _fox_prep(small, fox_f_bias[l], bsz, seq)
        out_b = _fox_attn(main, cum, cumt, bsz, seq)

        w_uq, w_ukv = _mla_weights(mla_w_uq[l], mla_w_ukv[l])
        qup = _norm_matmul(small, mla_q_norm_g[l], w_uq, out_dtype=F32, col_off=SMALL_CQ, k=MLA_Q_RANK,
                           name="mla_q_up")
        kvup = _norm_matmul(small, mla_kv_norm_g[l], w_ukv, out_dtype=BF16, col_off=SMALL_CKV, k=MLA_KV_RANK,
                            name="mla_kv_up")
        qrope, krope = _rope(qup, small, bsz, seq)
        out_c = _mla_attn(qup, qrope, kvup, krope, bsz, seq)

        out_d = _banded_attn(main, swa_slopes, swa_sinks[l], bsz, seq, ng=SWA_KV_HEADS,
                             nr=SWA_HEADS // SWA_KV_HEADS, hd=SWA_HEAD_DIM, window=SWA_WINDOW, q_col=MAIN_DQ,
                             k_col=MAIN_DKV, v_col=MAIN_DKV + LANES, name="swa")

        xs = _out_proj((out_a, out_b, out_c, out_d), out_norm_g[l], w_out[l].astype(BF16), xs)

        xn, route, gates = _router(xs, norm_ffn_g[l], router_group_w[l], router_group_b[l],
                                   router_expert_w[l], router_expert_b[l])
        rank, counts = _rank(route)
        pos_rows, src_tok, blk_e, blk_rows = _moe_tables(route, rank, counts, n)
        yb = _experts(xn, src_tok, blk_e, blk_rows, exp_w_gate[l], exp_w_up[l], exp_w_down[l])
        xs = _combine(yb, pos_rows, xs, gates)
    return _final_norm(xs, final_norm_g).reshape(bsz, seq, d_model)
```

```python
import functools
import math

import numpy as np
import jax
import jax.numpy as jnp
from jax import lax
from jax.experimental import pallas as pl
from jax.experimental.pallas import tpu as pltpu

F32 = jnp.float32
BF16 = jnp.bfloat16
I32 = jnp.int32

NEG_INF = -1e30
NORM_EPS = 1e-6
LANES = 128
ROW_CHUNK = 128
LOG2E = 1.4426950408889634

Q_BLOCK = 128
GROUP_WIDTH = 1024
NSA_HEADS, NSA_KV_HEADS, NSA_HEAD_DIM = 8, 2, 128
NSA_CMP_STRIDE, NSA_CMP_LEN, NSA_SEL_BLOCK, NSA_TOP_N, NSA_WINDOW = 16, 32, 64, 16, 512
FOX_HEADS, FOX_HEAD_DIM = 8, 128
MLA_HEADS, MLA_Q_RANK, MLA_KV_RANK, MLA_NOPE_DIM, MLA_ROPE_DIM, MLA_V_DIM = 8, 768, 512, 128, 64, 128
ROPE_THETA = 10000.0
SWA_HEADS, SWA_KV_HEADS, SWA_HEAD_DIM, SWA_WINDOW = 16, 2, 64, 128
N_GROUPS, EXPERTS_PER_GROUP, TOP_K, D_EXPERT = 8, 8, 2, 384
N_EXPERTS = N_GROUPS * EXPERTS_PER_GROUP
IN_SPLITS = (1024, 1536, 24, 3072, 8, 768, 512, 64, 1024, 256)

MAIN_AQ, MAIN_AKV, MAIN_BQKV, MAIN_DQ, MAIN_DKV, MAIN_COLS = 0, 1024, 2560, 5632, 6656, 6912
SMALL_CQ, SMALL_CKV, SMALL_MISC, SMALL_COLS = 0, 768, 1280, 1536
MISC_GATE_LANE, MISC_FORGET_LANE = 64, 88

MOE_ROWS = 512
MOE_SUB_ROWS = 256
DE_CHUNK = 128


def _dot(a, b):
    return jnp.dot(a, b, preferred_element_type=F32)


def _dot_nt(a, b):
    return lax.dot_general(a, b, (((1,), (1,)), ((), ())), preferred_element_type=F32)


def _alibi_slopes(n_heads):
    return jnp.exp2(-8.0 * jnp.arange(1, n_heads + 1, dtype=F32) / n_heads)


def _split3(x):
    hi = x.astype(BF16)
    r1 = x - hi.astype(F32)
    mid = r1.astype(BF16)
    lo = (r1 - mid.astype(F32)).astype(BF16)
    return hi, mid, lo


def _params(sem, vmem_mb=None):
    kw = dict(dimension_semantics=sem)
    if vmem_mb is not None:
        kw["vmem_limit_bytes"] = vmem_mb * 1024 * 1024
    return pltpu.CompilerParams(**kw)


def _norm_matmul_kernel(x_ref, g_ref, w_ref, o_ref, xn_ref, *, col_off, k):
    @pl.when(pl.program_id(1) == 0)
    def _():
        x = x_ref[:, col_off:col_off + k].astype(F32)
        ms = jnp.mean(x * x, axis=-1, keepdims=True)
        xn_ref[...] = (x * lax.rsqrt(ms + NORM_EPS) * g_ref[...]).astype(BF16)

    o_ref[...] = _dot(xn_ref[...], w_ref[...]).astype(o_ref.dtype)


def _norm_matmul(x, gain, w, *, out_dtype, col_off=0, k=None, tm=512, tn=512, name="norm_matmul"):
    m, kfull = x.shape
    k = kfull if k is None else k
    n = w.shape[1]
    tm = min(tm, m)
    assert m % tm == 0 and n % tn == 0 and w.shape[0] == k
    return pl.pallas_call(
        functools.partial(_norm_matmul_kernel, col_off=col_off, k=k),
        grid=(m // tm, n // tn),
        in_specs=[
            pl.BlockSpec((tm, kfull), lambda i, j: (i, 0)),
            pl.BlockSpec((1, k), lambda i, j: (0, 0)),
            pl.BlockSpec((k, tn), lambda i, j: (0, j)),
        ],
        out_specs=pl.BlockSpec((tm, tn), lambda i, j: (i, j)),
        out_shape=jax.ShapeDtypeStruct((m, n), out_dtype),
        scratch_shapes=[pltpu.VMEM((tm, k), BF16)],
        compiler_params=_params(("parallel", "arbitrary"), 56),
        name=name,
    )(x, gain.reshape(1, k).astype(F32), w)


def _out_proj_kernel(a_ref, b_ref, c_ref, d_ref, g_ref, w_ref, r_ref, o_ref, xn_ref):
    @pl.when(pl.program_id(1) == 0)
    def _():
        for i, ref in enumerate((a_ref, b_ref, c_ref, d_ref)):
            x = ref[...]
            ms = jnp.mean(x * x, axis=-1, keepdims=True)
            g = g_ref[:, i * GROUP_WIDTH:(i + 1) * GROUP_WIDTH]
            xn_ref[:, i * GROUP_WIDTH:(i + 1) * GROUP_WIDTH] = (x * lax.rsqrt(ms + NORM_EPS) * g).astype(BF16)

    o_ref[...] = r_ref[...] + _dot(xn_ref[...], w_ref[...])


def _out_proj(outs, gain, w, resid, *, tm=512, tn=512):
    m, d = resid.shape
    tm = min(tm, m)
    k = 4 * GROUP_WIDTH
    grp = pl.BlockSpec((tm, GROUP_WIDTH), lambda i, j: (i, 0))
    return pl.pallas_call(
        _out_proj_kernel,
        grid=(m // tm, d // tn),
        in_specs=[grp, grp, grp, grp,
                  pl.BlockSpec((1, k), lambda i, j: (0, 0)),
                  pl.BlockSpec((k, tn), lambda i, j: (0, j)),
                  pl.BlockSpec((tm, tn), lambda i, j: (i, j))],
        out_specs=pl.BlockSpec((tm, tn), lambda i, j: (i, j)),
        out_shape=jax.ShapeDtypeStruct((m, d), F32),
        scratch_shapes=[pltpu.VMEM((tm, k), BF16)],
        compiler_params=_params(("parallel", "arbitrary"), 56),
        name="out_proj",
    )(*outs, gain.reshape(1, k).astype(F32), w, resid)


def _fox_prep_kernel(x_ref, b_ref, cumt_ref, carry_ref, *, ts):
    @pl.when(pl.program_id(1) == 0)
    def _():
        carry_ref[...] = jnp.zeros_like(carry_ref)

    z = x_ref[...] + b_ref[...]
    lf = jnp.minimum(z, 0.0) - jnp.log1p(jnp.exp(-jnp.abs(z)))
    row = lax.broadcasted_iota(I32, (ts, ts), 0)
    col = lax.broadcasted_iota(I32, (ts, ts), 1)
    tri = jnp.where(col <= row, 1.0, 0.0).astype(BF16)
    hi, mid, lo = _split3(lf)
    cum = (_dot(tri, hi) + _dot(tri, mid)) + _dot(tri, lo) + carry_ref[...]
    carry_ref[...] = cum[ts - 1:ts, :]
    cum_t = jnp.concatenate([cum[i * LANES:(i + 1) * LANES, :].T for i in range(ts // LANES)], axis=1)
    cumt_ref[0] = cum_t[MISC_FORGET_LANE:MISC_FORGET_LANE + FOX_HEADS, :]


def _fox_prep(small, f_bias, bsz, seq, *, ts=512):
    ts = min(ts, seq)
    n = bsz * seq
    ns = seq // ts
    bias = jnp.zeros((1, LANES), F32).at[0, MISC_FORGET_LANE:MISC_FORGET_LANE + FOX_HEADS].set(f_bias.astype(F32))
    misc_blk = SMALL_MISC // LANES
    return pl.pallas_call(
        functools.partial(_fox_prep_kernel, ts=ts),
        grid=(bsz, ns),
        in_specs=[pl.BlockSpec((ts, LANES), lambda b, i: (b * ns + i, misc_blk)),
                  pl.BlockSpec((1, LANES), lambda b, i: (0, 0))],
        out_specs=pl.BlockSpec((1, FOX_HEADS, ts), lambda b, i: (b, 0, i)),
        out_shape=jax.ShapeDtypeStruct((bsz, FOX_HEADS, seq), F32),
        scratch_shapes=[pltpu.VMEM((1, LANES), F32)],
        compiler_params=_params(("parallel", "arbitrary")),
        name="fox_prep",
    )(small, bias)


def _flash_rows(s2, v, m_sc, l_sc, acc_sc, r0, rows):
    sl = slice(r0, r0 + rows)
    m_prev = m_sc[sl]
    m_new = jnp.maximum(m_prev, jnp.max(s2, axis=1, keepdims=True))
    alpha = jnp.exp2(m_prev - m_new)
    p = jnp.exp2(s2 - m_new)
    l_sc[sl] = alpha * l_sc[sl] + jnp.sum(p, axis=1, keepdims=True)
    acc_sc[sl] = alpha * acc_sc[sl] + _dot(p.astype(BF16), v)
    m_sc[sl] = m_new


def _flash_init(m_sc, l_sc, acc_sc):
    m_sc[...] = jnp.full(m_sc.shape, NEG_INF, F32)
    l_sc[...] = jnp.zeros(l_sc.shape, F32)
    acc_sc[...] = jnp.zeros(acc_sc.shape, F32)


def _pipelined_sweep(n_full, issue, softmax_pv, last_tile, sa_sc, sb_sc):
    def full_tile(ki, src, dst):
        issue(ki + 1, dst)
        softmax_pv(ki, src)

    issue(0, sa_sc)

    def pair(j, carry):
        full_tile(2 * j, sa_sc, sb_sc)
        full_tile(2 * j + 1, sb_sc, sa_sc)
        return carry

    lax.fori_loop(0, n_full // 2, pair, 0)

    @pl.when(n_full % 2 == 1)
    def _():
        full_tile(n_full - 1, sa_sc, sb_sc)
        last_tile(sb_sc)

    @pl.when(n_full % 2 == 0)
    def _():
        last_tile(sa_sc)


def _causal_sweep(qi, tq, score_fn, load_v, sa_sc, sb_sc, m_sc, l_sc, acc_sc):
    nch = tq // ROW_CHUNK
    chunk = lambda c: slice(c * ROW_CHUNK, (c + 1) * ROW_CHUNK)

    def issue(ki, dst):
        for c in range(nch):
            dst[chunk(c), :] = score_fn(c, ki)

    def softmax_pv(ki, src):
        v = load_v(ki, tq)
        for c in range(nch):
            _flash_rows(src[chunk(c), :], v, m_sc, l_sc, acc_sc, c * ROW_CHUNK, ROW_CHUNK)

    def diag_tile(src):
        for c in range(nch):
            ncols = (c + 1) * ROW_CHUNK
            row = lax.broadcasted_iota(I32, (ROW_CHUNK, ncols), 0) + c * ROW_CHUNK
            col = lax.broadcasted_iota(I32, (ROW_CHUNK, ncols), 1)
            s2 = jnp.where(col <= row, src[chunk(c), 0:ncols], NEG_INF)
            _flash_rows(s2, load_v(qi, ncols), m_sc, l_sc, acc_sc, c * ROW_CHUNK, ROW_CHUNK)

    _pipelined_sweep(qi, issue, softmax_pv, diag_tile, sa_sc, sb_sc)


def _fox_attn_kernel(q_ref, k_ref, v_ref, ck_ref, o_ref, sa_sc, sb_sc, m_sc, l_sc, acc_sc, *, tq, scale):
    qi = pl.program_id(2)
    _flash_init(m_sc, l_sc, acc_sc)
    c0 = ck_ref[0, pl.ds(qi, 1), :][:, 0:1]

    def score(c, ki):
        k0 = pl.multiple_of(ki * tq, tq)
        k = k_ref[pl.ds(k0, tq), :]
        ck2 = (ck_ref[0, pl.ds(ki, 1), :] - c0) * LOG2E
        q = q_ref[c * ROW_CHUNK:(c + 1) * ROW_CHUNK, :]
        return _dot_nt(q, k) * (scale * LOG2E) - ck2

    def load_v(ki, ncols):
        return v_ref[pl.ds(pl.multiple_of(ki * tq, tq), ncols), :]

    _causal_sweep(qi, tq, score, load_v, sa_sc, sb_sc, m_sc, l_sc, acc_sc)
    o_ref[...] = acc_sc[...] / l_sc[...]


def _fox_attn(main, cumt, bsz, seq, *, tq=512):
    tq = min(tq, seq)
    nq = seq // tq
    n = bsz * seq
    d = FOX_HEAD_DIM
    qb, kb, vb = MAIN_BQKV // d, MAIN_BQKV // d + FOX_HEADS, MAIN_BQKV // d + 2 * FOX_HEADS
    cumt3 = cumt.reshape(bsz * FOX_HEADS, nq, tq)
    return pl.pallas_call(
        functools.partial(_fox_attn_kernel, tq=tq, scale=d ** -0.5),
        grid=(bsz, FOX_HEADS, nq),
        in_specs=[pl.BlockSpec((tq, d), lambda b, h, i: (b * nq + i, qb + h)),
                  pl.BlockSpec((seq, d), lambda b, h, i: (b, kb + h)),
                  pl.BlockSpec((seq, d), lambda b, h, i: (b, vb + h)),
                  pl.BlockSpec((1, nq, tq), lambda b, h, i: (b * FOX_HEADS + h, 0, 0))],
        out_specs=pl.BlockSpec((tq, d), lambda b, h, i: (b * nq + i, h)),
        out_shape=jax.ShapeDtypeStruct((n, FOX_HEADS * d), F32),
        scratch_shapes=[pltpu.VMEM((tq, tq), F32), pltpu.VMEM((tq, tq), F32),
                        pltpu.VMEM((tq, 1), F32), pltpu.VMEM((tq, 1), F32), pltpu.VMEM((tq, d), F32)],
        compiler_params=_params(("parallel", "parallel", "arbitrary")),
        name="fox_attn",
    )(main, main, main, cumt3)


def _mla_attn_kernel(qn_ref, qr_ref, kn_ref, kr_ref, v_ref, o_ref, sa_sc, sb_sc, m_sc, l_sc, acc_sc, *, tq, scale):
    qi = pl.program_id(2)
    _flash_init(m_sc, l_sc, acc_sc)

    def score(c, ki):
        k0 = pl.multiple_of(ki * tq, tq)
        k = jnp.concatenate([kn_ref[pl.ds(k0, tq), :], kr_ref[pl.ds(k0, tq), :]], axis=1)
        rows = slice(c * ROW_CHUNK, (c + 1) * ROW_CHUNK)
        q = jnp.concatenate([qn_ref[rows, :].astype(BF16), qr_ref[rows, :]], axis=1)
        return _dot_nt(q, k) * (scale * LOG2E)

    def load_v(ki, ncols):
        return v_ref[pl.ds(pl.multiple_of(ki * tq, tq), ncols), :]

    _causal_sweep(qi, tq, score, load_v, sa_sc, sb_sc, m_sc, l_sc, acc_sc)
    o_ref[...] = acc_sc[...] / l_sc[...]


def _mla_attn(qup, qrope, kvup, krope, bsz, seq, *, tq=512):
    tq = min(tq, seq)
    nq = seq // tq
    n = bsz * seq
    d = LANES
    return pl.pallas_call(
        functools.partial(_mla_attn_kernel, tq=tq, scale=(MLA_NOPE_DIM + MLA_ROPE_DIM) ** -0.5),
        grid=(bsz, MLA_HEADS, nq),
        in_specs=[pl.BlockSpec((tq, d), lambda b, h, i: (b * nq + i, h)),
                  pl.BlockSpec((tq, d), lambda b, h, i: (b * nq + i, h)),
                  pl.BlockSpec((seq, d), lambda b, h, i: (b, 2 * h)),
                  pl.BlockSpec((seq, d), lambda b, h, i: (b, 0)),
                  pl.BlockSpec((seq, d), lambda b, h, i: (b, 2 * h + 1))],
        out_specs=pl.BlockSpec((tq, d), lambda b, h, i: (b * nq + i, h)),
        out_shape=jax.ShapeDtypeStruct((n, MLA_HEADS * MLA_V_DIM), F32),
        scratch_shapes=[pltpu.VMEM((tq, tq), F32), pltpu.VMEM((tq, tq), F32),
                        pltpu.VMEM((tq, 1), F32), pltpu.VMEM((tq, 1), F32), pltpu.VMEM((tq, d), F32)],
        compiler_params=_params(("parallel", "parallel", "arbitrary")),
        name="mla_attn",
    )(qup, qrope, kvup, krope, kvup)


def _rope_kernel(q_ref, k_ref, cos_ref, sin_ref, qo_ref, ko_ref):
    half = MLA_ROPE_DIM // 2

    def rope(x):
        lane = lax.broadcasted_iota(I32, x.shape, 1)
        first = (lane % MLA_ROPE_DIM) < half
        swapped = jnp.where(first, pltpu.roll(x, LANES - half, 1), pltpu.roll(x, half, 1))
        return x * cos_ref[...] + swapped * sin_ref[...]

    for t in range(q_ref.shape[1] // LANES):
        qo_ref[:, t * LANES:(t + 1) * LANES] = rope(q_ref[:, t * LANES:(t + 1) * LANES]).astype(BF16)
    k = rope(k_ref[...])
    lane = lax.broadcasted_iota(I32, k.shape, 1)
    ko_ref[...] = jnp.where(lane < MLA_ROPE_DIM, k, 0.0).astype(BF16)


def _rope(qup, small, bsz, seq, *, ts=512):
    ts = min(ts, seq)
    ns = seq // ts
    n = bsz * seq
    pos = jnp.arange(seq, dtype=F32)
    inv = ROPE_THETA ** (-jnp.arange(0, MLA_ROPE_DIM, 2, dtype=F32) / MLA_ROPE_DIM)
    ang = pos[:, None] * inv[None, :]
    cos, sin = jnp.cos(ang), jnp.sin(ang)
    cos_t = jnp.concatenate([cos, cos, cos, cos], axis=1)
    sin_t = jnp.concatenate([-sin, sin, -sin, sin], axis=1)
    w = MLA_HEADS * LANES
    return pl.pallas_call(
        _rope_kernel,
        grid=(bsz, ns),
        in_specs=[pl.BlockSpec((ts, w), lambda b, i: (b * ns + i, 1)),
                  pl.BlockSpec((ts, LANES), lambda b, i: (b * ns + i, SMALL_MISC // LANES)),
                  pl.BlockSpec((ts, LANES), lambda b, i: (i, 0)),
                  pl.BlockSpec((ts, LANES), lambda b, i: (i, 0))],
        out_specs=[pl.BlockSpec((ts, w), lambda b, i: (b * ns + i, 0)),
                   pl.BlockSpec((ts, LANES), lambda b, i: (b * ns + i, 0))],
        out_shape=[jax.ShapeDtypeStruct((n, w), BF16), jax.ShapeDtypeStruct((n, LANES), BF16)],
        compiler_params=_params(("parallel", "parallel")),
        name="mla_rope",
    )(qup, small, cos_t, sin_t)


def _banded_kernel(*refs, tq, nr, hd, window, slab, scale, has_sinks, seq):
    if has_sinks:
        slopes_ref, sinks_ref, q_ref, k_ref, v_ref, o_ref = refs
    else:
        slopes_ref, q_ref, k_ref, v_ref, o_ref = refs
    g = pl.program_id(1)
    qi = pl.program_id(2)
    q0 = qi * tq
    start = pl.multiple_of(jnp.minimum(jnp.maximum(q0 + tq - slab, 0), seq - slab), tq)
    kt = k_ref[pl.ds(start, slab), :]
    vt = v_ref[pl.ds(start, slab), :]
    q = q_ref[...]
    if hd == LANES:
        qs = jnp.concatenate([q[:, r * hd:(r + 1) * hd] for r in range(nr)], axis=0)
    else:
        lane = lax.broadcasted_iota(I32, (tq, LANES), 1)
        mine = (lane // hd) == g
        parts = []
        for r in range(nr):
            qr = q[:, r * hd:(r + 1) * hd]
            parts.append(jnp.where(mine, jnp.concatenate([qr, qr], axis=1), jnp.zeros((), q.dtype)))
        qs = jnp.concatenate(parts, axis=0)
    s = _dot_nt(qs, kt) * scale
    qpos = q0 + lax.broadcasted_iota(I32, (tq, slab), 0)
    kpos = start + lax.broadcasted_iota(I32, (tq, slab), 1)
    dist = qpos - kpos
    valid = (dist >= 0) & (dist < window)
    distf = dist.astype(F32)
    es, dens = [], []
    for r in range(nr):
        sr = s[r * tq:(r + 1) * tq] - slopes_ref[g * nr + r] * distf
        sr = jnp.where(valid, sr, NEG_INF)
        m = jnp.max(sr, axis=1, keepdims=True)
        if has_sinks:
            sk = sinks_ref[g * nr + r]
            m = jnp.maximum(m, sk)
        e = jnp.exp(sr - m)
        den = jnp.sum(e, axis=1, keepdims=True)
        if has_sinks:
            den = den + jnp.exp(sk - m)
        es.append(e.astype(BF16))
        dens.append(den)
    o = _dot(jnp.concatenate(es, axis=0), vt) / jnp.concatenate(dens, axis=0)
    if hd == LANES:
        o_ref[...] = jnp.concatenate([o[r * tq:(r + 1) * tq] for r in range(nr)], axis=1)
    else:
        og = jnp.where(g == 0, o[:, :hd], o[:, hd:])
        o_ref[...] = jnp.concatenate([og[r * tq:(r + 1) * tq] for r in range(nr)], axis=1)


def _banded_attn(main, slopes, sinks, bsz, seq, *, ng, nr, hd, window, q_col, k_col, v_col, name, tq=128):
    tq = min(tq, seq)
    nq = seq // tq
    n = bsz * seq
    slab = min(-(-(window - 1) // tq) * tq + tq, seq)
    qw = nr * hd
    has_sinks = sinks is not None
    smem = pl.BlockSpec(memory_space=pltpu.SMEM)
    if hd == LANES:
        k_spec = pl.BlockSpec((seq, LANES), lambda b, g, i: (b, k_col // LANES + g))
        v_spec = pl.BlockSpec((seq, LANES), lambda b, g, i: (b, v_col // LANES + g))
    else:
        k_spec = pl.BlockSpec((seq, LANES), lambda b, g, i: (b, k_col // LANES))
        v_spec = pl.BlockSpec((seq, LANES), lambda b, g, i: (b, v_col // LANES))
    in_specs = [smem] + ([smem] if has_sinks else []) + [
        pl.BlockSpec((tq, qw), lambda b, g, i: (b * nq + i, q_col // qw + g)), k_spec, v_spec]
    args = [slopes] + ([sinks.astype(F32)] if has_sinks else []) + [main, main, main]
    return pl.pallas_call(
        functools.partial(_banded_kernel, tq=tq, nr=nr, hd=hd, window=window, slab=slab,
                          scale=hd ** -0.5, has_sinks=has_sinks, seq=seq),
        grid=(bsz, ng, nq),
        in_specs=in_specs,
        out_specs=pl.BlockSpec((tq, qw), lambda b, g, i: (b * nq + i, g)),
        out_shape=jax.ShapeDtypeStruct((n, ng * qw), F32),
        compiler_params=_params(("parallel", "parallel", "parallel")),
        name=name,
    )(*args)


def _compress_kernel(x_ref, pos_ref, w1_ref, w2_ref, o_ref, xf_ref, *, seq):
    nc = seq // NSA_CMP_STRIDE
    hd = NSA_HEAD_DIM
    xf_ref[0:seq, :] = x_ref[...].astype(F32)
    xf_ref[seq:seq + NSA_CMP_STRIDE, :] = jnp.zeros((NSA_CMP_STRIDE, hd), F32)
    acc = jnp.zeros((nc, hd), F32)
    for j in range(NSA_CMP_LEN):
        rows = xf_ref[pl.ds(j, nc, stride=NSA_CMP_STRIDE), :] + pos_ref[0, j:j + 1, :]
        acc = acc + _dot(rows.astype(BF16), w1_ref[0, j * hd:(j + 1) * hd, :])
    hid = jax.nn.gelu(acc)
    o_ref[0, 0, 0] = _dot(hid.astype(BF16), w2_ref[0]).astype(BF16)


def _compress(main, pos, w1, w2, bsz, seq):
    nc = seq // NSA_CMP_STRIDE
    hd = NSA_HEAD_DIM
    ng = NSA_KV_HEADS
    col0 = MAIN_AKV // hd
    return pl.pallas_call(
        functools.partial(_compress_kernel, seq=seq),
        grid=(bsz, ng, 2),
        in_specs=[pl.BlockSpec((seq, hd), lambda b, g, t: (b, col0 + t * ng + g)),
                  pl.BlockSpec((1, NSA_CMP_LEN, hd), lambda b, g, t: (t, 0, 0)),
                  pl.BlockSpec((1, NSA_CMP_LEN * hd, hd), lambda b, g, t: (t, 0, 0)),
                  pl.BlockSpec((1, hd, hd), lambda b, g, t: (t, 0, 0))],
        out_specs=pl.BlockSpec((1, 1, 1, nc, hd), lambda b, g, t: (b, g, t, 0, 0)),
        out_shape=jax.ShapeDtypeStruct((bsz, ng, 2, nc, hd), BF16),
        scratch_shapes=[pltpu.VMEM((seq + NSA_CMP_STRIDE, hd), F32)],
        compiler_params=_params(("parallel", "parallel", "parallel")),
        name="nsa_compress",
    )(main, pos, w1, w2)


def _nsa_cmp_kernel(slopes_ref, q_ref, kc_ref, vc_ref, o_ref, sel_ref, *, tq, nc, nr):
    g = pl.program_id(1)
    qi = pl.program_id(2)
    hd = NSA_HEAD_DIM
    q0 = qi * tq
    q = q_ref[...]
    qs = jnp.concatenate([q[:, r * hd:(r + 1) * hd] for r in range(nr)], axis=0)
    kc = kc_ref[0, 0, 0]
    vc = vc_ref[0, 0, 0]
    s = _dot_nt(qs, kc) * (hd ** -0.5)
    tpos = q0 + lax.broadcasted_iota(I32, (tq, nc), 0)
    cend = lax.broadcasted_iota(I32, (tq, nc), 1) * NSA_CMP_STRIDE + (NSA_CMP_LEN - 1)
    dist = tpos - cend
    valid = dist >= 0
    distf = dist.astype(F32)
    ps = []
    psum = jnp.zeros((tq, nc), F32)
    for r in range(nr):
        sr = s[r * tq:(r + 1) * tq] - slopes_ref[g * nr + r] * distf
        sr = jnp.where(valid, sr, NEG_INF)
        m = jnp.max(sr, axis=1, keepdims=True)
        e = jnp.exp(sr - m)
        p = jnp.where(valid, e / jnp.sum(e, axis=1, keepdims=True), 0.0)
        psum = psum + p
        ps.append(p.astype(BF16))
    o = _dot(jnp.concatenate(ps, axis=0), vc)
    o_ref[...] = jnp.concatenate([o[r * tq:(r + 1) * tq] for r in range(nr)], axis=1)

    nb = LANES
    n_slc = nc * NSA_CMP_STRIDE // NSA_SEL_BLOCK
    per = NSA_SEL_BLOCK // NSA_CMP_STRIDE
    blk = lax.broadcasted_iota(I32, (nb, nc), 0)
    cidx = lax.broadcasted_iota(I32, (nb, nc), 1)
    overlap = (cidx <= per * blk + per - 1) & (cidx >= per * blk - 1) & (cidx < nc - 1) & (blk < n_slc)
    ov = jnp.where(overlap, 1.0, 0.0).astype(BF16)
    hi, mid, lo = _split3(psum)
    imp = (_dot_nt(ov, hi) + _dot_nt(ov, mid)) + _dot_nt(ov, lo)
    j = lax.broadcasted_iota(I32, (nb, tq), 0)
    cur = (q0 + lax.broadcasted_iota(I32, (nb, tq), 1)) // NSA_SEL_BLOCK
    forced = (j == 0) | (j == cur) | (j == cur - 1)
    imp = jnp.where(forced, 1e6, imp)
    imp = jnp.where(j > cur, -1e6, imp)
    imp = jnp.where(j >= n_slc, -3e38, imp)
    rank = jnp.zeros((nb, tq), F32)
    for i in range(n_slc):
        vi = imp[i:i + 1, :]
        ahead = (vi > imp) | ((vi == imp) & (j > i))
        rank = rank + jnp.where(ahead, 1.0, 0.0)
    sel = jnp.where(rank < float(min(NSA_TOP_N, n_slc)), 0.0, NEG_INF)
    sel_ref[...] = sel.T.astype(BF16)


def _nsa_cmp(main, kvc, slopes, bsz, seq, *, tq=128):
    tq = min(tq, seq)
    nq = seq // tq
    n = bsz * seq
    ng, nr, hd = NSA_KV_HEADS, NSA_HEADS // NSA_KV_HEADS, NSA_HEAD_DIM
    nc = seq // NSA_CMP_STRIDE
    qw = nr * hd
    smem = pl.BlockSpec(memory_space=pltpu.SMEM)
    return pl.pallas_call(
        functools.partial(_nsa_cmp_kernel, tq=tq, nc=nc, nr=nr),
        grid=(bsz, ng, nq),
        in_specs=[smem,
                  pl.BlockSpec((tq, qw), lambda b, g, i: (b * nq + i, g)),
                  pl.BlockSpec((1, 1, 1, nc, hd), lambda b, g, i: (b, g, 0, 0, 0)),
                  pl.BlockSpec((1, 1, 1, nc, hd), lambda b, g, i: (b, g, 1, 0, 0))],
        out_specs=[pl.BlockSpec((tq, qw), lambda b, g, i: (b * nq + i, g)),
                   pl.BlockSpec((tq, LANES), lambda b, g, i: ((b * ng + g) * nq + i, 0))],
        out_shape=[jax.ShapeDtypeStruct((n, ng * qw), F32),
                   jax.ShapeDtypeStruct((bsz * ng * seq, LANES), BF16)],
        compiler_params=_params(("parallel", "parallel", "parallel")),
        name="nsa_cmp",
    )(slopes, main, kvc, kvc)


def _nsa_slc_kernel(slopes_ref, q_ref, k_ref, v_ref, sel_ref, o_ref, sa_sc, sb_sc, m_sc, l_sc, acc_sc,
                    *, tq, tk, nr):
    g = pl.program_id(1)
    qi = pl.program_id(2)
    hd = NSA_HEAD_DIM
    q0 = qi * tq
    _flash_init(m_sc, l_sc, acc_sc)
    last = q0 // tk
    c2 = (hd ** -0.5) * LOG2E
    head = lambda r: slice(r * tq, (r + 1) * tq)

    def issue(ki, dst):
        k0 = pl.multiple_of(ki * tk, tk)
        kblk = (k0 + lax.broadcasted_iota(I32, (tk, LANES), 0)) // NSA_SEL_BLOCK
        onehot = jnp.where(kblk == lax.broadcasted_iota(I32, (tk, LANES), 1), 1.0, 0.0).astype(BF16)
        ka = jnp.concatenate([k_ref[pl.ds(k0, tk), :], onehot], axis=1)
        rel = (k0 - q0 + lax.broadcasted_iota(I32, (1, tk), 1)).astype(F32)
        for r in range(nr):
            qa = jnp.concatenate([q_ref[:, r * hd:(r + 1) * hd], sel_ref[...]], axis=1)
            dst[head(r), :] = _dot_nt(qa, ka) * c2 + (slopes_ref[g * nr + r] * LOG2E) * rel

    def load_v(ki):
        return v_ref[pl.ds(pl.multiple_of(ki * tk, tk), tk), :]

    def softmax_pv(ki, src):
        v = load_v(ki)
        for r in range(nr):
            _flash_rows(src[head(r), :], v, m_sc, l_sc, acc_sc, r * tq, tq)

    def last_tile(src):
        v = load_v(last)
        k0 = last * tk
        ahead = (k0 + lax.broadcasted_iota(I32, (tq, tk), 1)) > (q0 + lax.broadcasted_iota(I32, (tq, tk), 0))
        for r in range(nr):
            _flash_rows(jnp.where(ahead, NEG_INF, src[head(r), :]), v, m_sc, l_sc, acc_sc, r * tq, tq)

    _pipelined_sweep(last, issue, softmax_pv, last_tile, sa_sc, sb_sc)
    o = acc_sc[...] / l_sc[...]
    o_ref[...] = jnp.concatenate([o[r * tq:(r + 1) * tq] for r in range(nr)], axis=1)


def _nsa_slc(main, sel, slopes, bsz, seq, *, tq=128, tk=512):
    tq = min(tq, seq)
    tk = min(tk, seq)
    nq = seq // tq
    n = bsz * seq
    ng, nr, hd = NSA_KV_HEADS, NSA_HEADS // NSA_KV_HEADS, NSA_HEAD_DIM
    qw = nr * hd
    kb = MAIN_AKV // hd + 2 * ng
    vb = MAIN_AKV // hd + 3 * ng
    smem = pl.BlockSpec(memory_space=pltpu.SMEM)
    return pl.pallas_call(
        functools.partial(_nsa_slc_kernel, tq=tq, tk=tk, nr=nr),
        grid=(bsz, ng, nq),
        in_specs=[smem,
                  pl.BlockSpec((tq, qw), lambda b, g, i: (b * nq + i, g)),
                  pl.BlockSpec((seq, hd), lambda b, g, i: (b, kb + g)),
                  pl.BlockSpec((seq, hd), lambda b, g, i: (b, vb + g)),
                  pl.BlockSpec((tq, LANES), lambda b, g, i: ((b * ng + g) * nq + i, 0))],
        out_specs=pl.BlockSpec((tq, qw), lambda b, g, i: (b * nq + i, g)),
        out_shape=jax.ShapeDtypeStruct((n, ng * qw), F32),
        scratch_shapes=[pltpu.VMEM((nr * tq, tk), F32), pltpu.VMEM((nr * tq, tk), F32),
                        pltpu.VMEM((nr * tq, 1), F32), pltpu.VMEM((nr * tq, 1), F32),
                        pltpu.VMEM((nr * tq, hd), F32)],
        compiler_params=_params(("parallel", "parallel", "arbitrary")),
        name="nsa_slc",
    )(slopes, main, main, main, sel)


def _nsa_gate_kernel(c_ref, s_ref, w_ref, g_ref, o_ref):
    gates = jax.nn.sigmoid(g_ref[...])
    hd = NSA_HEAD_DIM
    for h in range(NSA_HEADS):
        sl = slice(h * hd, (h + 1) * hd)
        lane = MISC_GATE_LANE + h
        o_ref[:, sl] = (gates[:, lane:lane + 1] * c_ref[:, sl]
                        + gates[:, lane + NSA_HEADS:lane + NSA_HEADS + 1] * s_ref[:, sl]
                        + gates[:, lane + 2 * NSA_HEADS:lane + 2 * NSA_HEADS + 1] * w_ref[:, sl])


def _nsa_gate(o_cmp, o_slc, o_win, small, *, tm=512):
    n, w = o_cmp.shape
    tm = min(tm, n)
    row = pl.BlockSpec((tm, w), lambda i: (i, 0))
    return pl.pallas_call(
        _nsa_gate_kernel,
        grid=(n // tm,),
        in_specs=[row, row, row, pl.BlockSpec((tm, LANES), lambda i: (i, SMALL_MISC // LANES))],
        out_specs=row,
        out_shape=jax.ShapeDtypeStruct((n, w), F32),
        compiler_params=_params(("parallel",)),
        name="nsa_gate",
    )(o_cmp, o_slc, o_win, small)


def _router_kernel(x_ref, g_ref, wr_ref, br_ref, xn_ref, route_ref, gate_ref, *, tm):
    x = x_ref[...]
    ms = jnp.mean(x * x, axis=-1, keepdims=True)
    xn = x * lax.rsqrt(ms + NORM_EPS) * g_ref[...]
    xn_ref[...] = xn
    logits = lax.dot_general(wr_ref[...], xn, (((1,), (1,)), ((), ())), precision=lax.Precision.HIGHEST,
                             preferred_element_type=F32) + br_ref[:, 0:1]
    ng, ne = N_GROUPS, EXPERTS_PER_GROUP
    lg = logits[0:ng, :]
    sub = lax.broadcasted_iota(I32, (ng, tm), 0)
    mg = jnp.max(lg, axis=0, keepdims=True)
    eg = jnp.exp(lg - mg)
    pg = eg / jnp.sum(eg, axis=0, keepdims=True)
    pg_top = jnp.max(pg, axis=0, keepdims=True)
    g_idx = jnp.min(jnp.where(pg == pg_top, sub, ng), axis=0, keepdims=True)
    le = jnp.zeros((ne, tm), F32)
    for gi in range(ng):
        le = jnp.where(g_idx == gi, logits[ng + gi * ne:ng + (gi + 1) * ne, :], le)
    v1 = jnp.max(le, axis=0, keepdims=True)
    i1 = jnp.min(jnp.where(le == v1, sub, ne), axis=0, keepdims=True)
    rest = jnp.where(sub == i1, -jnp.inf, le)
    v2 = jnp.max(rest, axis=0, keepdims=True)
    i2 = jnp.min(jnp.where(rest == v2, sub, ne), axis=0, keepdims=True)
    e2 = jnp.exp(v2 - v1)
    den = 1.0 + e2
    w1 = pg_top * (1.0 / den)
    w2 = pg_top * (e2 / den)
    ex1 = (g_idx * ne + i1).astype(F32)
    ex2 = (g_idx * ne + i2).astype(F32)
    zero = jnp.zeros((1, tm), F32)
    route = jnp.concatenate([ex1, ex2, w1, w2, zero, zero, zero, zero], axis=0)
    route_ref[...] = route
    pad = jnp.concatenate([route, jnp.zeros((LANES - 8, tm), F32)], axis=0)
    gate_ref[...] = jnp.concatenate([pad[:, i * LANES:(i + 1) * LANES].T for i in range(tm // LANES)], axis=0)


def _router(x, gain, rg_w, rg_b, re_w, re_b, *, tm=256):
    n, d = x.shape
    tm = min(tm, n)
    nl = N_GROUPS + N_EXPERTS
    wr = jnp.zeros((LANES, d), F32).at[:nl].set(jnp.concatenate([rg_w, re_w], axis=1).T.astype(F32))
    br = jnp.zeros((LANES, LANES), F32).at[:nl, :].set(
        jnp.concatenate([rg_b, re_b]).astype(F32)[:, None] * jnp.ones((1, LANES), F32))
    return pl.pallas_call(
        functools.partial(_router_kernel, tm=tm),
        grid=(n // tm,),
        in_specs=[pl.BlockSpec((tm, d), lambda i: (i, 0)),
                  pl.BlockSpec((1, d), lambda i: (0, 0)),
                  pl.BlockSpec((LANES, d), lambda i: (0, 0)),
                  pl.BlockSpec((LANES, LANES), lambda i: (0, 0))],
        out_specs=[pl.BlockSpec((tm, d), lambda i: (i, 0)),
                   pl.BlockSpec((8, tm), lambda i: (0, i)),
                   pl.BlockSpec((tm, LANES), lambda i: (i, 0))],
        out_shape=[jax.ShapeDtypeStruct((n, d), F32),
                   jax.ShapeDtypeStruct((8, n), F32),
                   jax.ShapeDtypeStruct((n, LANES), F32)],
        compiler_params=_params(("parallel",), 48),
        name="moe_router",
    )(x, gain.reshape(1, d).astype(F32), wr, br)


def _slot_kernel(route_ref, pos_ref, cnt_ref, start_ref, carry_ref, *, tm):
    phase = pl.program_id(0)
    i = pl.program_id(1)
    e1 = route_ref[0:1, :].astype(I32)
    e2 = route_ref[1:2, :].astype(I32)
    sub = lax.broadcasted_iota(I32, (N_EXPERTS, tm), 0)
    oh1 = jnp.where(sub == e1, 1.0, 0.0)
    oh2 = jnp.where(sub == e2, 1.0, 0.0)
    ohs = oh1 + oh2

    @pl.when((phase == 0) & (i == 0))
    def _():
        carry_ref[...] = jnp.zeros_like(carry_ref)

    @pl.when(phase == 0)
    def _():
        carry_ref[...] = carry_ref[...] + jnp.sum(ohs, axis=1, keepdims=True)
        pos_ref[...] = jnp.zeros(pos_ref.shape, I32)

    @pl.when((phase == 1) & (i == 0))
    def _():
        cnt = carry_ref[...]
        cnt_ref[...] = cnt
        padded = jnp.floor((cnt + (MOE_ROWS - 1)) / MOE_ROWS) * MOE_ROWS
        row = lax.broadcasted_iota(I32, (N_EXPERTS, N_EXPERTS), 0)
        col = lax.broadcasted_iota(I32, (N_EXPERTS, N_EXPERTS), 1)
        lower = jnp.where(col < row, 1.0, 0.0).astype(BF16)
        hi, mid, lo = _split3(padded)
        start_ref[...] = (_dot(lower, hi) + _dot(lower, mid)) + _dot(lower, lo)
        carry_ref[...] = jnp.zeros_like(carry_ref)

    @pl.when(phase == 1)
    def _():
        row = lax.broadcasted_iota(I32, (tm, tm), 0)
        col = lax.broadcasted_iota(I32, (tm, tm), 1)
        upper = jnp.where(row < col, 1.0, 0.0).astype(BF16)
        before = _dot(ohs.astype(BF16), upper) + (carry_ref[:, 0:1] + start_ref[:, 0:1])
        p1 = jnp.sum(oh1 * before, axis=0, keepdims=True)
        p2 = jnp.sum(oh2 * before, axis=0, keepdims=True)
        zero = jnp.zeros((1, tm), F32)
        pos_ref[...] = jnp.concatenate([p1, p2, zero, zero, zero, zero, zero, zero], axis=0).astype(I32)
        carry_ref[...] = carry_ref[...] + jnp.sum(ohs, axis=1, keepdims=True)


def _slots(route, *, tm=512):
    n = route.shape[1]
    tm = min(tm, n)
    const = pl.BlockSpec((N_EXPERTS, LANES), lambda p, i: (0, 0))
    return pl.pallas_call(
        functools.partial(_slot_kernel, tm=tm),
        grid=(2, n // tm),
        in_specs=[pl.BlockSpec((8, tm), lambda p, i: (0, i))],
        out_specs=[pl.BlockSpec((8, tm), lambda p, i: (0, i * p)), const, const],
        out_shape=[jax.ShapeDtypeStruct((8, n), I32), jax.ShapeDtypeStruct((N_EXPERTS, LANES), F32),
                   jax.ShapeDtypeStruct((N_EXPERTS, LANES), F32)],
        scratch_shapes=[pltpu.VMEM((N_EXPERTS, LANES), F32)],
        compiler_params=_params(("arbitrary", "arbitrary")),
        name="moe_slots",
    )(route)


def _experts_kernel(be_ref, br_ref, src_ref, srcn_ref, x_hbm, wg_ref, wu_ref, wd_ref, o_ref,
                    xbuf, xb16, wgu16, wd16, sem, *, nblk):
    i = pl.program_id(0)
    c = pl.program_id(1)
    slot = i % 2
    rows = br_ref[i]

    def row_copy(src, tok, r, s):
        return pltpu.make_async_copy(x_hbm.at[pl.ds(tok, 1), :], xbuf.at[s, pl.ds(r, 1), :], sem.at[s])

    def start_rows(src, n_rows, s):
        def body(r, carry):
            row_copy(src, src[0, 0, r], r, s).start()
            return carry
        lax.fori_loop(0, n_rows, body, 0)

    def wait_rows(n_rows, s):
        def body(r, carry):
            row_copy(src_ref, 0, r, s).wait()
            return carry
        lax.fori_loop(0, n_rows, body, 0)

    @pl.when((i == 0) & (c == 0))
    def _():
        xbuf[...] = jnp.zeros(xbuf.shape, F32)
        start_rows(src_ref, rows, 0)

    @pl.when(c == 0)
    def _():
        wait_rows(rows, slot)
        xb16[...] = xbuf[slot].astype(BF16)
        o_ref[...] = jnp.zeros(o_ref.shape, F32)

    @pl.when((c == 1) & (i + 1 < nblk))
    def _():
        start_rows(srcn_ref, br_ref[jnp.minimum(i + 1, nblk - 1)], 1 - slot)

    wgu16[:, :DE_CHUNK] = wg_ref[0, 0].astype(BF16)
    wgu16[:, DE_CHUNK:] = wu_ref[0, 0].astype(BF16)
    wd16[...] = wd_ref[0, 0].astype(BF16)
    for sb in range(MOE_ROWS // MOE_SUB_ROWS):
        @pl.when(rows > sb * MOE_SUB_ROWS)
        def _():
            rs = slice(sb * MOE_SUB_ROWS, (sb + 1) * MOE_SUB_ROWS)
            hgu = _dot(xb16[rs, :], wgu16[...])
            h = jax.nn.silu(hgu[:, :DE_CHUNK]) * hgu[:, DE_CHUNK:]
            o_ref[rs, :] += _dot(h.astype(BF16), wd16[...])


def _experts(xn, src_tok, blk_e, blk_rows, w_gate, w_up, w_down, layer):
    n, d = xn.shape
    nblk = blk_e.shape[0]
    nch = D_EXPERT // DE_CHUNK
    src3 = src_tok.reshape(nblk, 1, MOE_ROWS)
    grid_spec = pltpu.PrefetchScalarGridSpec(
        num_scalar_prefetch=2,
        grid=(nblk, nch),
        in_specs=[
            pl.BlockSpec((1, 1, MOE_ROWS), lambda i, c, be, br: (i, 0, 0), memory_space=pltpu.SMEM),
            pl.BlockSpec((1, 1, MOE_ROWS), lambda i, c, be, br: (jnp.minimum(i + 1, nblk - 1), 0, 0),
                         memory_space=pltpu.SMEM),
            pl.BlockSpec(memory_space=pl.ANY),
            pl.BlockSpec((1, 1, d, DE_CHUNK), lambda i, c, be, br: (layer, be[i], 0, c)),
            pl.BlockSpec((1, 1, d, DE_CHUNK), lambda i, c, be, br: (layer, be[i], 0, c)),
            pl.BlockSpec((1, 1, DE_CHUNK, d), lambda i, c, be, br: (layer, be[i], c, 0)),
        ],
        out_specs=pl.BlockSpec((MOE_ROWS, d), lambda i, c, be, br: (i, 0)),
        scratch_shapes=[pltpu.VMEM((2, MOE_ROWS, d), F32), pltpu.VMEM((MOE_ROWS, d), BF16),
                        pltpu.VMEM((d, 2 * DE_CHUNK), BF16), pltpu.VMEM((DE_CHUNK, d), BF16),
                        pltpu.SemaphoreType.DMA((2,))],
    )
    return pl.pallas_call(
        functools.partial(_experts_kernel, nblk=nblk),
        grid_spec=grid_spec,
        out_shape=jax.ShapeDtypeStruct((nblk * MOE_ROWS, d), F32),
        compiler_params=_params(("arbitrary", "arbitrary"), 60),
        name="moe_experts",
    )(blk_e, blk_rows, src3, src3, xn, w_gate, w_up, w_down)


def _combine_kernel(pos_ref, y_hbm, x_ref, gate_ref, o_ref, ybuf, sem, *, tm):
    def row_copy(k, r, p):
        return pltpu.make_async_copy(y_hbm.at[pl.ds(p, 1), :], ybuf.at[k, pl.ds(r, 1), :], sem.at[0])

    def start(r, carry):
        row_copy(0, r, pos_ref[0, 0, r]).start()
        row_copy(1, r, pos_ref[0, 0, tm + r]).start()
        return carry

    def wait(r, carry):
        row_copy(0, r, 0).wait()
        row_copy(1, r, 0).wait()
        return carry

    lax.fori_loop(0, tm, start, 0)
    lax.fori_loop(0, tm, wait, 0)
    g = gate_ref[...]
    o_ref[...] = x_ref[...] + (g[:, 2:3] * ybuf[0] + g[:, 3:4] * ybuf[1])


def _combine(yb, pos, x, gates, *, tm=256):
    n, d = x.shape
    tm = min(tm, n)
    nt = n // tm
    pos3 = jnp.concatenate([pos[0].reshape(nt, 1, tm), pos[1].reshape(nt, 1, tm)], axis=2)
    return pl.pallas_call(
        functools.partial(_combine_kernel, tm=tm),
        grid=(nt,),
        in_specs=[pl.BlockSpec((1, 1, 2 * tm), lambda i: (i, 0, 0), memory_space=pltpu.SMEM),
                  pl.BlockSpec(memory_space=pl.ANY),
                  pl.BlockSpec((tm, d), lambda i: (i, 0)),
                  pl.BlockSpec((tm, LANES), lambda i: (i, 0))],
        out_specs=pl.BlockSpec((tm, d), lambda i: (i, 0)),
        out_shape=jax.ShapeDtypeStruct((n, d), F32),
        scratch_shapes=[pltpu.VMEM((2, tm, d), F32), pltpu.SemaphoreType.DMA((1,))],
        compiler_params=_params(("arbitrary",), 48),
        name="moe_combine",
    )(pos3, yb, x, gates)


def _final_norm_kernel(x_ref, g_ref, o_ref):
    x = x_ref[...]
    ms = jnp.mean(x * x, axis=-1, keepdims=True)
    o_ref[...] = x * lax.rsqrt(ms + NORM_EPS) * g_ref[...]


def _final_norm(x, gain, *, tm=512):
    n, d = x.shape
    tm = min(tm, n)
    return pl.pallas_call(
        _final_norm_kernel,
        grid=(n // tm,),
        in_specs=[pl.BlockSpec((tm, d), lambda i: (i, 0)), pl.BlockSpec((1, d), lambda i: (0, 0))],
        out_specs=pl.BlockSpec((tm, d), lambda i: (i, 0)),
        out_shape=jax.ShapeDtypeStruct((n, d), F32),
        compiler_params=_params(("parallel",)),
        name="final_norm",
    )(x, gain.reshape(1, d).astype(F32))


def _in_proj_weights(w_in):
    off = np.concatenate([[0], np.cumsum(IN_SPLITS)])
    seg = lambda i: w_in[:, off[i]:off[i + 1]]
    w_main = jnp.concatenate([seg(0), seg(1), seg(3), seg(8), seg(9)], axis=1).astype(BF16)
    pad = jnp.zeros((w_in.shape[0], SMALL_COLS - (SMALL_MISC + 64 + 24 + 8)), w_in.dtype)
    w_small = jnp.concatenate([seg(5), seg(6), seg(7), seg(2), seg(4), pad], axis=1).astype(BF16)
    return w_main, w_small


def _mla_weights(w_uq, w_ukv):
    per = MLA_NOPE_DIM + MLA_ROPE_DIM
    w3 = w_uq.reshape(MLA_Q_RANK, MLA_HEADS, per)
    nope = w3[:, :, :MLA_NOPE_DIM].reshape(MLA_Q_RANK, MLA_HEADS * MLA_NOPE_DIM)
    rope = jnp.pad(w3[:, :, MLA_NOPE_DIM:], ((0, 0), (0, 0), (0, LANES - MLA_ROPE_DIM)))
    rope = rope.reshape(MLA_Q_RANK, MLA_HEADS * LANES)
    return jnp.concatenate([nope, rope], axis=1).astype(BF16), w_ukv.astype(BF16)


def _moe_tables(pos, counts, starts, n_tok):
    n_assign = n_tok * TOP_K
    nblk = (n_assign + N_EXPERTS * (MOE_ROWS - 1)) // MOE_ROWS
    cnt = counts[:, 0].astype(I32)
    pstart = starts[:, 0].astype(I32)
    pend = pstart + (cnt + MOE_ROWS - 1) // MOE_ROWS * MOE_ROWS
    tok = jnp.broadcast_to(jnp.arange(n_tok, dtype=I32)[None, :], (2, n_tok))
    src_tok = jnp.zeros((nblk * MOE_ROWS,), I32).at[pos[0:2].reshape(-1)].set(tok.reshape(-1))
    row0 = jnp.arange(nblk, dtype=I32) * MOE_ROWS
    blk_e = jnp.minimum(jnp.sum((pend[None, :] <= row0[:, None]).astype(I32), axis=1), N_EXPERTS - 1)
    blk_rows = jnp.clip(cnt[blk_e] - (row0 - pstart[blk_e]), 0, MOE_ROWS).astype(I32)
    return src_tok, blk_e, blk_rows


def kernel(x, norm_mix_g, w_in, nsa_kc_pos, nsa_kc_w1, nsa_kc_w2, nsa_vc_pos, nsa_vc_w1, nsa_vc_w2, fox_f_bias,
           mla_q_norm_g, mla_kv_norm_g, mla_w_uq, mla_w_ukv, swa_sinks, out_norm_g, w_out, norm_ffn_g,
           router_group_w, router_group_b, router_expert_w, router_expert_b, exp_w_gate, exp_w_up, exp_w_down,
           final_norm_g):
    bsz, seq, d_model = x.shape
    n = bsz * seq
    depth = w_in.shape[0]
    xs = x.reshape(n, d_model).astype(F32)
    nsa_slopes = _alibi_slopes(NSA_HEADS)
    swa_slopes = _alibi_slopes(SWA_HEADS)
    nsa_nr = NSA_HEADS // NSA_KV_HEADS
    for l in range(depth):
        w_main, w_small = _in_proj_weights(w_in[l])
        main = _norm_matmul(xs, norm_mix_g[l], w_main, out_dtype=BF16, tn=768, name="in_proj_main")
        small = _norm_matmul(xs, norm_mix_g[l], w_small, out_dtype=F32, tn=768, name="in_proj_small")

        pos = jnp.stack([nsa_kc_pos[l], nsa_vc_pos[l]]).astype(F32)
        w1 = jnp.stack([nsa_kc_w1[l], nsa_vc_w1[l]]).astype(BF16)
        w2 = jnp.stack([nsa_kc_w2[l], nsa_vc_w2[l]]).astype(BF16)
        kvc = _compress(main, pos, w1, w2, bsz, seq)
        o_cmp, sel = _nsa_cmp(main, kvc, nsa_slopes, bsz, seq)
        o_slc = _nsa_slc(main, sel, nsa_slopes, bsz, seq)
        o_win = _banded_attn(main, nsa_slopes, None, bsz, seq, ng=NSA_KV_HEADS, nr=nsa_nr, hd=NSA_HEAD_DIM,
                             window=NSA_WINDOW, q_col=MAIN_AQ, k_col=MAIN_AKV + 8 * NSA_HEAD_DIM,
                             v_col=MAIN_AKV + 10 * NSA_HEAD_DIM, name="nsa_win")
        out_a = _nsa_gate(o_cmp, o_slc, o_win, small)

        cumt = _fox_prep(small, fox_f_bias[l], bsz, seq)
        out_b = _fox_attn(main, cumt, bsz, seq)

        w_uq, w_ukv = _mla_weights(mla_w_uq[l], mla_w_ukv[l])
        qup = _norm_matmul(small, mla_q_norm_g[l], w_uq, out_dtype=F32, col_off=SMALL_CQ, k=MLA_Q_RANK,
                           name="mla_q_up")
        kvup = _norm_matmul(small, mla_kv_norm_g[l], w_ukv, out_dtype=BF16, col_off=SMALL_CKV, k=MLA_KV_RANK,
                            name="mla_kv_up")
        qrope, krope = _rope(qup, small, bsz, seq)
        out_c = _mla_attn(qup, qrope, kvup, krope, bsz, seq)

        out_d = _banded_attn(main, swa_slopes, swa_sinks[l], bsz, seq, ng=SWA_KV_HEADS,
                             nr=SWA_HEADS // SWA_KV_HEADS, hd=SWA_HEAD_DIM, window=SWA_WINDOW, q_col=MAIN_DQ,
                             k_col=MAIN_DKV, v_col=MAIN_DKV + LANES, name="swa")

        xs = _out_proj((out_a, out_b, out_c, out_d), out_norm_g[l], w_out[l].astype(BF16), xs)

        xn, route, gates = _router(xs, norm_ffn_g[l], router_group_w[l], router_group_b[l],
                                   router_expert_w[l], router_expert_b[l])
        pos_rows, counts, starts = _slots(route)
        src_tok, blk_e, blk_rows = _moe_tables(pos_rows, counts, starts, n)
        yb = _experts(xn, src_tok, blk_e, blk_rows, exp_w_gate, exp_w_up, exp_w_down, l)
        xs = _combine(yb, pos_rows, xs, gates)
    return _final_norm(xs, final_norm_g).reshape(bsz, seq, d_model)
```

```python
import functools
import math

import numpy as np
import jax
import jax.numpy as jnp
from jax import lax
from jax.experimental import pallas as pl
from jax.experimental.pallas import tpu as pltpu

F32 = jnp.float32
BF16 = jnp.bfloat16
I32 = jnp.int32

NEG_INF = -1e30
NORM_EPS = 1e-6
LANES = 128
ROW_CHUNK = 128
LOG2E = 1.4426950408889634

Q_BLOCK = 128
GROUP_WIDTH = 1024
NSA_HEADS, NSA_KV_HEADS, NSA_HEAD_DIM = 8, 2, 128
NSA_CMP_STRIDE, NSA_CMP_LEN, NSA_SEL_BLOCK, NSA_TOP_N, NSA_WINDOW = 16, 32, 64, 16, 512
FOX_HEADS, FOX_HEAD_DIM = 8, 128
MLA_HEADS, MLA_Q_RANK, MLA_KV_RANK, MLA_NOPE_DIM, MLA_ROPE_DIM, MLA_V_DIM = 8, 768, 512, 128, 64, 128
ROPE_THETA = 10000.0
SWA_HEADS, SWA_KV_HEADS, SWA_HEAD_DIM, SWA_WINDOW = 16, 2, 64, 128
N_GROUPS, EXPERTS_PER_GROUP, TOP_K, D_EXPERT = 8, 8, 2, 384
N_EXPERTS = N_GROUPS * EXPERTS_PER_GROUP
IN_SPLITS = (1024, 1536, 24, 3072, 8, 768, 512, 64, 1024, 256)

MAIN_AQ, MAIN_AKV, MAIN_BQKV, MAIN_DQ, MAIN_DKV, MAIN_COLS = 0, 1024, 2560, 5632, 6656, 7168
SMALL_CQ, SMALL_CKV, SMALL_MISC, SMALL_COLS = 0, 768, 1280, 1536
MISC_GATE_LANE, MISC_FORGET_LANE = 64, 88

MOE_ROWS = 512
MOE_SUB_ROWS = 128
DE_CHUNK = 128


def _dot(a, b):
    return jnp.dot(a, b, preferred_element_type=F32)


def _dot_nt(a, b):
    return lax.dot_general(a, b, (((1,), (1,)), ((), ())), preferred_element_type=F32)


def _alibi_slopes(n_heads):
    return jnp.exp2(-8.0 * jnp.arange(1, n_heads + 1, dtype=F32) / n_heads)


def _split3(x):
    hi = x.astype(BF16)
    r1 = x - hi.astype(F32)
    mid = r1.astype(BF16)
    lo = (r1 - mid.astype(F32)).astype(BF16)
    return hi, mid, lo


def _params(sem, vmem_mb=None):
    kw = dict(dimension_semantics=sem)
    if vmem_mb is not None:
        kw["vmem_limit_bytes"] = vmem_mb * 1024 * 1024
    return pltpu.CompilerParams(**kw)


def _norm_matmul_kernel(x_ref, g_ref, w_ref, o_ref, xn_ref, *, col_off, k):
    @pl.when(pl.program_id(1) == 0)
    def _():
        x = x_ref[:, col_off:col_off + k].astype(F32)
        ms = jnp.mean(x * x, axis=-1, keepdims=True)
        xn_ref[...] = (x * lax.rsqrt(ms + NORM_EPS) * g_ref[...]).astype(BF16)

    o_ref[...] = _dot(xn_ref[...], w_ref[...]).astype(o_ref.dtype)


def _norm_matmul(x, gain, w, *, out_dtype, col_off=0, k=None, tm=512, tn=512, name="norm_matmul"):
    m, kfull = x.shape
    k = kfull if k is None else k
    n = w.shape[1]
    tm = min(tm, m)
    assert m % tm == 0 and n % tn == 0 and w.shape[0] == k
    return pl.pallas_call(
        functools.partial(_norm_matmul_kernel, col_off=col_off, k=k),
        grid=(m // tm, n // tn),
        in_specs=[
            pl.BlockSpec((tm, kfull), lambda i, j: (i, 0)),
            pl.BlockSpec((1, k), lambda i, j: (0, 0)),
            pl.BlockSpec((k, tn), lambda i, j: (0, j)),
        ],
        out_specs=pl.BlockSpec((tm, tn), lambda i, j: (i, j)),
        out_shape=jax.ShapeDtypeStruct((m, n), out_dtype),
        scratch_shapes=[pltpu.VMEM((tm, k), BF16)],
        compiler_params=_params(("parallel", "arbitrary"), 56),
        name=name,
    )(x, gain.reshape(1, k).astype(F32), w)


def _out_proj_kernel(a_ref, b_ref, c_ref, d_ref, g_ref, w_ref, r_ref, o_ref, xn_ref):
    @pl.when(pl.program_id(1) == 0)
    def _():
        for i, ref in enumerate((a_ref, b_ref, c_ref, d_ref)):
            x = ref[...]
            ms = jnp.mean(x * x, axis=-1, keepdims=True)
            g = g_ref[:, i * GROUP_WIDTH:(i + 1) * GROUP_WIDTH]
            xn_ref[:, i * GROUP_WIDTH:(i + 1) * GROUP_WIDTH] = (x * lax.rsqrt(ms + NORM_EPS) * g).astype(BF16)

    o_ref[...] = r_ref[...] + _dot(xn_ref[...], w_ref[...])


def _out_proj(outs, gain, w, resid, *, tm=512, tn=512):
    m, d = resid.shape
    tm = min(tm, m)
    k = 4 * GROUP_WIDTH
    grp = pl.BlockSpec((tm, GROUP_WIDTH), lambda i, j: (i, 0))
    return pl.pallas_call(
        _out_proj_kernel,
        grid=(m // tm, d // tn),
        in_specs=[grp, grp, grp, grp,
                  pl.BlockSpec((1, k), lambda i, j: (0, 0)),
                  pl.BlockSpec((k, tn), lambda i, j: (0, j)),
                  pl.BlockSpec((tm, tn), lambda i, j: (i, j))],
        out_specs=pl.BlockSpec((tm, tn), lambda i, j: (i, j)),
        out_shape=jax.ShapeDtypeStruct((m, d), F32),
        scratch_shapes=[pltpu.VMEM((tm, k), BF16)],
        compiler_params=_params(("parallel", "arbitrary"), 56),
        name="out_proj",
    )(*outs, gain.reshape(1, k).astype(F32), w, resid)


def _fox_prep_kernel(x_ref, b_ref, cumt_ref, carry_ref, *, ts):
    @pl.when(pl.program_id(1) == 0)
    def _():
        carry_ref[...] = jnp.zeros_like(carry_ref)

    z = x_ref[...] + b_ref[...]
    lf = jnp.minimum(z, 0.0) - jnp.log1p(jnp.exp(-jnp.abs(z)))
    row = lax.broadcasted_iota(I32, (ts, ts), 0)
    col = lax.broadcasted_iota(I32, (ts, ts), 1)
    tri = jnp.where(col <= row, 1.0, 0.0).astype(BF16)
    hi, mid, lo = _split3(lf)
    cum = (_dot(tri, hi) + _dot(tri, mid)) + _dot(tri, lo) + carry_ref[...]
    carry_ref[...] = cum[ts - 1:ts, :]
    cum_t = jnp.concatenate([cum[i * LANES:(i + 1) * LANES, :].T for i in range(ts // LANES)], axis=1)
    cumt_ref[0] = cum_t[MISC_FORGET_LANE:MISC_FORGET_LANE + FOX_HEADS, :]


def _fox_prep(small, f_bias, bsz, seq, *, ts=512):
    ts = min(ts, seq)
    n = bsz * seq
    ns = seq // ts
    bias = jnp.zeros((1, LANES), F32).at[0, MISC_FORGET_LANE:MISC_FORGET_LANE + FOX_HEADS].set(f_bias.astype(F32))
    misc_blk = SMALL_MISC // LANES
    return pl.pallas_call(
        functools.partial(_fox_prep_kernel, ts=ts),
        grid=(bsz, ns),
        in_specs=[pl.BlockSpec((ts, LANES), lambda b, i: (b * ns + i, misc_blk)),
                  pl.BlockSpec((1, LANES), lambda b, i: (0, 0))],
        out_specs=pl.BlockSpec((1, FOX_HEADS, ts), lambda b, i: (b, 0, i)),
        out_shape=jax.ShapeDtypeStruct((bsz, FOX_HEADS, seq), F32),
        scratch_shapes=[pltpu.VMEM((1, LANES), F32)],
        compiler_params=_params(("parallel", "arbitrary")),
        name="fox_prep",
    )(small, bias)


def _with_ones(v):
    return jnp.concatenate([v, jnp.ones((v.shape[0], LANES), v.dtype)], axis=1)


def _flash_rows(s2, v1, m_sc, l_sc, acc_sc, r0, rows):
    sl = slice(r0, r0 + rows)
    dv = v1.shape[1] - LANES
    m_prev = m_sc[sl]
    m_new = jnp.maximum(m_prev, jnp.max(s2, axis=1, keepdims=True))
    alpha = jnp.exp2(m_prev - m_new)
    p = jnp.concatenate([jnp.exp2(s2[:, j * LANES:(j + 1) * LANES] - m_new)
                         for j in range(s2.shape[1] // LANES)], axis=1)
    pv = _dot(p.astype(BF16), v1)
    l_sc[sl] = alpha * l_sc[sl] + pv[:, dv:]
    acc_sc[sl] = alpha * acc_sc[sl] + pv[:, :dv]
    m_sc[sl] = m_new


def _flash_init(m_sc, l_sc, acc_sc):
    m_sc[...] = jnp.full(m_sc.shape, NEG_INF, F32)
    l_sc[...] = jnp.zeros(l_sc.shape, F32)
    acc_sc[...] = jnp.zeros(acc_sc.shape, F32)


def _pipelined_sweep(n_full, issue, softmax_pv, last_tile, sa_sc, sb_sc):
    def full_tile(ki, src, dst):
        issue(ki + 1, dst)
        softmax_pv(ki, src)

    issue(0, sa_sc)

    def pair(j, carry):
        full_tile(2 * j, sa_sc, sb_sc)
        full_tile(2 * j + 1, sb_sc, sa_sc)
        return carry

    lax.fori_loop(0, n_full // 2, pair, 0)

    @pl.when(n_full % 2 == 1)
    def _():
        full_tile(n_full - 1, sa_sc, sb_sc)
        last_tile(sb_sc)

    @pl.when(n_full % 2 == 0)
    def _():
        last_tile(sa_sc)


def _causal_sweep(qi, tq, score_fn, load_v, sa_sc, sb_sc, m_sc, l_sc, acc_sc):
    nch = tq // ROW_CHUNK
    chunk = lambda c: slice(c * ROW_CHUNK, (c + 1) * ROW_CHUNK)

    def issue(ki, dst):
        for c in range(nch):
            dst[chunk(c), :] = score_fn(c, ki)

    def softmax_pv(ki, src):
        v = _with_ones(load_v(ki, tq))
        for c in range(nch):
            _flash_rows(src[chunk(c), :], v, m_sc, l_sc, acc_sc, c * ROW_CHUNK, ROW_CHUNK)

    def diag_tile(src):
        for c in range(nch):
            ncols = (c + 1) * ROW_CHUNK
            row = lax.broadcasted_iota(I32, (ROW_CHUNK, ncols), 0) + c * ROW_CHUNK
            col = lax.broadcasted_iota(I32, (ROW_CHUNK, ncols), 1)
            s2 = jnp.where(col <= row, src[chunk(c), 0:ncols], NEG_INF)
            _flash_rows(s2, _with_ones(load_v(qi, ncols)), m_sc, l_sc, acc_sc, c * ROW_CHUNK, ROW_CHUNK)

    _pipelined_sweep(qi, issue, softmax_pv, diag_tile, sa_sc, sb_sc)


def _fox_attn_kernel(q_ref, k_ref, v_ref, ck_ref, o_ref, sa_sc, sb_sc, m_sc, l_sc, acc_sc, *, tq, scale):
    qi = pl.program_id(2)
    _flash_init(m_sc, l_sc, acc_sc)
    c0 = ck_ref[0, pl.ds(qi, 1), :][:, 0:1]

    def score(c, ki):
        k0 = pl.multiple_of(ki * tq, tq)
        k = k_ref[pl.ds(k0, tq), :]
        ck2 = (ck_ref[0, pl.ds(ki, 1), :] - c0) * LOG2E
        q = q_ref[c * ROW_CHUNK:(c + 1) * ROW_CHUNK, :]
        return _dot_nt(q, k) * (scale * LOG2E) - ck2

    def load_v(ki, ncols):
        return v_ref[pl.ds(pl.multiple_of(ki * tq, tq), ncols), :]

    _causal_sweep(qi, tq, score, load_v, sa_sc, sb_sc, m_sc, l_sc, acc_sc)
    o_ref[...] = acc_sc[...] / l_sc[...]


def _fox_attn(main, cumt, bsz, seq, *, tq=512):
    tq = min(tq, seq)
    nq = seq // tq
    n = bsz * seq
    d = FOX_HEAD_DIM
    qb, kb, vb = MAIN_BQKV // d, MAIN_BQKV // d + FOX_HEADS, MAIN_BQKV // d + 2 * FOX_HEADS
    cumt3 = cumt.reshape(bsz * FOX_HEADS, nq, tq)
    return pl.pallas_call(
        functools.partial(_fox_attn_kernel, tq=tq, scale=d ** -0.5),
        grid=(bsz, FOX_HEADS, nq),
        in_specs=[pl.BlockSpec((tq, d), lambda b, h, i: (b * nq + i, qb + h)),
                  pl.BlockSpec((seq, d), lambda b, h, i: (b, kb + h)),
                  pl.BlockSpec((seq, d), lambda b, h, i: (b, vb + h)),
                  pl.BlockSpec((1, nq, tq), lambda b, h, i: (b * FOX_HEADS + h, 0, 0))],
        out_specs=pl.BlockSpec((tq, d), lambda b, h, i: (b * nq + i, h)),
        out_shape=jax.ShapeDtypeStruct((n, FOX_HEADS * d), F32),
        scratch_shapes=[pltpu.VMEM((tq, tq), F32), pltpu.VMEM((tq, tq), F32),
                        pltpu.VMEM((tq, LANES), F32), pltpu.VMEM((tq, LANES), F32), pltpu.VMEM((tq, d), F32)],
        compiler_params=_params(("parallel", "parallel", "arbitrary")),
        name="fox_attn",
    )(main, main, main, cumt3)


def _mla_attn_kernel(qn_ref, qr_ref, kn_ref, kr_ref, v_ref, o_ref, sa_sc, sb_sc, m_sc, l_sc, acc_sc, *, tq, scale):
    qi = pl.program_id(2)
    _flash_init(m_sc, l_sc, acc_sc)

    def score(c, ki):
        k0 = pl.multiple_of(ki * tq, tq)
        k = jnp.concatenate([kn_ref[pl.ds(k0, tq), :], kr_ref[pl.ds(k0, tq), :]], axis=1)
        rows = slice(c * ROW_CHUNK, (c + 1) * ROW_CHUNK)
        q = jnp.concatenate([qn_ref[rows, :].astype(BF16), qr_ref[rows, :]], axis=1)
        return _dot_nt(q, k) * (scale * LOG2E)

    def load_v(ki, ncols):
        return v_ref[pl.ds(pl.multiple_of(ki * tq, tq), ncols), :]

    _causal_sweep(qi, tq, score, load_v, sa_sc, sb_sc, m_sc, l_sc, acc_sc)
    o_ref[...] = acc_sc[...] / l_sc[...]


def _mla_attn(qup, qrope, kvup, krope, bsz, seq, *, tq=512):
    tq = min(tq, seq)
    nq = seq // tq
    n = bsz * seq
    d = LANES
    return pl.pallas_call(
        functools.partial(_mla_attn_kernel, tq=tq, scale=(MLA_NOPE_DIM + MLA_ROPE_DIM) ** -0.5),
        grid=(bsz, MLA_HEADS, nq),
        in_specs=[pl.BlockSpec((tq, d), lambda b, h, i: (b * nq + i, h)),
                  pl.BlockSpec((tq, d), lambda b, h, i: (b * nq + i, h)),
                  pl.BlockSpec((seq, d), lambda b, h, i: (b, 2 * h)),
                  pl.BlockSpec((seq, d), lambda b, h, i: (b, 0)),
                  pl.BlockSpec((seq, d), lambda b, h, i: (b, 2 * h + 1))],
        out_specs=pl.BlockSpec((tq, d), lambda b, h, i: (b * nq + i, h)),
        out_shape=jax.ShapeDtypeStruct((n, MLA_HEADS * MLA_V_DIM), F32),
        scratch_shapes=[pltpu.VMEM((tq, tq), F32), pltpu.VMEM((tq, tq), F32),
                        pltpu.VMEM((tq, LANES), F32), pltpu.VMEM((tq, LANES), F32), pltpu.VMEM((tq, d), F32)],
        compiler_params=_params(("parallel", "parallel", "arbitrary")),
        name="mla_attn",
    )(qup, qrope, kvup, krope, kvup)


def _rope_kernel(q_ref, k_ref, cos_ref, sin_ref, qo_ref, ko_ref):
    half = MLA_ROPE_DIM // 2

    def rope(x):
        lane = lax.broadcasted_iota(I32, x.shape, 1)
        first = (lane % MLA_ROPE_DIM) < half
        swapped = jnp.where(first, pltpu.roll(x, LANES - half, 1), pltpu.roll(x, half, 1))
        return x * cos_ref[...] + swapped * sin_ref[...]

    for t in range(q_ref.shape[1] // LANES):
        qo_ref[:, t * LANES:(t + 1) * LANES] = rope(q_ref[:, t * LANES:(t + 1) * LANES]).astype(BF16)
    k = rope(k_ref[...])
    lane = lax.broadcasted_iota(I32, k.shape, 1)
    ko_ref[...] = jnp.where(lane < MLA_ROPE_DIM, k, 0.0).astype(BF16)


def _rope(qup, small, bsz, seq, *, ts=512):
    ts = min(ts, seq)
    ns = seq // ts
    n = bsz * seq
    pos = jnp.arange(seq, dtype=F32)
    inv = ROPE_THETA ** (-jnp.arange(0, MLA_ROPE_DIM, 2, dtype=F32) / MLA_ROPE_DIM)
    ang = pos[:, None] * inv[None, :]
    cos, sin = jnp.cos(ang), jnp.sin(ang)
    cos_t = jnp.concatenate([cos, cos, cos, cos], axis=1)
    sin_t = jnp.concatenate([-sin, sin, -sin, sin], axis=1)
    w = MLA_HEADS * LANES
    return pl.pallas_call(
        _rope_kernel,
        grid=(bsz, ns),
        in_specs=[pl.BlockSpec((ts, w), lambda b, i: (b * ns + i, 1)),
                  pl.BlockSpec((ts, LANES), lambda b, i: (b * ns + i, SMALL_MISC // LANES)),
                  pl.BlockSpec((ts, LANES), lambda b, i: (i, 0)),
                  pl.BlockSpec((ts, LANES), lambda b, i: (i, 0))],
        out_specs=[pl.BlockSpec((ts, w), lambda b, i: (b * ns + i, 0)),
                   pl.BlockSpec((ts, LANES), lambda b, i: (b * ns + i, 0))],
        out_shape=[jax.ShapeDtypeStruct((n, w), BF16), jax.ShapeDtypeStruct((n, LANES), BF16)],
        compiler_params=_params(("parallel", "parallel")),
        name="mla_rope",
    )(qup, small, cos_t, sin_t)


def _banded_kernel(*refs, tq, nr, hd, window, slab, scale, has_sinks, seq):
    if has_sinks:
        slopes_ref, sinks_ref, q_ref, k_ref, v_ref, o_ref = refs
    else:
        slopes_ref, q_ref, k_ref, v_ref, o_ref = refs
    g = pl.program_id(1)
    qi = pl.program_id(2)
    q0 = qi * tq
    start = pl.multiple_of(jnp.minimum(jnp.maximum(q0 + tq - slab, 0), seq - slab), tq)
    kt = k_ref[pl.ds(start, slab), :]
    v1 = _with_ones(v_ref[pl.ds(start, slab), :])
    q = q_ref[...]
    low = lax.broadcasted_iota(I32, (tq, LANES), 1) < hd
    if hd == LANES:
        qs = jnp.concatenate([q[:, r * hd:(r + 1) * hd] for r in range(nr)], axis=0)
    else:
        zero = jnp.zeros((), q.dtype)
        parts = []
        for j in range(nr // 2):
            pair = q[:, j * LANES:(j + 1) * LANES]
            parts += [jnp.where(low, pair, zero), jnp.where(low, zero, pair)]
        qs = jnp.concatenate(parts, axis=0)
    s = _dot_nt(qs, kt) * (scale * LOG2E)
    qpos = q0 + lax.broadcasted_iota(I32, (tq, slab), 0)
    kpos = start + lax.broadcasted_iota(I32, (tq, slab), 1)
    dist = qpos - kpos
    valid = (dist >= 0) & (dist < window)
    krel = (start - q0 + lax.broadcasted_iota(I32, (1, slab), 1)).astype(F32)
    qrel = lax.broadcasted_iota(I32, (tq, 1), 0).astype(F32)
    es, sinks_e = [], []
    for r in range(nr):
        slope2 = slopes_ref[g * nr + r] * LOG2E
        sr = jnp.where(valid, s[r * tq:(r + 1) * tq] + slope2 * krel, NEG_INF)
        m = jnp.max(sr, axis=1, keepdims=True)
        if has_sinks:
            sk = sinks_ref[g * nr + r] * LOG2E + slope2 * qrel
            m = jnp.maximum(m, sk)
            sinks_e.append(jnp.exp2(sk - m))
        es.append(jnp.exp2(sr - m).astype(BF16))
    pv = _dot(jnp.concatenate(es, axis=0), v1)
    den = pv[:, LANES:]
    if has_sinks:
        den = den + jnp.concatenate(sinks_e, axis=0)
    o = pv[:, :LANES] / den
    if hd == LANES:
        o_ref[...] = jnp.concatenate([o[r * tq:(r + 1) * tq] for r in range(nr)], axis=1)
    else:
        o_ref[...] = jnp.concatenate(
            [jnp.where(low, o[2 * j * tq:(2 * j + 1) * tq], o[(2 * j + 1) * tq:(2 * j + 2) * tq])
             for j in range(nr // 2)], axis=1)


def _banded_attn(main, slopes, sinks, bsz, seq, *, ng, nr, hd, window, q_col, k_col, v_col, name, tq=128):
    tq = min(tq, seq)
    nq = seq // tq
    n = bsz * seq
    slab = min(-(-(window - 1) // tq) * tq + tq, seq)
    qw = nr * hd
    has_sinks = sinks is not None
    smem = pl.BlockSpec(memory_space=pltpu.SMEM)
    in_specs = [smem] + ([smem] if has_sinks else []) + [
        pl.BlockSpec((tq, qw), lambda b, g, i: (b * nq + i, q_col // qw + g)),
        pl.BlockSpec((seq, LANES), lambda b, g, i: (b, k_col // LANES + g)),
        pl.BlockSpec((seq, LANES), lambda b, g, i: (b, v_col // LANES + g))]
    args = [slopes] + ([sinks.astype(F32)] if has_sinks else []) + [main, main, main]
    return pl.pallas_call(
        functools.partial(_banded_kernel, tq=tq, nr=nr, hd=hd, window=window, slab=slab,
                          scale=hd ** -0.5, has_sinks=has_sinks, seq=seq),
        grid=(bsz, ng, nq),
        in_specs=in_specs,
        out_specs=pl.BlockSpec((tq, qw), lambda b, g, i: (b * nq + i, g)),
        out_shape=jax.ShapeDtypeStruct((n, ng * qw), F32),
        compiler_params=_params(("parallel", "parallel", "parallel")),
        name=name,
    )(*args)


def _compress_kernel(x_ref, pos_ref, w1_ref, w2_ref, o_ref, xf_ref, *, seq):
    nc = seq // NSA_CMP_STRIDE
    hd = NSA_HEAD_DIM
    xf_ref[0:seq, :] = x_ref[...].astype(F32)
    xf_ref[seq:seq + NSA_CMP_STRIDE, :] = jnp.zeros((NSA_CMP_STRIDE, hd), F32)
    acc = jnp.zeros((nc, hd), F32)
    for j in range(NSA_CMP_LEN):
        rows = xf_ref[pl.ds(j, nc, stride=NSA_CMP_STRIDE), :] + pos_ref[0, j:j + 1, :]
        acc = acc + _dot(rows.astype(BF16), w1_ref[0, j * hd:(j + 1) * hd, :])
    hid = jax.nn.gelu(acc)
    o_ref[0, 0, 0] = _dot(hid.astype(BF16), w2_ref[0]).astype(BF16)


def _compress(main, pos, w1, w2, bsz, seq):
    nc = seq // NSA_CMP_STRIDE
    hd = NSA_HEAD_DIM
    ng = NSA_KV_HEADS
    col0 = MAIN_AKV // hd
    return pl.pallas_call(
        functools.partial(_compress_kernel, seq=seq),
        grid=(bsz, ng, 2),
        in_specs=[pl.BlockSpec((seq, hd), lambda b, g, t: (b, col0 + t * ng + g)),
                  pl.BlockSpec((1, NSA_CMP_LEN, hd), lambda b, g, t: (t, 0, 0)),
                  pl.BlockSpec((1, NSA_CMP_LEN * hd, hd), lambda b, g, t: (t, 0, 0)),
                  pl.BlockSpec((1, hd, hd), lambda b, g, t: (t, 0, 0))],
        out_specs=pl.BlockSpec((1, 1, 1, nc, hd), lambda b, g, t: (b, g, t, 0, 0)),
        out_shape=jax.ShapeDtypeStruct((bsz, ng, 2, nc, hd), BF16),
        scratch_shapes=[pltpu.VMEM((seq + NSA_CMP_STRIDE, hd), F32)],
        compiler_params=_params(("parallel", "parallel", "parallel")),
        name="nsa_compress",
    )(main, pos, w1, w2)


def _nsa_cmp_kernel(slopes_ref, q_ref, kc_ref, vc_ref, o_ref, sel_ref, *, tq, nc, nr):
    g = pl.program_id(1)
    qi = pl.program_id(2)
    hd = NSA_HEAD_DIM
    q0 = qi * tq
    q = q_ref[...]
    qs = jnp.concatenate([q[:, r * hd:(r + 1) * hd] for r in range(nr)], axis=0)
    kc = kc_ref[0, 0, 0]
    vc = vc_ref[0, 0, 0]
    s = _dot_nt(qs, kc) * (hd ** -0.5)
    tpos = q0 + lax.broadcasted_iota(I32, (tq, nc), 0)
    cend = lax.broadcasted_iota(I32, (tq, nc), 1) * NSA_CMP_STRIDE + (NSA_CMP_LEN - 1)
    dist = tpos - cend
    valid = dist >= 0
    distf = dist.astype(F32)
    ps = []
    psum = jnp.zeros((tq, nc), F32)
    for r in range(nr):
        sr = s[r * tq:(r + 1) * tq] - slopes_ref[g * nr + r] * distf
        sr = jnp.where(valid, sr, NEG_INF)
        m = jnp.max(sr, axis=1, keepdims=True)
        e = jnp.exp(sr - m)
        p = jnp.where(valid, e / jnp.sum(e, axis=1, keepdims=True), 0.0)
        psum = psum + p
        ps.append(p.astype(BF16))
    o = _dot(jnp.concatenate(ps, axis=0), vc)
    o_ref[...] = jnp.concatenate([o[r * tq:(r + 1) * tq] for r in range(nr)], axis=1)

    nb = LANES
    n_slc = nc * NSA_CMP_STRIDE // NSA_SEL_BLOCK
    per = NSA_SEL_BLOCK // NSA_CMP_STRIDE
    blk = lax.broadcasted_iota(I32, (nb, nc), 0)
    cidx = lax.broadcasted_iota(I32, (nb, nc), 1)
    overlap = (cidx <= per * blk + per - 1) & (cidx >= per * blk - 1) & (cidx < nc - 1) & (blk < n_slc)
    ov = jnp.where(overlap, 1.0, 0.0).astype(BF16)
    hi, mid, lo = _split3(psum)
    imp = (_dot_nt(ov, hi) + _dot_nt(ov, mid)) + _dot_nt(ov, lo)
    j = lax.broadcasted_iota(I32, (nb, tq), 0)
    cur = (q0 + lax.broadcasted_iota(I32, (nb, tq), 1)) // NSA_SEL_BLOCK
    forced = (j == 0) | (j == cur) | (j == cur - 1)
    imp = jnp.where(forced, 1e6, imp)
    imp = jnp.where(j > cur, -1e6, imp)
    imp = jnp.where(j >= n_slc, -3e38, imp)
    rank = jnp.zeros((nb, tq), F32)
    for i in range(n_slc):
        vi = imp[i:i + 1, :]
        ahead = (vi > imp) | ((vi == imp) & (j > i))
        rank = rank + jnp.where(ahead, 1.0, 0.0)
    sel = jnp.where(rank < float(min(NSA_TOP_N, n_slc)), 0.0, NEG_INF)
    sel_ref[...] = sel.T.astype(BF16)


def _nsa_cmp(main, kvc, slopes, bsz, seq, *, tq=128):
    tq = min(tq, seq)
    nq = seq // tq
    n = bsz * seq
    ng, nr, hd = NSA_KV_HEADS, NSA_HEADS // NSA_KV_HEADS, NSA_HEAD_DIM
    nc = seq // NSA_CMP_STRIDE
    qw = nr * hd
    smem = pl.BlockSpec(memory_space=pltpu.SMEM)
    return pl.pallas_call(
        functools.partial(_nsa_cmp_kernel, tq=tq, nc=nc, nr=nr),
        grid=(bsz, ng, nq),
        in_specs=[smem,
                  pl.BlockSpec((tq, qw), lambda b, g, i: (b * nq + i, g)),
                  pl.BlockSpec((1, 1, 1, nc, hd), lambda b, g, i: (b, g, 0, 0, 0)),
                  pl.BlockSpec((1, 1, 1, nc, hd), lambda b, g, i: (b, g, 1, 0, 0))],
        out_specs=[pl.BlockSpec((tq, qw), lambda b, g, i: (b * nq + i, g)),
                   pl.BlockSpec((tq, LANES), lambda b, g, i: ((b * ng + g) * nq + i, 0))],
        out_shape=[jax.ShapeDtypeStruct((n, ng * qw), F32),
                   jax.ShapeDtypeStruct((bsz * ng * seq, LANES), BF16)],
        compiler_params=_params(("parallel", "parallel", "parallel")),
        name="nsa_cmp",
    )(slopes, main, kvc, kvc)


def _nsa_slc_kernel(slopes_ref, q_ref, k_ref, v_ref, sel_ref, o_ref, sa_sc, sb_sc, m_sc, l_sc, acc_sc,
                    *, tq, tk, nr):
    g = pl.program_id(1)
    qi = pl.program_id(2)
    hd = NSA_HEAD_DIM
    q0 = qi * tq
    _flash_init(m_sc, l_sc, acc_sc)
    last = q0 // tk
    c2 = (hd ** -0.5) * LOG2E
    head = lambda r: slice(r * tq, (r + 1) * tq)

    def issue(ki, dst):
        k0 = pl.multiple_of(ki * tk, tk)
        kblk = (k0 + lax.broadcasted_iota(I32, (tk, LANES), 0)) // NSA_SEL_BLOCK
        onehot = jnp.where(kblk == lax.broadcasted_iota(I32, (tk, LANES), 1), 1.0, 0.0).astype(BF16)
        ka = jnp.concatenate([k_ref[pl.ds(k0, tk), :], onehot], axis=1)
        rel = (k0 - q0 + lax.broadcasted_iota(I32, (1, tk), 1)).astype(F32)
        for r in range(nr):
            qa = jnp.concatenate([q_ref[:, r * hd:(r + 1) * hd], sel_ref[...]], axis=1)
            dst[head(r), :] = _dot_nt(qa, ka) * c2 + (slopes_ref[g * nr + r] * LOG2E) * rel

    def load_v(ki):
        return _with_ones(v_ref[pl.ds(pl.multiple_of(ki * tk, tk), tk), :])

    def softmax_pv(ki, src):
        v = load_v(ki)
        for r in range(nr):
            _flash_rows(src[head(r), :], v, m_sc, l_sc, acc_sc, r * tq, tq)

    def last_tile(src):
        v = load_v(last)
        k0 = last * tk
        ahead = (k0 + lax.broadcasted_iota(I32, (tq, tk), 1)) > (q0 + lax.broadcasted_iota(I32, (tq, tk), 0))
        for r in range(nr):
            _flash_rows(jnp.where(ahead, NEG_INF, src[head(r), :]), v, m_sc, l_sc, acc_sc, r * tq, tq)

    _pipelined_sweep(last, issue, softmax_pv, last_tile, sa_sc, sb_sc)
    o = acc_sc[...] / l_sc[...]
    o_ref[...] = jnp.concatenate([o[r * tq:(r + 1) * tq] for r in range(nr)], axis=1)


def _nsa_slc(main, sel, slopes, bsz, seq, *, tq=128, tk=512):
    tq = min(tq, seq)
    tk = min(tk, seq)
    nq = seq // tq
    n = bsz * seq
    ng, nr, hd = NSA_KV_HEADS, NSA_HEADS // NSA_KV_HEADS, NSA_HEAD_DIM
    qw = nr * hd
    kb = MAIN_AKV // hd + 2 * ng
    vb = MAIN_AKV // hd + 3 * ng
    smem = pl.BlockSpec(memory_space=pltpu.SMEM)
    return pl.pallas_call(
        functools.partial(_nsa_slc_kernel, tq=tq, tk=tk, nr=nr),
        grid=(bsz, ng, nq),
        in_specs=[smem,
                  pl.BlockSpec((tq, qw), lambda b, g, i: (b * nq + i, g)),
                  pl.BlockSpec((seq, hd), lambda b, g, i: (b, kb + g)),
                  pl.BlockSpec((seq, hd), lambda b, g, i: (b, vb + g)),
                  pl.BlockSpec((tq, LANES), lambda b, g, i: ((b * ng + g) * nq + i, 0))],
        out_specs=pl.BlockSpec((tq, qw), lambda b, g, i: (b * nq + i, g)),
        out_shape=jax.ShapeDtypeStruct((n, ng * qw), F32),
        scratch_shapes=[pltpu.VMEM((nr * tq, tk), F32), pltpu.VMEM((nr * tq, tk), F32),
                        pltpu.VMEM((nr * tq, LANES), F32), pltpu.VMEM((nr * tq, LANES), F32),
                        pltpu.VMEM((nr * tq, hd), F32)],
        compiler_params=_params(("parallel", "parallel", "arbitrary")),
        name="nsa_slc",
    )(slopes, main, main, main, sel)


def _nsa_gate_kernel(c_ref, s_ref, w_ref, g_ref, o_ref):
    gates = jax.nn.sigmoid(g_ref[...])
    hd = NSA_HEAD_DIM
    for h in range(NSA_HEADS):
        sl = slice(h * hd, (h + 1) * hd)
        lane = MISC_GATE_LANE + h
        o_ref[:, sl] = (gates[:, lane:lane + 1] * c_ref[:, sl]
                        + gates[:, lane + NSA_HEADS:lane + NSA_HEADS + 1] * s_ref[:, sl]
                        + gates[:, lane + 2 * NSA_HEADS:lane + 2 * NSA_HEADS + 1] * w_ref[:, sl])


def _nsa_gate(o_cmp, o_slc, o_win, small, *, tm=512):
    n, w = o_cmp.shape
    tm = min(tm, n)
    row = pl.BlockSpec((tm, w), lambda i: (i, 0))
    return pl.pallas_call(
        _nsa_gate_kernel,
        grid=(n // tm,),
        in_specs=[row, row, row, pl.BlockSpec((tm, LANES), lambda i: (i, SMALL_MISC // LANES))],
        out_specs=row,
        out_shape=jax.ShapeDtypeStruct((n, w), F32),
        compiler_params=_params(("parallel",)),
        name="nsa_gate",
    )(o_cmp, o_slc, o_win, small)


def _router_kernel(x_ref, g_ref, wr_ref, br_ref, xn_ref, route_ref, gate_ref, *, tm):
    x = x_ref[...]
    ms = jnp.mean(x * x, axis=-1, keepdims=True)
    xn = x * lax.rsqrt(ms + NORM_EPS) * g_ref[...]
    xn_ref[...] = xn
    logits = lax.dot_general(wr_ref[...], xn, (((1,), (1,)), ((), ())), precision=lax.Precision.HIGHEST,
                             preferred_element_type=F32) + br_ref[:, 0:1]
    ng, ne = N_GROUPS, EXPERTS_PER_GROUP
    lg = logits[0:ng, :]
    sub = lax.broadcasted_iota(I32, (ng, tm), 0)
    mg = jnp.max(lg, axis=0, keepdims=True)
    eg = jnp.exp(lg - mg)
    pg = eg / jnp.sum(eg, axis=0, keepdims=True)
    pg_top = jnp.max(pg, axis=0, keepdims=True)
    g_idx = jnp.min(jnp.where(pg == pg_top, sub, ng), axis=0, keepdims=True)
    le = jnp.zeros((ne, tm), F32)
    for gi in range(ng):
        le = jnp.where(g_idx == gi, logits[ng + gi * ne:ng + (gi + 1) * ne, :], le)
    v1 = jnp.max(le, axis=0, keepdims=True)
    i1 = jnp.min(jnp.where(le == v1, sub, ne), axis=0, keepdims=True)
    rest = jnp.where(sub == i1, -jnp.inf, le)
    v2 = jnp.max(rest, axis=0, keepdims=True)
    i2 = jnp.min(jnp.where(rest == v2, sub, ne), axis=0, keepdims=True)
    e2 = jnp.exp(v2 - v1)
    den = 1.0 + e2
    w1 = pg_top * (1.0 / den)
    w2 = pg_top * (e2 / den)
    ex1 = (g_idx * ne + i1).astype(F32)
    ex2 = (g_idx * ne + i2).astype(F32)
    zero = jnp.zeros((1, tm), F32)
    route = jnp.concatenate([ex1, ex2, w1, w2, zero, zero, zero, zero], axis=0)
    route_ref[...] = route
    pad = jnp.concatenate([route, jnp.zeros((LANES - 8, tm), F32)], axis=0)
    gate_ref[...] = jnp.concatenate([pad[:, i * LANES:(i + 1) * LANES].T for i in range(tm // LANES)], axis=0)


def _router(x, gain, rg_w, rg_b, re_w, re_b, *, tm=256):
    n, d = x.shape
    tm = min(tm, n)
    nl = N_GROUPS + N_EXPERTS
    wr = jnp.zeros((LANES, d), F32).at[:nl].set(jnp.concatenate([rg_w, re_w], axis=1).T.astype(F32))
    br = jnp.zeros((LANES, LANES), F32).at[:nl, :].set(
        jnp.concatenate([rg_b, re_b]).astype(F32)[:, None] * jnp.ones((1, LANES), F32))
    return pl.pallas_call(
        functools.partial(_router_kernel, tm=tm),
        grid=(n // tm,),
        in_specs=[pl.BlockSpec((tm, d), lambda i: (i, 0)),
                  pl.BlockSpec((1, d), lambda i: (0, 0)),
                  pl.BlockSpec((LANES, d), lambda i: (0, 0)),
                  pl.BlockSpec((LANES, LANES), lambda i: (0, 0))],
        out_specs=[pl.BlockSpec((tm, d), lambda i: (i, 0)),
                   pl.BlockSpec((8, tm), lambda i: (0, i)),
                   pl.BlockSpec((tm, LANES), lambda i: (i, 0))],
        out_shape=[jax.ShapeDtypeStruct((n, d), F32),
                   jax.ShapeDtypeStruct((8, n), F32),
                   jax.ShapeDtypeStruct((n, LANES), F32)],
        compiler_params=_params(("parallel",), 48),
        name="moe_router",
    )(x, gain.reshape(1, d).astype(F32), wr, br)


def _slot_kernel(route_ref, pos_ref, cnt_ref, start_ref, carry_ref, *, tm):
    phase = pl.program_id(0)
    i = pl.program_id(1)
    e1 = route_ref[0:1, :].astype(I32)
    e2 = route_ref[1:2, :].astype(I32)
    sub = lax.broadcasted_iota(I32, (N_EXPERTS, tm), 0)
    oh1 = jnp.where(sub == e1, 1.0, 0.0)
    oh2 = jnp.where(sub == e2, 1.0, 0.0)
    ohs = oh1 + oh2

    @pl.when((phase == 0) & (i == 0))
    def _():
        carry_ref[...] = jnp.zeros_like(carry_ref)

    @pl.when(phase == 0)
    def _():
        carry_ref[...] = carry_ref[...] + jnp.sum(ohs, axis=1, keepdims=True)
        pos_ref[...] = jnp.zeros(pos_ref.shape, I32)

    @pl.when((phase == 1) & (i == 0))
    def _():
        cnt = carry_ref[...]
        cnt_ref[...] = cnt
        padded = jnp.floor((cnt + (MOE_ROWS - 1)) / MOE_ROWS) * MOE_ROWS
        row = lax.broadcasted_iota(I32, (N_EXPERTS, N_EXPERTS), 0)
        col = lax.broadcasted_iota(I32, (N_EXPERTS, N_EXPERTS), 1)
        lower = jnp.where(col < row, 1.0, 0.0).astype(BF16)
        hi, mid, lo = _split3(padded)
        start_ref[...] = (_dot(lower, hi) + _dot(lower, mid)) + _dot(lower, lo)
        carry_ref[...] = jnp.zeros_like(carry_ref)

    @pl.when(phase == 1)
    def _():
        row = lax.broadcasted_iota(I32, (tm, tm), 0)
        col = lax.broadcasted_iota(I32, (tm, tm), 1)
        upper = jnp.where(row < col, 1.0, 0.0).astype(BF16)
        before = _dot(ohs.astype(BF16), upper) + (carry_ref[:, 0:1] + start_ref[:, 0:1])
        p1 = jnp.sum(oh1 * before, axis=0, keepdims=True)
        p2 = jnp.sum(oh2 * before, axis=0, keepdims=True)
        zero = jnp.zeros((1, tm), F32)
        pos_ref[...] = jnp.concatenate([p1, p2, zero, zero, zero, zero, zero, zero], axis=0).astype(I32)
        carry_ref[...] = carry_ref[...] + jnp.sum(ohs, axis=1, keepdims=True)


def _slots(route, *, tm=512):
    n = route.shape[1]
    tm = min(tm, n)
    const = pl.BlockSpec((N_EXPERTS, LANES), lambda p, i: (0, 0))
    return pl.pallas_call(
        functools.partial(_slot_kernel, tm=tm),
        grid=(2, n // tm),
        in_specs=[pl.BlockSpec((8, tm), lambda p, i: (0, i))],
        out_specs=[pl.BlockSpec((8, tm), lambda p, i: (0, i * p)), const, const],
        out_shape=[jax.ShapeDtypeStruct((8, n), I32), jax.ShapeDtypeStruct((N_EXPERTS, LANES), F32),
                   jax.ShapeDtypeStruct((N_EXPERTS, LANES), F32)],
        scratch_shapes=[pltpu.VMEM((N_EXPERTS, LANES), F32)],
        compiler_params=_params(("arbitrary", "arbitrary")),
        name="moe_slots",
    )(route)


def _experts_kernel(be_ref, br_ref, src_ref, srcn_ref, x_hbm, wg_ref, wu_ref, wd_ref, o_ref,
                    xbuf, xb16, wgu16, wd16, sem, *, nblk):
    i = pl.program_id(0)
    c = pl.program_id(1)
    slot = i % 2
    rows = br_ref[i]

    def row_copy(src, tok, r, s):
        return pltpu.make_async_copy(x_hbm.at[pl.ds(tok, 1), :], xbuf.at[s, pl.ds(r, 1), :], sem.at[s])

    def start_rows(src, n_rows, s):
        def body(r, carry):
            row_copy(src, src[0, 0, r], r, s).start()
            return carry
        lax.fori_loop(0, n_rows, body, 0)

    def wait_rows(n_rows, s):
        def body(r, carry):
            row_copy(src_ref, 0, r, s).wait()
            return carry
        lax.fori_loop(0, n_rows, body, 0)

    @pl.when((i == 0) & (c == 0))
    def _():
        xbuf[...] = jnp.zeros(xbuf.shape, F32)
        start_rows(src_ref, rows, 0)

    @pl.when(c == 0)
    def _():
        wait_rows(rows, slot)
        xb16[...] = xbuf[slot].astype(BF16)
        o_ref[...] = jnp.zeros(o_ref.shape, F32)

    @pl.when((c == 1) & (i + 1 < nblk))
    def _():
        start_rows(srcn_ref, br_ref[jnp.minimum(i + 1, nblk - 1)], 1 - slot)

    @pl.when(rows > 0)
    def _():
        wgu16[:, :DE_CHUNK] = wg_ref[0, 0].astype(BF16)
        wgu16[:, DE_CHUNK:] = wu_ref[0, 0].astype(BF16)
        wd16[...] = wd_ref[0, 0].astype(BF16)

    def sub_blocks(n_sub):
        rss = [slice(sb * MOE_SUB_ROWS, (sb + 1) * MOE_SUB_ROWS) for sb in range(n_sub)]
        hgus = [_dot(xb16[rs, :], wgu16[...]) for rs in rss]
        hs = [(jax.nn.silu(hgu[:, :DE_CHUNK]) * hgu[:, DE_CHUNK:]).astype(BF16) for hgu in hgus]
        for rs, h in zip(rss, hs):
            o_ref[rs, :] += _dot(h, wd16[...])

    n_sub_max = MOE_ROWS // MOE_SUB_ROWS
    for n_sub in range(1, n_sub_max + 1):
        lo = (n_sub - 1) * MOE_SUB_ROWS
        cond = (rows > lo) if n_sub == n_sub_max else ((rows > lo) & (rows <= lo + MOE_SUB_ROWS))
        pl.when(cond)(functools.partial(sub_blocks, n_sub))


def _experts(xn, src_tok, blk_e, blk_rows, w_gate, w_up, w_down, layer):
    n, d = xn.shape
    nblk = blk_e.shape[0]
    nch = D_EXPERT // DE_CHUNK
    src3 = src_tok.reshape(nblk, 1, MOE_ROWS)

    def chunk_of(i, c, br):
        return jnp.where(br[i] > 0, c, nch - 1)

    grid_spec = pltpu.PrefetchScalarGridSpec(
        num_scalar_prefetch=2,
        grid=(nblk, nch),
        in_specs=[
            pl.BlockSpec((1, 1, MOE_ROWS), lambda i, c, be, br: (i, 0, 0), memory_space=pltpu.SMEM),
            pl.BlockSpec((1, 1, MOE_ROWS), lambda i, c, be, br: (jnp.minimum(i + 1, nblk - 1), 0, 0),
                         memory_space=pltpu.SMEM),
            pl.BlockSpec(memory_space=pl.ANY),
            pl.BlockSpec((1, 1, d, DE_CHUNK), lambda i, c, be, br: (layer, be[i], 0, chunk_of(i, c, br))),
            pl.BlockSpec((1, 1, d, DE_CHUNK), lambda i, c, be, br: (layer, be[i], 0, chunk_of(i, c, br))),
            pl.BlockSpec((1, 1, DE_CHUNK, d), lambda i, c, be, br: (layer, be[i], chunk_of(i, c, br), 0)),
        ],
        out_specs=pl.BlockSpec((MOE_ROWS, d), lambda i, c, be, br: (i, 0)),
        scratch_shapes=[pltpu.VMEM((2, MOE_ROWS, d), F32), pltpu.VMEM((MOE_ROWS, d), BF16),
                        pltpu.VMEM((d, 2 * DE_CHUNK), BF16), pltpu.VMEM((DE_CHUNK, d), BF16),
                        pltpu.SemaphoreType.DMA((2,))],
    )
    return pl.pallas_call(
        functools.partial(_experts_kernel, nblk=nblk),
        grid_spec=grid_spec,
        out_shape=jax.ShapeDtypeStruct((nblk * MOE_ROWS, d), F32),
        compiler_params=_params(("arbitrary", "arbitrary"), 60),
        name="moe_experts",
    )(blk_e, blk_rows, src3, src3, xn, w_gate, w_up, w_down)


def _combine_kernel(pos_ref, y_hbm, x_ref, gate_ref, fg_ref, o_ref, ybuf, sem, *, tm, final_norm):
    def row_copy(k, r, p):
        return pltpu.make_async_copy(y_hbm.at[pl.ds(p, 1), :], ybuf.at[k, pl.ds(r, 1), :], sem.at[0])

    def start(r, carry):
        row_copy(0, r, pos_ref[0, 0, r]).start()
        row_copy(1, r, pos_ref[0, 0, tm + r]).start()
        return carry

    def wait(r, carry):
        row_copy(0, r, 0).wait()
        row_copy(1, r, 0).wait()
        return carry

    lax.fori_loop(0, tm, start, 0, unroll=8)
    lax.fori_loop(0, tm, wait, 0, unroll=8)
    g = gate_ref[...]
    y = x_ref[...] + (g[:, 2:3] * ybuf[0] + g[:, 3:4] * ybuf[1])
    if final_norm:
        ms = jnp.mean(y * y, axis=-1, keepdims=True)
        y = y * lax.rsqrt(ms + NORM_EPS) * fg_ref[...]
    o_ref[...] = y


def _combine(yb, pos, x, gates, final_gain=None, *, tm=256):
    n, d = x.shape
    tm = min(tm, n)
    nt = n // tm
    pos3 = jnp.concatenate([pos[0].reshape(nt, 1, tm), pos[1].reshape(nt, 1, tm)], axis=2)
    final_norm = final_gain is not None
    fg = (final_gain if final_norm else jnp.ones((d,), F32)).reshape(1, d).astype(F32)
    return pl.pallas_call(
        functools.partial(_combine_kernel, tm=tm, final_norm=final_norm),
        grid=(nt,),
        in_specs=[pl.BlockSpec((1, 1, 2 * tm), lambda i: (i, 0, 0), memory_space=pltpu.SMEM),
                  pl.BlockSpec(memory_space=pl.ANY),
                  pl.BlockSpec((tm, d), lambda i: (i, 0)),
                  pl.BlockSpec((tm, LANES), lambda i: (i, 0)),
                  pl.BlockSpec((1, d), lambda i: (0, 0))],
        out_specs=pl.BlockSpec((tm, d), lambda i: (i, 0)),
        out_shape=jax.ShapeDtypeStruct((n, d), F32),
        scratch_shapes=[pltpu.VMEM((2, tm, d), F32), pltpu.SemaphoreType.DMA((1,))],
        compiler_params=_params(("arbitrary",), 48),
        name="moe_combine",
    )(pos3, yb, x, gates, fg)


def _in_proj_weights(w_in):
    off = np.concatenate([[0], np.cumsum(IN_SPLITS)])
    seg = lambda i: w_in[:, off[i]:off[i + 1]]
    dkv = seg(9).reshape(-1, 2 * SWA_KV_HEADS, 1, SWA_HEAD_DIM)
    dkv = jnp.broadcast_to(dkv, dkv.shape[:2] + (LANES // SWA_HEAD_DIM, SWA_HEAD_DIM)).reshape(w_in.shape[0], -1)
    w_main = jnp.concatenate([seg(0), seg(1), seg(3), seg(8), dkv], axis=1).astype(BF16)
    pad = jnp.zeros((w_in.shape[0], SMALL_COLS - (SMALL_MISC + 64 + 24 + 8)), w_in.dtype)
    w_small = jnp.concatenate([seg(5), seg(6), seg(7), seg(2), seg(4), pad], axis=1).astype(BF16)
    return w_main, w_small


def _mla_weights(w_uq, w_ukv):
    per = MLA_NOPE_DIM + MLA_ROPE_DIM
    w3 = w_uq.reshape(MLA_Q_RANK, MLA_HEADS, per)
    nope = w3[:, :, :MLA_NOPE_DIM].reshape(MLA_Q_RANK, MLA_HEADS * MLA_NOPE_DIM)
    rope = jnp.pad(w3[:, :, MLA_NOPE_DIM:], ((0, 0), (0, 0), (0, LANES - MLA_ROPE_DIM)))
    rope = rope.reshape(MLA_Q_RANK, MLA_HEADS * LANES)
    return jnp.concatenate([nope, rope], axis=1).astype(BF16), w_ukv.astype(BF16)


def _moe_tables(pos, counts, starts, n_tok):
    n_assign = n_tok * TOP_K
    nblk = (n_assign + N_EXPERTS * (MOE_ROWS - 1)) // MOE_ROWS
    cnt = counts[:, 0].astype(I32)
    pstart = starts[:, 0].astype(I32)
    pend = pstart + (cnt + MOE_ROWS - 1) // MOE_ROWS * MOE_ROWS
    tok = jnp.broadcast_to(jnp.arange(n_tok, dtype=I32)[None, :], (2, n_tok))
    src_tok = jnp.zeros((nblk * MOE_ROWS,), I32).at[pos[0:2].reshape(-1)].set(tok.reshape(-1))
    row0 = jnp.arange(nblk, dtype=I32) * MOE_ROWS
    blk_e = jnp.minimum(jnp.sum((pend[None, :] <= row0[:, None]).astype(I32), axis=1), N_EXPERTS - 1)
    blk_rows = jnp.clip(cnt[blk_e] - (row0 - pstart[blk_e]), 0, MOE_ROWS).astype(I32)
    return src_tok, blk_e, blk_rows


def kernel(x, norm_mix_g, w_in, nsa_kc_pos, nsa_kc_w1, nsa_kc_w2, nsa_vc_pos, nsa_vc_w1, nsa_vc_w2, fox_f_bias,
           mla_q_norm_g, mla_kv_norm_g, mla_w_uq, mla_w_ukv, swa_sinks, out_norm_g, w_out, norm_ffn_g,
           router_group_w, router_group_b, router_expert_w, router_expert_b, exp_w_gate, exp_w_up, exp_w_down,
           final_norm_g):
    bsz, seq, d_model = x.shape
    n = bsz * seq
    depth = w_in.shape[0]
    xs = x.reshape(n, d_model).astype(F32)
    nsa_slopes = _alibi_slopes(NSA_HEADS)
    swa_slopes = _alibi_slopes(SWA_HEADS)
    nsa_nr = NSA_HEADS // NSA_KV_HEADS
    for l in range(depth):
        w_main, w_small = _in_proj_weights(w_in[l])
        main = _norm_matmul(xs, norm_mix_g[l], w_main, out_dtype=BF16, tn=1024, name="in_proj_main")
        small = _norm_matmul(xs, norm_mix_g[l], w_small, out_dtype=F32, tn=768, name="in_proj_small")

        pos = jnp.stack([nsa_kc_pos[l], nsa_vc_pos[l]]).astype(F32)
        w1 = jnp.stack([nsa_kc_w1[l], nsa_vc_w1[l]]).astype(BF16)
        w2 = jnp.stack([nsa_kc_w2[l], nsa_vc_w2[l]]).astype(BF16)
        kvc = _compress(main, pos, w1, w2, bsz, seq)
        o_cmp, sel = _nsa_cmp(main, kvc, nsa_slopes, bsz, seq)
        o_slc = _nsa_slc(main, sel, nsa_slopes, bsz, seq)
        o_win = _banded_attn(main, nsa_slopes, None, bsz, seq, ng=NSA_KV_HEADS, nr=nsa_nr, hd=NSA_HEAD_DIM,
                             window=NSA_WINDOW, q_col=MAIN_AQ, k_col=MAIN_AKV + 8 * NSA_HEAD_DIM,
                             v_col=MAIN_AKV + 10 * NSA_HEAD_DIM, name="nsa_win")
        out_a = _nsa_gate(o_cmp, o_slc, o_win, small)

        cumt = _fox_prep(small, fox_f_bias[l], bsz, seq)
        out_b = _fox_attn(main, cumt, bsz, seq)

        w_uq, w_ukv = _mla_weights(mla_w_uq[l], mla_w_ukv[l])
        qup = _norm_matmul(small, mla_q_norm_g[l], w_uq, out_dtype=F32, col_off=SMALL_CQ, k=MLA_Q_RANK,
                           name="mla_q_up")
        kvup = _norm_matmul(small, mla_kv_norm_g[l], w_ukv, out_dtype=BF16, col_off=SMALL_CKV, k=MLA_KV_RANK,
                            name="mla_kv_up")
        qrope, krope = _rope(qup, small, bsz, seq)
        out_c = _mla_attn(qup, qrope, kvup, krope, bsz, seq)

        out_d = _banded_attn(main, swa_slopes, swa_sinks[l], bsz, seq, ng=SWA_KV_HEADS,
                             nr=SWA_HEADS // SWA_KV_HEADS, hd=SWA_HEAD_DIM, window=SWA_WINDOW, q_col=MAIN_DQ,
                             k_col=MAIN_DKV, v_col=MAIN_DKV + SWA_KV_HEADS * LANES, name="swa")

        xs = _out_proj((out_a, out_b, out_c, out_d), out_norm_g[l], w_out[l].astype(BF16), xs)

        xn, route, gates = _router(xs, norm_ffn_g[l], router_group_w[l], router_group_b[l],
                                   router_expert_w[l], router_expert_b[l])
        pos_rows, counts, starts = _slots(route)
        src_tok, blk_e, blk_rows = _moe_tables(pos_rows, counts, starts, n)
        yb = _experts(xn, src_tok, blk_e, blk_rows, exp_w_gate, exp_w_up, exp_w_down, l)
        xs = _combine(yb, pos_rows, xs, gates, final_norm_g if l == depth - 1 else None)
    return xs.reshape(bsz, seq, d_model)
```

```python
import functools
import math

import numpy as np
import jax
import jax.numpy as jnp
from jax import lax
from jax.experimental import pallas as pl
from jax.experimental.pallas import tpu as pltpu

F32 = jnp.float32
BF16 = jnp.bfloat16
I32 = jnp.int32

NEG_INF = -1e30
NORM_EPS = 1e-6
LANES = 128
ROW_CHUNK = 128
SCORE_ROWS = 512
LOG2E = 1.4426950408889634

Q_BLOCK = 128
GROUP_WIDTH = 1024
NSA_HEADS, NSA_KV_HEADS, NSA_HEAD_DIM = 8, 2, 128
NSA_CMP_STRIDE, NSA_CMP_LEN, NSA_SEL_BLOCK, NSA_TOP_N, NSA_WINDOW = 16, 32, 64, 16, 512
FOX_HEADS, FOX_HEAD_DIM = 8, 128
MLA_HEADS, MLA_Q_RANK, MLA_KV_RANK, MLA_NOPE_DIM, MLA_ROPE_DIM, MLA_V_DIM = 8, 768, 512, 128, 64, 128
ROPE_THETA = 10000.0
SWA_HEADS, SWA_KV_HEADS, SWA_HEAD_DIM, SWA_WINDOW = 16, 2, 64, 128
N_GROUPS, EXPERTS_PER_GROUP, TOP_K, D_EXPERT = 8, 8, 2, 384
N_EXPERTS = N_GROUPS * EXPERTS_PER_GROUP
IN_SPLITS = (1024, 1536, 24, 3072, 8, 768, 512, 64, 1024, 256)

MAIN_AQ, MAIN_AKV, MAIN_BQKV, MAIN_DQ, MAIN_DKV, MAIN_COLS = 0, 1024, 2560, 5632, 6656, 7168
SMALL_CQ, SMALL_CKV, SMALL_MISC, SMALL_COLS = 0, 768, 1280, 1536
MISC_GATE_LANE, MISC_FORGET_LANE = 64, 88

MOE_ROWS = 512
MOE_SUB_ROWS = 128
DE_CHUNK = 128


def _dot(a, b):
    return jnp.dot(a, b, preferred_element_type=F32)


def _dot_nt(a, b):
    return lax.dot_general(a, b, (((1,), (1,)), ((), ())), preferred_element_type=F32)


def _alibi_slopes(n_heads):
    return jnp.exp2(-8.0 * jnp.arange(1, n_heads + 1, dtype=F32) / n_heads)


def _split3(x):
    hi = x.astype(BF16)
    r1 = x - hi.astype(F32)
    mid = r1.astype(BF16)
    lo = (r1 - mid.astype(F32)).astype(BF16)
    return hi, mid, lo


def _params(sem, vmem_mb=None):
    kw = dict(dimension_semantics=sem)
    if vmem_mb is not None:
        kw["vmem_limit_bytes"] = vmem_mb * 1024 * 1024
    return pltpu.CompilerParams(**kw)


def _norm_matmul_kernel(x_ref, g_ref, w_ref, o_ref, xn_ref, *, col_off, k):
    @pl.when(pl.program_id(1) == 0)
    def _():
        x = x_ref[:, col_off:col_off + k].astype(F32)
        ms = jnp.mean(x * x, axis=-1, keepdims=True)
        xn_ref[...] = (x * lax.rsqrt(ms + NORM_EPS) * g_ref[...]).astype(BF16)

    o_ref[...] = _dot(xn_ref[...], w_ref[...]).astype(o_ref.dtype)


def _norm_matmul(x, gain, w, *, out_dtype, col_off=0, k=None, tm=512, tn=512, name="norm_matmul"):
    m, kfull = x.shape
    k = kfull if k is None else k
    n = w.shape[1]
    tm = min(tm, m)
    assert m % tm == 0 and n % tn == 0 and w.shape[0] == k
    return pl.pallas_call(
        functools.partial(_norm_matmul_kernel, col_off=col_off, k=k),
        grid=(m // tm, n // tn),
        in_specs=[
            pl.BlockSpec((tm, kfull), lambda i, j: (i, 0)),
            pl.BlockSpec((1, k), lambda i, j: (0, 0)),
            pl.BlockSpec((k, tn), lambda i, j: (0, j)),
        ],
        out_specs=pl.BlockSpec((tm, tn), lambda i, j: (i, j)),
        out_shape=jax.ShapeDtypeStruct((m, n), out_dtype),
        scratch_shapes=[pltpu.VMEM((tm, k), BF16)],
        compiler_params=_params(("parallel", "arbitrary"), 56),
        name=name,
    )(x, gain.reshape(1, k).astype(F32), w)


def _out_proj_kernel(a_ref, b_ref, c_ref, d_ref, g_ref, w_ref, r_ref, o_ref, xn_ref):
    @pl.when(pl.program_id(1) == 0)
    def _():
        for i, ref in enumerate((a_ref, b_ref, c_ref, d_ref)):
            x = ref[...]
            ms = jnp.mean(x * x, axis=-1, keepdims=True)
            g = g_ref[:, i * GROUP_WIDTH:(i + 1) * GROUP_WIDTH]
            xn_ref[:, i * GROUP_WIDTH:(i + 1) * GROUP_WIDTH] = (x * lax.rsqrt(ms + NORM_EPS) * g).astype(BF16)

    o_ref[...] = r_ref[...] + _dot(xn_ref[...], w_ref[...])


def _out_proj(outs, gain, w, resid, *, tm=512, tn=512):
    m, d = resid.shape
    tm = min(tm, m)
    k = 4 * GROUP_WIDTH
    grp = pl.BlockSpec((tm, GROUP_WIDTH), lambda i, j: (i, 0))
    return pl.pallas_call(
        _out_proj_kernel,
        grid=(m // tm, d // tn),
        in_specs=[grp, grp, grp, grp,
                  pl.BlockSpec((1, k), lambda i, j: (0, 0)),
                  pl.BlockSpec((k, tn), lambda i, j: (0, j)),
                  pl.BlockSpec((tm, tn), lambda i, j: (i, j))],
        out_specs=pl.BlockSpec((tm, tn), lambda i, j: (i, j)),
        out_shape=jax.ShapeDtypeStruct((m, d), F32),
        scratch_shapes=[pltpu.VMEM((tm, k), BF16)],
        compiler_params=_params(("parallel", "arbitrary"), 56),
        name="out_proj",
    )(*outs, gain.reshape(1, k).astype(F32), w, resid)


def _fox_prep_kernel(x_ref, b_ref, cumt_ref, carry_ref, *, ts):
    @pl.when(pl.program_id(1) == 0)
    def _():
        carry_ref[...] = jnp.zeros_like(carry_ref)

    z = x_ref[...] + b_ref[...]
    lf = jnp.minimum(z, 0.0) - jnp.log1p(jnp.exp(-jnp.abs(z)))
    row = lax.broadcasted_iota(I32, (ts, ts), 0)
    col = lax.broadcasted_iota(I32, (ts, ts), 1)
    tri = jnp.where(col <= row, 1.0, 0.0).astype(BF16)
    hi, mid, lo = _split3(lf)
    cum = (_dot(tri, hi) + _dot(tri, mid)) + _dot(tri, lo) + carry_ref[...]
    carry_ref[...] = cum[ts - 1:ts, :]
    cum_t = jnp.concatenate([cum[i * LANES:(i + 1) * LANES, :].T for i in range(ts // LANES)], axis=1)
    cumt_ref[0] = cum_t[MISC_FORGET_LANE:MISC_FORGET_LANE + FOX_HEADS, :]


def _fox_prep(small, f_bias, bsz, seq, *, ts=512):
    ts = min(ts, seq)
    n = bsz * seq
    ns = seq // ts
    bias = jnp.zeros((1, LANES), F32).at[0, MISC_FORGET_LANE:MISC_FORGET_LANE + FOX_HEADS].set(f_bias.astype(F32))
    misc_blk = SMALL_MISC // LANES
    return pl.pallas_call(
        functools.partial(_fox_prep_kernel, ts=ts),
        grid=(bsz, ns),
        in_specs=[pl.BlockSpec((ts, LANES), lambda b, i: (b * ns + i, misc_blk)),
                  pl.BlockSpec((1, LANES), lambda b, i: (0, 0))],
        out_specs=pl.BlockSpec((1, FOX_HEADS, ts), lambda b, i: (b, 0, i)),
        out_shape=jax.ShapeDtypeStruct((bsz, FOX_HEADS, seq), F32),
        scratch_shapes=[pltpu.VMEM((1, LANES), F32)],
        compiler_params=_params(("parallel", "arbitrary")),
        name="fox_prep",
    )(small, bias)


def _with_ones(v):
    return jnp.concatenate([v, jnp.ones((v.shape[0], LANES), v.dtype)], axis=1)


def _flash_rows(s2, v1, m_sc, l_sc, acc_sc, r0, rows):
    sl = slice(r0, r0 + rows)
    dv = v1.shape[1] - LANES
    m_prev = m_sc[sl]
    m_new = jnp.maximum(m_prev, jnp.max(s2, axis=1, keepdims=True))
    alpha = jnp.exp2(m_prev - m_new)
    p = jnp.concatenate([jnp.exp2(s2[:, j * LANES:(j + 1) * LANES] - m_new)
                         for j in range(s2.shape[1] // LANES)], axis=1)
    pv = _dot(p.astype(BF16), v1)
    l_sc[sl] = alpha * l_sc[sl] + pv[:, dv:]
    acc_sc[sl] = alpha * acc_sc[sl] + pv[:, :dv]
    m_sc[sl] = m_new


def _flash_init(m_sc, l_sc, acc_sc):
    m_sc[...] = jnp.full(m_sc.shape, NEG_INF, F32)
    l_sc[...] = jnp.zeros(l_sc.shape, F32)
    acc_sc[...] = jnp.zeros(acc_sc.shape, F32)


def _pipelined_sweep(n_full, issue, softmax_pv, last_tile, sa_sc, sb_sc):
    def full_tile(ki, src, dst):
        issue(ki + 1, dst)
        softmax_pv(ki, src)

    issue(0, sa_sc)

    def pair(j, carry):
        full_tile(2 * j, sa_sc, sb_sc)
        full_tile(2 * j + 1, sb_sc, sa_sc)
        return carry

    lax.fori_loop(0, n_full // 2, pair, 0)

    @pl.when(n_full % 2 == 1)
    def _():
        full_tile(n_full - 1, sa_sc, sb_sc)
        last_tile(sb_sc)

    @pl.when(n_full % 2 == 0)
    def _():
        last_tile(sa_sc)


def _causal_sweep(qi, tq, score_fn, load_v, sa_sc, sb_sc, m_sc, l_sc, acc_sc):
    nch = tq // ROW_CHUNK
    chunk = lambda c: slice(c * ROW_CHUNK, (c + 1) * ROW_CHUNK)
    score_rows = min(SCORE_ROWS, tq)

    def issue(ki, dst):
        for c in range(tq // score_rows):
            rows = slice(c * score_rows, (c + 1) * score_rows)
            dst[rows, :] = score_fn(rows, ki)

    def softmax_pv(ki, src):
        v = _with_ones(load_v(ki, tq))
        for c in range(nch):
            _flash_rows(src[chunk(c), :], v, m_sc, l_sc, acc_sc, c * ROW_CHUNK, ROW_CHUNK)

    def diag_tile(src):
        for c in range(nch):
            ncols = (c + 1) * ROW_CHUNK
            row = lax.broadcasted_iota(I32, (ROW_CHUNK, ncols), 0) + c * ROW_CHUNK
            col = lax.broadcasted_iota(I32, (ROW_CHUNK, ncols), 1)
            s2 = jnp.where(col <= row, src[chunk(c), 0:ncols], NEG_INF)
            _flash_rows(s2, _with_ones(load_v(qi, ncols)), m_sc, l_sc, acc_sc, c * ROW_CHUNK, ROW_CHUNK)

    _pipelined_sweep(qi, issue, softmax_pv, diag_tile, sa_sc, sb_sc)


def _fox_attn_kernel(q_ref, k_ref, v_ref, ck_ref, o_ref, sa_sc, sb_sc, m_sc, l_sc, acc_sc, *, tq, scale):
    qi = pl.program_id(2)
    _flash_init(m_sc, l_sc, acc_sc)
    c0 = ck_ref[0, pl.ds(qi, 1), :][:, 0:1]

    def score(rows, ki):
        k0 = pl.multiple_of(ki * tq, tq)
        k = k_ref[pl.ds(k0, tq), :]
        ck2 = (ck_ref[0, pl.ds(ki, 1), :] - c0) * LOG2E
        return _dot_nt(q_ref[rows, :], k) * (scale * LOG2E) - ck2

    def load_v(ki, ncols):
        return v_ref[pl.ds(pl.multiple_of(ki * tq, tq), ncols), :]

    _causal_sweep(qi, tq, score, load_v, sa_sc, sb_sc, m_sc, l_sc, acc_sc)
    o_ref[...] = acc_sc[...] / l_sc[...]


def _fox_attn(main, cumt, bsz, seq, *, tq=512):
    tq = min(tq, seq)
    nq = seq // tq
    n = bsz * seq
    d = FOX_HEAD_DIM
    qb, kb, vb = MAIN_BQKV // d, MAIN_BQKV // d + FOX_HEADS, MAIN_BQKV // d + 2 * FOX_HEADS
    cumt3 = cumt.reshape(bsz * FOX_HEADS, nq, tq)
    return pl.pallas_call(
        functools.partial(_fox_attn_kernel, tq=tq, scale=d ** -0.5),
        grid=(bsz, FOX_HEADS, nq),
        in_specs=[pl.BlockSpec((tq, d), lambda b, h, i: (b * nq + i, qb + h)),
                  pl.BlockSpec((seq, d), lambda b, h, i: (b, kb + h)),
                  pl.BlockSpec((seq, d), lambda b, h, i: (b, vb + h)),
                  pl.BlockSpec((1, nq, tq), lambda b, h, i: (b * FOX_HEADS + h, 0, 0))],
        out_specs=pl.BlockSpec((tq, d), lambda b, h, i: (b * nq + i, h)),
        out_shape=jax.ShapeDtypeStruct((n, FOX_HEADS * d), F32),
        scratch_shapes=[pltpu.VMEM((tq, tq), F32), pltpu.VMEM((tq, tq), F32),
                        pltpu.VMEM((tq, LANES), F32), pltpu.VMEM((tq, LANES), F32), pltpu.VMEM((tq, d), F32)],
        compiler_params=_params(("parallel", "parallel", "arbitrary")),
        name="fox_attn",
    )(main, main, main, cumt3)


def _mla_attn_kernel(qn_ref, qr_ref, kn_ref, kr_ref, v_ref, o_ref, q_sc, sa_sc, sb_sc, m_sc, l_sc, acc_sc,
                     *, tq, scale):
    qi = pl.program_id(2)
    _flash_init(m_sc, l_sc, acc_sc)
    q_sc[:, :LANES] = qn_ref[...].astype(BF16)
    q_sc[:, LANES:] = qr_ref[...]

    def score(rows, ki):
        k0 = pl.multiple_of(ki * tq, tq)
        k = jnp.concatenate([kn_ref[pl.ds(k0, tq), :], kr_ref[pl.ds(k0, tq), :]], axis=1)
        return _dot_nt(q_sc[rows, :], k) * (scale * LOG2E)

    def load_v(ki, ncols):
        return v_ref[pl.ds(pl.multiple_of(ki * tq, tq), ncols), :]

    _causal_sweep(qi, tq, score, load_v, sa_sc, sb_sc, m_sc, l_sc, acc_sc)
    o_ref[...] = acc_sc[...] / l_sc[...]


def _mla_attn(qup, qrope, kvup, krope, bsz, seq, *, tq=512):
    tq = min(tq, seq)
    nq = seq // tq
    n = bsz * seq
    d = LANES
    return pl.pallas_call(
        functools.partial(_mla_attn_kernel, tq=tq, scale=(MLA_NOPE_DIM + MLA_ROPE_DIM) ** -0.5),
        grid=(bsz, MLA_HEADS, nq),
        in_specs=[pl.BlockSpec((tq, d), lambda b, h, i: (b * nq + i, h)),
                  pl.BlockSpec((tq, d), lambda b, h, i: (b * nq + i, h)),
                  pl.BlockSpec((seq, d), lambda b, h, i: (b, 2 * h)),
                  pl.BlockSpec((seq, d), lambda b, h, i: (b, 0)),
                  pl.BlockSpec((seq, d), lambda b, h, i: (b, 2 * h + 1))],
        out_specs=pl.BlockSpec((tq, d), lambda b, h, i: (b * nq + i, h)),
        out_shape=jax.ShapeDtypeStruct((n, MLA_HEADS * MLA_V_DIM), F32),
        scratch_shapes=[pltpu.VMEM((tq, 2 * LANES), BF16),
                        pltpu.VMEM((tq, tq), F32), pltpu.VMEM((tq, tq), F32),
                        pltpu.VMEM((tq, LANES), F32), pltpu.VMEM((tq, LANES), F32), pltpu.VMEM((tq, d), F32)],
        compiler_params=_params(("parallel", "parallel", "arbitrary")),
        name="mla_attn",
    )(qup, qrope, kvup, krope, kvup)


def _rope_kernel(q_ref, k_ref, cos_ref, sin_ref, qo_ref, ko_ref):
    half = MLA_ROPE_DIM // 2

    def rope(x):
        lane = lax.broadcasted_iota(I32, x.shape, 1)
        first = (lane % MLA_ROPE_DIM) < half
        swapped = jnp.where(first, pltpu.roll(x, LANES - half, 1), pltpu.roll(x, half, 1))
        return x * cos_ref[...] + swapped * sin_ref[...]

    for t in range(q_ref.shape[1] // LANES):
        qo_ref[:, t * LANES:(t + 1) * LANES] = rope(q_ref[:, t * LANES:(t + 1) * LANES]).astype(BF16)
    k = rope(k_ref[...])
    lane = lax.broadcasted_iota(I32, k.shape, 1)
    ko_ref[...] = jnp.where(lane < MLA_ROPE_DIM, k, 0.0).astype(BF16)


def _rope(qup, small, bsz, seq, *, ts=512):
    ts = min(ts, seq)
    ns = seq // ts
    n = bsz * seq
    pos = jnp.arange(seq, dtype=F32)
    inv = ROPE_THETA ** (-jnp.arange(0, MLA_ROPE_DIM, 2, dtype=F32) / MLA_ROPE_DIM)
    ang = pos[:, None] * inv[None, :]
    cos, sin = jnp.cos(ang), jnp.sin(ang)
    cos_t = jnp.concatenate([cos, cos, cos, cos], axis=1)
    sin_t = jnp.concatenate([-sin, sin, -sin, sin], axis=1)
    w = MLA_HEADS * LANES
    return pl.pallas_call(
        _rope_kernel,
        grid=(bsz, ns),
        in_specs=[pl.BlockSpec((ts, w), lambda b, i: (b * ns + i, 1)),
                  pl.BlockSpec((ts, LANES), lambda b, i: (b * ns + i, SMALL_MISC // LANES)),
                  pl.BlockSpec((ts, LANES), lambda b, i: (i, 0)),
                  pl.BlockSpec((ts, LANES), lambda b, i: (i, 0))],
        out_specs=[pl.BlockSpec((ts, w), lambda b, i: (b * ns + i, 0)),
                   pl.BlockSpec((ts, LANES), lambda b, i: (b * ns + i, 0))],
        out_shape=[jax.ShapeDtypeStruct((n, w), BF16), jax.ShapeDtypeStruct((n, LANES), BF16)],
        compiler_params=_params(("parallel", "parallel")),
        name="mla_rope",
    )(qup, small, cos_t, sin_t)


def _banded_kernel(*refs, tq, nr, hd, window, slab, scale, has_sinks, seq):
    if has_sinks:
        slopes_ref, sinks_ref, q_ref, k_ref, v_ref, o_ref = refs
    else:
        slopes_ref, q_ref, k_ref, v_ref, o_ref = refs
    g = pl.program_id(1)
    qi = pl.program_id(2)
    q0 = qi * tq
    start = pl.multiple_of(jnp.minimum(jnp.maximum(q0 + tq - slab, 0), seq - slab), tq)
    kt = k_ref[pl.ds(start, slab), :]
    v1 = _with_ones(v_ref[pl.ds(start, slab), :])
    q = q_ref[...]
    low = lax.broadcasted_iota(I32, (tq, LANES), 1) < hd
    if hd == LANES:
        qs = jnp.concatenate([q[:, r * hd:(r + 1) * hd] for r in range(nr)], axis=0)
    else:
        zero = jnp.zeros((), q.dtype)
        parts = []
        for j in range(nr // 2):
            pair = q[:, j * LANES:(j + 1) * LANES]
            parts += [jnp.where(low, pair, zero), jnp.where(low, zero, pair)]
        qs = jnp.concatenate(parts, axis=0)
    s = _dot_nt(qs, kt) * (scale * LOG2E)
    qpos = q0 + lax.broadcasted_iota(I32, (tq, slab), 0)
    kpos = start + lax.broadcasted_iota(I32, (tq, slab), 1)
    dist = qpos - kpos
    valid = (dist >= 0) & (dist < window)
    krel = (start - q0 + lax.broadcasted_iota(I32, (1, slab), 1)).astype(F32)
    qrel = lax.broadcasted_iota(I32, (tq, 1), 0).astype(F32)
    es, sinks_e = [], []
    for r in range(nr):
        slope2 = slopes_ref[g * nr + r] * LOG2E
        sr = jnp.where(valid, s[r * tq:(r + 1) * tq] + slope2 * krel, NEG_INF)
        m = jnp.max(sr, axis=1, keepdims=True)
        if has_sinks:
            sk = sinks_ref[g * nr + r] * LOG2E + slope2 * qrel
            m = jnp.maximum(m, sk)
            sinks_e.append(jnp.exp2(sk - m))
        es.append(jnp.exp2(sr - m).astype(BF16))
    pv = _dot(jnp.concatenate(es, axis=0), v1)
    den = pv[:, LANES:]
    if has_sinks:
        den = den + jnp.concatenate(sinks_e, axis=0)
    o = pv[:, :LANES] / den
    if hd == LANES:
        o_ref[...] = jnp.concatenate([o[r * tq:(r + 1) * tq] for r in range(nr)], axis=1)
    else:
        o_ref[...] = jnp.concatenate(
            [jnp.where(low, o[2 * j * tq:(2 * j + 1) * tq], o[(2 * j + 1) * tq:(2 * j + 2) * tq])
             for j in range(nr // 2)], axis=1)


def _banded_attn(main, slopes, sinks, bsz, seq, *, ng, nr, hd, window, q_col, k_col, v_col, name, tq=128):
    tq = min(tq, seq)
    nq = seq // tq
    n = bsz * seq
    slab = min(-(-(window - 1) // tq) * tq + tq, seq)
    qw = nr * hd
    has_sinks = sinks is not None
    smem = pl.BlockSpec(memory_space=pltpu.SMEM)
    in_specs = [smem] + ([smem] if has_sinks else []) + [
        pl.BlockSpec((tq, qw), lambda b, g, i: (b * nq + i, q_col // qw + g)),
        pl.BlockSpec((seq, LANES), lambda b, g, i: (b, k_col // LANES + g)),
        pl.BlockSpec((seq, LANES), lambda b, g, i: (b, v_col // LANES + g))]
    args = [slopes] + ([sinks.astype(F32)] if has_sinks else []) + [main, main, main]
    return pl.pallas_call(
        functools.partial(_banded_kernel, tq=tq, nr=nr, hd=hd, window=window, slab=slab,
                          scale=hd ** -0.5, has_sinks=has_sinks, seq=seq),
        grid=(bsz, ng, nq),
        in_specs=in_specs,
        out_specs=pl.BlockSpec((tq, qw), lambda b, g, i: (b * nq + i, g)),
        out_shape=jax.ShapeDtypeStruct((n, ng * qw), F32),
        compiler_params=_params(("parallel", "parallel", "parallel")),
        name=name,
    )(*args)


def _compress_kernel(x_ref, pos_ref, w1_ref, w2_ref, o_ref, xf_ref, *, seq):
    nc = seq // NSA_CMP_STRIDE
    hd = NSA_HEAD_DIM
    xf_ref[0:seq, :] = x_ref[...].astype(F32)
    xf_ref[seq:seq + NSA_CMP_STRIDE, :] = jnp.zeros((NSA_CMP_STRIDE, hd), F32)
    acc = jnp.zeros((nc, hd), F32)
    for j in range(NSA_CMP_LEN):
        rows = xf_ref[pl.ds(j, nc, stride=NSA_CMP_STRIDE), :] + pos_ref[0, j:j + 1, :]
        acc = acc + _dot(rows.astype(BF16), w1_ref[0, j * hd:(j + 1) * hd, :])
    hid = jax.nn.gelu(acc)
    o_ref[0, 0, 0] = _dot(hid.astype(BF16), w2_ref[0]).astype(BF16)


def _compress(main, pos, w1, w2, bsz, seq):
    nc = seq // NSA_CMP_STRIDE
    hd = NSA_HEAD_DIM
    ng = NSA_KV_HEADS
    col0 = MAIN_AKV // hd
    return pl.pallas_call(
        functools.partial(_compress_kernel, seq=seq),
        grid=(bsz, ng, 2),
        in_specs=[pl.BlockSpec((seq, hd), lambda b, g, t: (b, col0 + t * ng + g)),
                  pl.BlockSpec((1, NSA_CMP_LEN, hd), lambda b, g, t: (t, 0, 0)),
                  pl.BlockSpec((1, NSA_CMP_LEN * hd, hd), lambda b, g, t: (t, 0, 0)),
                  pl.BlockSpec((1, hd, hd), lambda b, g, t: (t, 0, 0))],
        out_specs=pl.BlockSpec((1, 1, 1, nc, hd), lambda b, g, t: (b, g, t, 0, 0)),
        out_shape=jax.ShapeDtypeStruct((bsz, ng, 2, nc, hd), BF16),
        scratch_shapes=[pltpu.VMEM((seq + NSA_CMP_STRIDE, hd), F32)],
        compiler_params=_params(("parallel", "parallel", "parallel")),
        name="nsa_compress",
    )(main, pos, w1, w2)


def _nsa_cmp_kernel(slopes_ref, q_ref, kc_ref, vc_ref, o_ref, sel_ref, *, tq, nc, nr):
    g = pl.program_id(1)
    qi = pl.program_id(2)
    hd = NSA_HEAD_DIM
    q0 = qi * tq
    q = q_ref[...]
    qs = jnp.concatenate([q[:, r * hd:(r + 1) * hd] for r in range(nr)], axis=0)
    kc = kc_ref[0, 0, 0]
    vc = vc_ref[0, 0, 0]
    s = _dot_nt(qs, kc) * (hd ** -0.5)
    tpos = q0 + lax.broadcasted_iota(I32, (tq, nc), 0)
    cend = lax.broadcasted_iota(I32, (tq, nc), 1) * NSA_CMP_STRIDE + (NSA_CMP_LEN - 1)
    dist = tpos - cend
    valid = dist >= 0
    distf = dist.astype(F32)
    ps = []
    psum = jnp.zeros((tq, nc), F32)
    for r in range(nr):
        sr = s[r * tq:(r + 1) * tq] - slopes_ref[g * nr + r] * distf
        sr = jnp.where(valid, sr, NEG_INF)
        m = jnp.max(sr, axis=1, keepdims=True)
        e = jnp.exp(sr - m)
        p = jnp.where(valid, e / jnp.sum(e, axis=1, keepdims=True), 0.0)
        psum = psum + p
        ps.append(p.astype(BF16))
    o = _dot(jnp.concatenate(ps, axis=0), vc)
    o_ref[...] = jnp.concatenate([o[r * tq:(r + 1) * tq] for r in range(nr)], axis=1)

    nb = LANES
    n_slc = nc * NSA_CMP_STRIDE // NSA_SEL_BLOCK
    per = NSA_SEL_BLOCK // NSA_CMP_STRIDE
    blk = lax.broadcasted_iota(I32, (nb, nc), 0)
    cidx = lax.broadcasted_iota(I32, (nb, nc), 1)
    overlap = (cidx <= per * blk + per - 1) & (cidx >= per * blk - 1) & (cidx < nc - 1) & (blk < n_slc)
    ov = jnp.where(overlap, 1.0, 0.0).astype(BF16)
    hi, mid, lo = _split3(psum)
    imp = (_dot_nt(ov, hi) + _dot_nt(ov, mid)) + _dot_nt(ov, lo)
    j = lax.broadcasted_iota(I32, (nb, tq), 0)
    cur = (q0 + lax.broadcasted_iota(I32, (nb, tq), 1)) // NSA_SEL_BLOCK
    forced = (j == 0) | (j == cur) | (j == cur - 1)
    imp = jnp.where(forced, 1e6, imp)
    imp = jnp.where(j > cur, -1e6, imp)
    imp = jnp.where(j >= n_slc, -3e38, imp)
    rank = jnp.zeros((nb, tq), F32)
    for i in range(n_slc):
        vi = imp[i:i + 1, :]
        ahead = (vi > imp) | ((vi == imp) & (j > i))
        rank = rank + jnp.where(ahead, 1.0, 0.0)
    sel = jnp.where(rank < float(min(NSA_TOP_N, n_slc)), 0.0, NEG_INF)
    sel_ref[...] = sel.T.astype(BF16)


def _nsa_cmp(main, kvc, slopes, bsz, seq, *, tq=128):
    tq = min(tq, seq)
    nq = seq // tq
    n = bsz * seq
    ng, nr, hd = NSA_KV_HEADS, NSA_HEADS // NSA_KV_HEADS, NSA_HEAD_DIM
    nc = seq // NSA_CMP_STRIDE
    qw = nr * hd
    smem = pl.BlockSpec(memory_space=pltpu.SMEM)
    return pl.pallas_call(
        functools.partial(_nsa_cmp_kernel, tq=tq, nc=nc, nr=nr),
        grid=(bsz, ng, nq),
        in_specs=[smem,
                  pl.BlockSpec((tq, qw), lambda b, g, i: (b * nq + i, g)),
                  pl.BlockSpec((1, 1, 1, nc, hd), lambda b, g, i: (b, g, 0, 0, 0)),
                  pl.BlockSpec((1, 1, 1, nc, hd), lambda b, g, i: (b, g, 1, 0, 0))],
        out_specs=[pl.BlockSpec((tq, qw), lambda b, g, i: (b * nq + i, g)),
                   pl.BlockSpec((tq, LANES), lambda b, g, i: ((b * ng + g) * nq + i, 0))],
        out_shape=[jax.ShapeDtypeStruct((n, ng * qw), F32),
                   jax.ShapeDtypeStruct((bsz * ng * seq, LANES), BF16)],
        compiler_params=_params(("parallel", "parallel", "parallel")),
        name="nsa_cmp",
    )(slopes, main, kvc, kvc)


def _nsa_slc_kernel(slopes_ref, q_ref, k_ref, v_ref, sel_ref, o_ref, sa_sc, sb_sc, m_sc, l_sc, acc_sc,
                    *, tq, tk, nr):
    g = pl.program_id(1)
    qi = pl.program_id(2)
    hd = NSA_HEAD_DIM
    q0 = qi * tq
    _flash_init(m_sc, l_sc, acc_sc)
    last = q0 // tk
    c2 = (hd ** -0.5) * LOG2E
    head = lambda r: slice(r * tq, (r + 1) * tq)

    shift = int(math.log2(NSA_SEL_BLOCK))
    lane_minus_blk = (lax.broadcasted_iota(I32, (tk, LANES), 1)
                      - lax.shift_right_logical(lax.broadcasted_iota(I32, (tk, LANES), 0), shift))

    def issue(ki, dst):
        k0 = pl.multiple_of(ki * tk, tk)
        onehot = jnp.where(lane_minus_blk == ki * (tk // NSA_SEL_BLOCK), 1.0, 0.0).astype(BF16)
        ka = jnp.concatenate([k_ref[pl.ds(k0, tk), :], onehot], axis=1)
        rel = (k0 - q0 + lax.broadcasted_iota(I32, (1, tk), 1)).astype(F32)
        for r in range(nr):
            qa = jnp.concatenate([q_ref[:, r * hd:(r + 1) * hd], sel_ref[...]], axis=1)
            dst[head(r), :] = _dot_nt(qa, ka) * c2 + (slopes_ref[g * nr + r] * LOG2E) * rel

    def load_v(ki):
        return _with_ones(v_ref[pl.ds(pl.multiple_of(ki * tk, tk), tk), :])

    def softmax_pv(ki, src):
        v = load_v(ki)
        for r in range(nr):
            _flash_rows(src[head(r), :], v, m_sc, l_sc, acc_sc, r * tq, tq)

    def last_tile(src):
        v = load_v(last)
        k0 = last * tk
        ahead = (k0 + lax.broadcasted_iota(I32, (tq, tk), 1)) > (q0 + lax.broadcasted_iota(I32, (tq, tk), 0))
        for r in range(nr):
            _flash_rows(jnp.where(ahead, NEG_INF, src[head(r), :]), v, m_sc, l_sc, acc_sc, r * tq, tq)

    _pipelined_sweep(last, issue, softmax_pv, last_tile, sa_sc, sb_sc)
    o = acc_sc[...] / l_sc[...]
    o_ref[...] = jnp.concatenate([o[r * tq:(r + 1) * tq] for r in range(nr)], axis=1)


def _nsa_slc(main, sel, slopes, bsz, seq, *, tq=128, tk=512):
    tq = min(tq, seq)
    tk = min(tk, seq)
    nq = seq // tq
    n = bsz * seq
    ng, nr, hd = NSA_KV_HEADS, NSA_HEADS // NSA_KV_HEADS, NSA_HEAD_DIM
    qw = nr * hd
    kb = MAIN_AKV // hd + 2 * ng
    vb = MAIN_AKV // hd + 3 * ng
    smem = pl.BlockSpec(memory_space=pltpu.SMEM)
    return pl.pallas_call(
        functools.partial(_nsa_slc_kernel, tq=tq, tk=tk, nr=nr),
        grid=(bsz, ng, nq),
        in_specs=[smem,
                  pl.BlockSpec((tq, qw), lambda b, g, i: (b * nq + i, g)),
                  pl.BlockSpec((seq, hd), lambda b, g, i: (b, kb + g)),
                  pl.BlockSpec((seq, hd), lambda b, g, i: (b, vb + g)),
                  pl.BlockSpec((tq, LANES), lambda b, g, i: ((b * ng + g) * nq + i, 0))],
        out_specs=pl.BlockSpec((tq, qw), lambda b, g, i: (b * nq + i, g)),
        out_shape=jax.ShapeDtypeStruct((n, ng * qw), F32),
        scratch_shapes=[pltpu.VMEM((nr * tq, tk), F32), pltpu.VMEM((nr * tq, tk), F32),
                        pltpu.VMEM((nr * tq, LANES), F32), pltpu.VMEM((nr * tq, LANES), F32),
                        pltpu.VMEM((nr * tq, hd), F32)],
        compiler_params=_params(("parallel", "parallel", "arbitrary")),
        name="nsa_slc",
    )(slopes, main, main, main, sel)


def _nsa_gate_kernel(c_ref, s_ref, w_ref, g_ref, o_ref):
    gates = jax.nn.sigmoid(g_ref[...])
    hd = NSA_HEAD_DIM
    for h in range(NSA_HEADS):
        sl = slice(h * hd, (h + 1) * hd)
        lane = MISC_GATE_LANE + h
        o_ref[:, sl] = (gates[:, lane:lane + 1] * c_ref[:, sl]
                        + gates[:, lane + NSA_HEADS:lane + NSA_HEADS + 1] * s_ref[:, sl]
                        + gates[:, lane + 2 * NSA_HEADS:lane + 2 * NSA_HEADS + 1] * w_ref[:, sl])


def _nsa_gate(o_cmp, o_slc, o_win, small, *, tm=512):
    n, w = o_cmp.shape
    tm = min(tm, n)
    row = pl.BlockSpec((tm, w), lambda i: (i, 0))
    return pl.pallas_call(
        _nsa_gate_kernel,
        grid=(n // tm,),
        in_specs=[row, row, row, pl.BlockSpec((tm, LANES), lambda i: (i, SMALL_MISC // LANES))],
        out_specs=row,
        out_shape=jax.ShapeDtypeStruct((n, w), F32),
        compiler_params=_params(("parallel",)),
        name="nsa_gate",
    )(o_cmp, o_slc, o_win, small)


def _router_kernel(x_ref, g_ref, wh_ref, wm_ref, br_ref, xn_ref, route_ref, gate_ref, *, tm):
    x = x_ref[...]
    ms = jnp.mean(x * x, axis=-1, keepdims=True)
    xn = x * lax.rsqrt(ms + NORM_EPS) * g_ref[...]
    xn_ref[...] = xn
    xh = xn.astype(BF16)
    xm = (xn - xh.astype(F32)).astype(BF16)
    wh = wh_ref[...]
    logits = _dot_nt(wh, xh) + (_dot_nt(wh, xm) + _dot_nt(wm_ref[...], xh)) + br_ref[:, 0:1]
    ng, ne = N_GROUPS, EXPERTS_PER_GROUP
    lg = logits[0:ng, :]
    sub = lax.broadcasted_iota(I32, (ng, tm), 0)
    mg = jnp.max(lg, axis=0, keepdims=True)
    eg = jnp.exp(lg - mg)
    pg = eg / jnp.sum(eg, axis=0, keepdims=True)
    pg_top = jnp.max(pg, axis=0, keepdims=True)
    g_idx = jnp.min(jnp.where(pg == pg_top, sub, ng), axis=0, keepdims=True)
    le = jnp.zeros((ne, tm), F32)
    for gi in range(ng):
        le = jnp.where(g_idx == gi, logits[ng + gi * ne:ng + (gi + 1) * ne, :], le)
    v1 = jnp.max(le, axis=0, keepdims=True)
    i1 = jnp.min(jnp.where(le == v1, sub, ne), axis=0, keepdims=True)
    rest = jnp.where(sub == i1, -jnp.inf, le)
    v2 = jnp.max(rest, axis=0, keepdims=True)
    i2 = jnp.min(jnp.where(rest == v2, sub, ne), axis=0, keepdims=True)
    e2 = jnp.exp(v2 - v1)
    den = 1.0 + e2
    w1 = pg_top * (1.0 / den)
    w2 = pg_top * (e2 / den)
    ex1 = (g_idx * ne + i1).astype(F32)
    ex2 = (g_idx * ne + i2).astype(F32)
    zero = jnp.zeros((1, tm), F32)
    route = jnp.concatenate([ex1, ex2, w1, w2, zero, zero, zero, zero], axis=0)
    route_ref[...] = route
    pad = jnp.concatenate([route, jnp.zeros((LANES - 8, tm), F32)], axis=0)
    gate_ref[...] = jnp.concatenate([pad[:, i * LANES:(i + 1) * LANES].T for i in range(tm // LANES)], axis=0)


def _router(x, gain, rg_w, rg_b, re_w, re_b, *, tm=256):
    n, d = x.shape
    tm = min(tm, n)
    nl = N_GROUPS + N_EXPERTS
    wr = jnp.zeros((LANES, d), F32).at[:nl].set(jnp.concatenate([rg_w, re_w], axis=1).T.astype(F32))
    br = jnp.zeros((LANES, LANES), F32).at[:nl, :].set(
        jnp.concatenate([rg_b, re_b]).astype(F32)[:, None] * jnp.ones((1, LANES), F32))
    wr_hi = wr.astype(BF16)
    wr_mid = (wr - wr_hi.astype(F32)).astype(BF16)
    return pl.pallas_call(
        functools.partial(_router_kernel, tm=tm),
        grid=(n // tm,),
        in_specs=[pl.BlockSpec((tm, d), lambda i: (i, 0)),
                  pl.BlockSpec((1, d), lambda i: (0, 0)),
                  pl.BlockSpec((LANES, d), lambda i: (0, 0)),
                  pl.BlockSpec((LANES, d), lambda i: (0, 0)),
                  pl.BlockSpec((LANES, LANES), lambda i: (0, 0))],
        out_specs=[pl.BlockSpec((tm, d), lambda i: (i, 0)),
                   pl.BlockSpec((8, tm), lambda i: (0, i)),
                   pl.BlockSpec((tm, LANES), lambda i: (i, 0))],
        out_shape=[jax.ShapeDtypeStruct((n, d), F32),
                   jax.ShapeDtypeStruct((8, n), F32),
                   jax.ShapeDtypeStruct((n, LANES), F32)],
        compiler_params=_params(("parallel",), 48),
        name="moe_router",
    )(x, gain.reshape(1, d).astype(F32), wr_hi, wr_mid, br)


def _slot_kernel(route_ref, pos_ref, cnt_ref, start_ref, carry_ref, *, tm):
    phase = pl.program_id(0)
    i = pl.program_id(1)
    e1 = route_ref[0:1, :].astype(I32)
    e2 = route_ref[1:2, :].astype(I32)
    sub = lax.broadcasted_iota(I32, (N_EXPERTS, tm), 0)
    oh1 = jnp.where(sub == e1, 1.0, 0.0)
    oh2 = jnp.where(sub == e2, 1.0, 0.0)
    ohs = oh1 + oh2

    @pl.when((phase == 0) & (i == 0))
    def _():
        carry_ref[...] = jnp.zeros_like(carry_ref)

    @pl.when(phase == 0)
    def _():
        carry_ref[...] = carry_ref[...] + jnp.sum(ohs, axis=1, keepdims=True)
        pos_ref[...] = jnp.zeros(pos_ref.shape, I32)

    @pl.when((phase == 1) & (i == 0))
    def _():
        cnt = carry_ref[...]
        cnt_ref[...] = cnt
        padded = jnp.floor((cnt + (MOE_ROWS - 1)) / MOE_ROWS) * MOE_ROWS
        row = lax.broadcasted_iota(I32, (N_EXPERTS, N_EXPERTS), 0)
        col = lax.broadcasted_iota(I32, (N_EXPERTS, N_EXPERTS), 1)
        lower = jnp.where(col < row, 1.0, 0.0).astype(BF16)
        hi, mid, lo = _split3(padded)
        start_ref[...] = (_dot(lower, hi) + _dot(lower, mid)) + _dot(lower, lo)
        carry_ref[...] = jnp.zeros_like(carry_ref)

    @pl.when(phase == 1)
    def _():
        row = lax.broadcasted_iota(I32, (tm, tm), 0)
        col = lax.broadcasted_iota(I32, (tm, tm), 1)
        upper = jnp.where(row < col, 1.0, 0.0).astype(BF16)
        before = _dot(ohs.astype(BF16), upper) + (carry_ref[:, 0:1] + start_ref[:, 0:1])
        p1 = jnp.sum(oh1 * before, axis=0, keepdims=True)
        p2 = jnp.sum(oh2 * before, axis=0, keepdims=True)
        zero = jnp.zeros((1, tm), F32)
        pos_ref[...] = jnp.concatenate([p1, p2, zero, zero, zero, zero, zero, zero], axis=0).astype(I32)
        carry_ref[...] = carry_ref[...] + jnp.sum(ohs, axis=1, keepdims=True)


def _slots(route, *, tm=512):
    n = route.shape[1]
    tm = min(tm, n)
    const = pl.BlockSpec((N_EXPERTS, LANES), lambda p, i: (0, 0))
    return pl.pallas_call(
        functools.partial(_slot_kernel, tm=tm),
        grid=(2, n // tm),
        in_specs=[pl.BlockSpec((8, tm), lambda p, i: (0, i))],
        out_specs=[pl.BlockSpec((8, tm), lambda p, i: (0, i * p)), const, const],
        out_shape=[jax.ShapeDtypeStruct((8, n), I32), jax.ShapeDtypeStruct((N_EXPERTS, LANES), F32),
                   jax.ShapeDtypeStruct((N_EXPERTS, LANES), F32)],
        scratch_shapes=[pltpu.VMEM((N_EXPERTS, LANES), F32)],
        compiler_params=_params(("arbitrary", "arbitrary")),
        name="moe_slots",
    )(route)


def _experts_kernel(be_ref, br_ref, dst_ref, dstn_ref, x_hbm, wg_ref, wu_ref, wd_ref, y_hbm,
                    xbuf, xb16, wgu16, wd16, acc, sem_in, sem_out, *, nblk, n_tok):
    i = pl.program_id(0)
    c = pl.program_id(1)
    nch = pl.num_programs(1)
    slot = i % 2
    rows = br_ref[i]

    def in_copy(tok, r, s):
        return pltpu.make_async_copy(x_hbm.at[pl.ds(tok, 1), :], xbuf.at[s, pl.ds(r, 1), :], sem_in.at[s])

    def out_copy(dst, r, s):
        return pltpu.make_async_copy(acc.at[s, pl.ds(r, 1), :], y_hbm.at[pl.ds(dst, 1), :], sem_out.at[s])

    def start_gather(tbl, n_rows, s):
        def body(r, carry):
            v = tbl[0, 0, r]
            in_copy(jnp.where(v >= n_tok, v - n_tok, v), r, s).start()
            return carry
        lax.fori_loop(0, n_rows, body, 0)

    def wait_gather(n_rows, s):
        def body(r, carry):
            in_copy(0, r, s).wait()
            return carry
        lax.fori_loop(0, n_rows, body, 0)

    def start_scatter(n_rows, s):
        def body(r, carry):
            out_copy(dst_ref[0, 0, r], r, s).start()
            return carry
        lax.fori_loop(0, n_rows, body, 0)

    def wait_scatter(n_rows, s):
        def body(r, carry):
            out_copy(0, r, s).wait()
            return carry
        lax.fori_loop(0, n_rows, body, 0)

    @pl.when((i == 0) & (c == 0))
    def _():
        xbuf[...] = jnp.zeros(xbuf.shape, F32)
        start_gather(dst_ref, rows, 0)

    @pl.when(c == 0)
    def _():
        wait_gather(rows, slot)
        xb16[...] = xbuf[slot].astype(BF16)
        acc[slot] = jnp.zeros(acc.shape[1:], F32)

    @pl.when((c == 1) & (i + 1 < nblk))
    def _():
        start_gather(dstn_ref, br_ref[jnp.minimum(i + 1, nblk - 1)], 1 - slot)

    @pl.when(rows > 0)
    def _():
        wgu16[:, :DE_CHUNK] = wg_ref[0, 0].astype(BF16)
        wgu16[:, DE_CHUNK:] = wu_ref[0, 0].astype(BF16)
        wd16[...] = wd_ref[0, 0].astype(BF16)

    def sub_blocks(n_sub):
        rss = [slice(sb * MOE_SUB_ROWS, (sb + 1) * MOE_SUB_ROWS) for sb in range(n_sub)]
        hgus = [_dot(xb16[rs, :], wgu16[...]) for rs in rss]
        hs = [(jax.nn.silu(hgu[:, :DE_CHUNK]) * hgu[:, DE_CHUNK:]).astype(BF16) for hgu in hgus]
        for rs, h in zip(rss, hs):
            acc[slot, rs, :] += _dot(h, wd16[...])

    n_sub_max = MOE_ROWS // MOE_SUB_ROWS
    for n_sub in range(1, n_sub_max + 1):
        lo = (n_sub - 1) * MOE_SUB_ROWS
        cond = (rows > lo) if n_sub == n_sub_max else ((rows > lo) & (rows <= lo + MOE_SUB_ROWS))
        pl.when(cond)(functools.partial(sub_blocks, n_sub))

    @pl.when(c == nch - 1)
    def _():
        @pl.when(i > 0)
        def _():
            wait_scatter(br_ref[jnp.maximum(i - 1, 0)], 1 - slot)

        start_scatter(rows, slot)

        @pl.when(i == nblk - 1)
        def _():
            wait_scatter(rows, slot)


def _experts(xn, dst_row, blk_e, blk_rows, w_gate, w_up, w_down, layer):
    n, d = xn.shape
    nblk = blk_e.shape[0]
    nch = D_EXPERT // DE_CHUNK
    dst3 = dst_row.reshape(nblk, 1, MOE_ROWS)

    def chunk_of(i, c, br):
        return jnp.where(br[i] > 0, c, nch - 1)

    grid_spec = pltpu.PrefetchScalarGridSpec(
        num_scalar_prefetch=2,
        grid=(nblk, nch),
        in_specs=[
            pl.BlockSpec((1, 1, MOE_ROWS), lambda i, c, be, br: (i, 0, 0), memory_space=pltpu.SMEM),
            pl.BlockSpec((1, 1, MOE_ROWS), lambda i, c, be, br: (jnp.minimum(i + 1, nblk - 1), 0, 0),
                         memory_space=pltpu.SMEM),
            pl.BlockSpec(memory_space=pl.ANY),
            pl.BlockSpec((1, 1, d, DE_CHUNK), lambda i, c, be, br: (layer, be[i], 0, chunk_of(i, c, br))),
            pl.BlockSpec((1, 1, d, DE_CHUNK), lambda i, c, be, br: (layer, be[i], 0, chunk_of(i, c, br))),
            pl.BlockSpec((1, 1, DE_CHUNK, d), lambda i, c, be, br: (layer, be[i], chunk_of(i, c, br), 0)),
        ],
        out_specs=pl.BlockSpec(memory_space=pl.ANY),
        scratch_shapes=[pltpu.VMEM((2, MOE_ROWS, d), F32), pltpu.VMEM((MOE_ROWS, d), BF16),
                        pltpu.VMEM((d, 2 * DE_CHUNK), BF16), pltpu.VMEM((DE_CHUNK, d), BF16),
                        pltpu.VMEM((2, MOE_ROWS, d), F32),
                        pltpu.SemaphoreType.DMA((2,)), pltpu.SemaphoreType.DMA((2,))],
    )
    return pl.pallas_call(
        functools.partial(_experts_kernel, nblk=nblk, n_tok=n),
        grid_spec=grid_spec,
        out_shape=jax.ShapeDtypeStruct((TOP_K * n, d), F32),
        compiler_params=_params(("arbitrary", "arbitrary"), 60),
        name="moe_experts",
    )(blk_e, blk_rows, dst3, dst3, xn, w_gate, w_up, w_down)


def _moe_finish_kernel(x_ref, y0_ref, y1_ref, gate_ref, fg_ref, o_ref, *, final_norm):
    g = gate_ref[...]
    y = x_ref[...] + (g[:, 2:3] * y0_ref[...] + g[:, 3:4] * y1_ref[...])
    if final_norm:
        ms = jnp.mean(y * y, axis=-1, keepdims=True)
        y = y * lax.rsqrt(ms + NORM_EPS) * fg_ref[...]
    o_ref[...] = y


def _moe_finish(y, x, gates, final_gain=None, *, tm=256):
    n, d = x.shape
    tm = min(tm, n)
    nt = n // tm
    final_norm = final_gain is not None
    fg = (final_gain if final_norm else jnp.ones((d,), F32)).reshape(1, d).astype(F32)
    return pl.pallas_call(
        functools.partial(_moe_finish_kernel, final_norm=final_norm),
        grid=(nt,),
        in_specs=[pl.BlockSpec((tm, d), lambda i: (i, 0)),
                  pl.BlockSpec((tm, d), lambda i: (i, 0)),
                  pl.BlockSpec((tm, d), lambda i: (nt + i, 0)),
                  pl.BlockSpec((tm, LANES), lambda i: (i, 0)),
                  pl.BlockSpec((1, d), lambda i: (0, 0))],
        out_specs=pl.BlockSpec((tm, d), lambda i: (i, 0)),
        out_shape=jax.ShapeDtypeStruct((n, d), F32),
        compiler_params=_params(("parallel",), 48),
        name="moe_finish",
    )(x, y, y, gates, fg)


def _in_proj_weights(w_in):
    off = np.concatenate([[0], np.cumsum(IN_SPLITS)])
    seg = lambda i: w_in[:, off[i]:off[i + 1]]
    dkv = seg(9).reshape(-1, 2 * SWA_KV_HEADS, 1, SWA_HEAD_DIM)
    dkv = jnp.broadcast_to(dkv, dkv.shape[:2] + (LANES // SWA_HEAD_DIM, SWA_HEAD_DIM)).reshape(w_in.shape[0], -1)
    w_main = jnp.concatenate([seg(0), seg(1), seg(3), seg(8), dkv], axis=1).astype(BF16)
    pad = jnp.zeros((w_in.shape[0], SMALL_COLS - (SMALL_MISC + 64 + 24 + 8)), w_in.dtype)
    w_small = jnp.concatenate([seg(5), seg(6), seg(7), seg(2), seg(4), pad], axis=1).astype(BF16)
    return w_main, w_small


def _mla_weights(w_uq, w_ukv):
    per = MLA_NOPE_DIM + MLA_ROPE_DIM
    w3 = w_uq.reshape(MLA_Q_RANK, MLA_HEADS, per)
    nope = w3[:, :, :MLA_NOPE_DIM].reshape(MLA_Q_RANK, MLA_HEADS * MLA_NOPE_DIM)
    rope = jnp.pad(w3[:, :, MLA_NOPE_DIM:], ((0, 0), (0, 0), (0, LANES - MLA_ROPE_DIM)))
    rope = rope.reshape(MLA_Q_RANK, MLA_HEADS * LANES)
    return jnp.concatenate([nope, rope], axis=1).astype(BF16), w_ukv.astype(BF16)


def _moe_tables(pos, counts, starts, n_tok):
    n_assign = n_tok * TOP_K
    nblk = (n_assign + N_EXPERTS * (MOE_ROWS - 1)) // MOE_ROWS
    cnt = counts[:, 0].astype(I32)
    pstart = starts[:, 0].astype(I32)
    pend = pstart + (cnt + MOE_ROWS - 1) // MOE_ROWS * MOE_ROWS
    dst_row = jnp.zeros((nblk * MOE_ROWS,), I32).at[pos[0:TOP_K].reshape(-1)].set(jnp.arange(n_assign, dtype=I32))
    row0 = jnp.arange(nblk, dtype=I32) * MOE_ROWS
    blk_e = jnp.minimum(jnp.sum((pend[None, :] <= row0[:, None]).astype(I32), axis=1), N_EXPERTS - 1)
    blk_rows = jnp.clip(cnt[blk_e] - (row0 - pstart[blk_e]), 0, MOE_ROWS).astype(I32)
    return dst_row, blk_e, blk_rows


def kernel(x, norm_mix_g, w_in, nsa_kc_pos, nsa_kc_w1, nsa_kc_w2, nsa_vc_pos, nsa_vc_w1, nsa_vc_w2, fox_f_bias,
           mla_q_norm_g, mla_kv_norm_g, mla_w_uq, mla_w_ukv, swa_sinks, out_norm_g, w_out, norm_ffn_g,
           router_group_w, router_group_b, router_expert_w, router_expert_b, exp_w_gate, exp_w_up, exp_w_down,
           final_norm_g):
    bsz, seq, d_model = x.shape
    n = bsz * seq
    depth = w_in.shape[0]
    xs = x.reshape(n, d_model).astype(F32)
    nsa_slopes = _alibi_slopes(NSA_HEADS)
    swa_slopes = _alibi_slopes(SWA_HEADS)
    nsa_nr = NSA_HEADS // NSA_KV_HEADS
    for l in range(depth):
        w_main, w_small = _in_proj_weights(w_in[l])
        main = _norm_matmul(xs, norm_mix_g[l], w_main, out_dtype=BF16, tn=1024, name="in_proj_main")
        small = _norm_matmul(xs, norm_mix_g[l], w_small, out_dtype=F32, tn=768, name="in_proj_small")

        pos = jnp.stack([nsa_kc_pos[l], nsa_vc_pos[l]]).astype(F32)
        w1 = jnp.stack([nsa_kc_w1[l], nsa_vc_w1[l]]).astype(BF16)
        w2 = jnp.stack([nsa_kc_w2[l], nsa_vc_w2[l]]).astype(BF16)
        kvc = _compress(main, pos, w1, w2, bsz, seq)
        o_cmp, sel = _nsa_cmp(main, kvc, nsa_slopes, bsz, seq)
        o_slc = _nsa_slc(main, sel, nsa_slopes, bsz, seq)
        o_win = _banded_attn(main, nsa_slopes, None, bsz, seq, ng=NSA_KV_HEADS, nr=nsa_nr, hd=NSA_HEAD_DIM,
                             window=NSA_WINDOW, q_col=MAIN_AQ, k_col=MAIN_AKV + 8 * NSA_HEAD_DIM,
                             v_col=MAIN_AKV + 10 * NSA_HEAD_DIM, name="nsa_win")
        out_a = _nsa_gate(o_cmp, o_slc, o_win, small)

        cumt = _fox_prep(small, fox_f_bias[l], bsz, seq)
        out_b = _fox_attn(main, cumt, bsz, seq)

        w_uq, w_ukv = _mla_weights(mla_w_uq[l], mla_w_ukv[l])
        qup = _norm_matmul(small, mla_q_norm_g[l], w_uq, out_dtype=F32, col_off=SMALL_CQ, k=MLA_Q_RANK,
                           name="mla_q_up")
        kvup = _norm_matmul(small, mla_kv_norm_g[l], w_ukv, out_dtype=BF16, col_off=SMALL_CKV, k=MLA_KV_RANK,
                            name="mla_kv_up")
        qrope, krope = _rope(qup, small, bsz, seq)
        out_c = _mla_attn(qup, qrope, kvup, krope, bsz, seq)

        out_d = _banded_attn(main, swa_slopes, swa_sinks[l], bsz, seq, ng=SWA_KV_HEADS,
                             nr=SWA_HEADS // SWA_KV_HEADS, hd=SWA_HEAD_DIM, window=SWA_WINDOW, q_col=MAIN_DQ,
                             k_col=MAIN_DKV, v_col=MAIN_DKV + SWA_KV_HEADS * LANES, name="swa")

        xs = _out_proj((out_a, out_b, out_c, out_d), out_norm_g[l], w_out[l].astype(BF16), xs)

        xn, route, gates = _router(xs, norm_ffn_g[l], router_group_w[l], router_group_b[l],
                                   router_expert_w[l], router_expert_b[l])
        pos_rows, counts, starts = _slots(route)
        dst_row, blk_e, blk_rows = _moe_tables(pos_rows, counts, starts, n)
        y = _experts(xn, dst_row, blk_e, blk_rows, exp_w_gate, exp_w_up, exp_w_down, l)
        xs = _moe_finish(y, xs, gates, final_norm_g if l == depth - 1 else None)
    return xs.reshape(bsz, seq, d_model)
```

```python
import functools
import math

import numpy as np
import jax
import jax.numpy as jnp
from jax import lax
from jax.experimental import pallas as pl
from jax.experimental.pallas import tpu as pltpu

F32 = jnp.float32
BF16 = jnp.bfloat16
I32 = jnp.int32

NEG_INF = -1e30
NORM_EPS = 1e-6
LANES = 128
ROW_CHUNK = 128
SCORE_ROWS = 512
LOG2E = 1.4426950408889634

Q_BLOCK = 128
GROUP_WIDTH = 1024
NSA_HEADS, NSA_KV_HEADS, NSA_HEAD_DIM = 8, 2, 128
NSA_CMP_STRIDE, NSA_CMP_LEN, NSA_SEL_BLOCK, NSA_TOP_N, NSA_WINDOW = 16, 32, 64, 16, 512
FOX_HEADS, FOX_HEAD_DIM = 8, 128
MLA_HEADS, MLA_Q_RANK, MLA_KV_RANK, MLA_NOPE_DIM, MLA_ROPE_DIM, MLA_V_DIM = 8, 768, 512, 128, 64, 128
ROPE_THETA = 10000.0
SWA_HEADS, SWA_KV_HEADS, SWA_HEAD_DIM, SWA_WINDOW = 16, 2, 64, 128
N_GROUPS, EXPERTS_PER_GROUP, TOP_K, D_EXPERT = 8, 8, 2, 384
N_EXPERTS = N_GROUPS * EXPERTS_PER_GROUP
IN_SPLITS = (1024, 1536, 24, 3072, 8, 768, 512, 64, 1024, 256)

MAIN_AQ, MAIN_AKV, MAIN_BQKV, MAIN_DQ, MAIN_DKV, MAIN_COLS = 0, 1024, 2560, 5632, 6656, 7168
SMALL_CQ, SMALL_CKV, SMALL_MISC, SMALL_COLS = 0, 768, 1280, 1536
MISC_GATE_LANE, MISC_FORGET_LANE = 64, 88

MOE_ROWS = 512
MOE_SUB_ROWS = 128
DE_CHUNK = 128
DMA_UNROLL = 8


def _dot(a, b):
    return jnp.dot(a, b, preferred_element_type=F32)


def _dot_nt(a, b):
    return lax.dot_general(a, b, (((1,), (1,)), ((), ())), preferred_element_type=F32)


def _alibi_slopes(n_heads):
    return jnp.exp2(-8.0 * jnp.arange(1, n_heads + 1, dtype=F32) / n_heads)


def _split3(x):
    hi = x.astype(BF16)
    r1 = x - hi.astype(F32)
    mid = r1.astype(BF16)
    lo = (r1 - mid.astype(F32)).astype(BF16)
    return hi, mid, lo


def _params(sem, vmem_mb=None):
    kw = dict(dimension_semantics=sem)
    if vmem_mb is not None:
        kw["vmem_limit_bytes"] = vmem_mb * 1024 * 1024
    return pltpu.CompilerParams(**kw)


def _norm_matmul_kernel(x_ref, g_ref, w_ref, o_ref, xn_ref, *, col_off, k):
    @pl.when(pl.program_id(1) == 0)
    def _():
        x = x_ref[:, col_off:col_off + k].astype(F32)
        ms = jnp.mean(x * x, axis=-1, keepdims=True)
        xn_ref[...] = (x * lax.rsqrt(ms + NORM_EPS) * g_ref[...]).astype(BF16)

    o_ref[...] = _dot(xn_ref[...], w_ref[...]).astype(o_ref.dtype)


def _norm_matmul(x, gain, w, *, out_dtype, col_off=0, k=None, tm=512, tn=512, name="norm_matmul"):
    m, kfull = x.shape
    k = kfull if k is None else k
    n = w.shape[1]
    tm = min(tm, m)
    assert m % tm == 0 and n % tn == 0 and w.shape[0] == k
    return pl.pallas_call(
        functools.partial(_norm_matmul_kernel, col_off=col_off, k=k),
        grid=(m // tm, n // tn),
        in_specs=[
            pl.BlockSpec((tm, kfull), lambda i, j: (i, 0)),
            pl.BlockSpec((1, k), lambda i, j: (0, 0)),
            pl.BlockSpec((k, tn), lambda i, j: (0, j)),
        ],
        out_specs=pl.BlockSpec((tm, tn), lambda i, j: (i, j)),
        out_shape=jax.ShapeDtypeStruct((m, n), out_dtype),
        scratch_shapes=[pltpu.VMEM((tm, k), BF16)],
        compiler_params=_params(("parallel", "arbitrary"), 56),
        name=name,
    )(x, gain.reshape(1, k).astype(F32), w)


def _out_proj_kernel(a_ref, b_ref, c_ref, d_ref, g_ref, w_ref, r_ref, o_ref, xn_ref):
    @pl.when(pl.program_id(1) == 0)
    def _():
        for i, ref in enumerate((a_ref, b_ref, c_ref, d_ref)):
            x = ref[...]
            ms = jnp.mean(x * x, axis=-1, keepdims=True)
            g = g_ref[:, i * GROUP_WIDTH:(i + 1) * GROUP_WIDTH]
            xn_ref[:, i * GROUP_WIDTH:(i + 1) * GROUP_WIDTH] = (x * lax.rsqrt(ms + NORM_EPS) * g).astype(BF16)

    o_ref[...] = r_ref[...] + _dot(xn_ref[...], w_ref[...])


def _out_proj(outs, gain, w, resid, *, tm=512, tn=1024):
    m, d = resid.shape
    tm = min(tm, m)
    k = 4 * GROUP_WIDTH
    grp = pl.BlockSpec((tm, GROUP_WIDTH), lambda i, j: (i, 0))
    return pl.pallas_call(
        _out_proj_kernel,
        grid=(m // tm, d // tn),
        in_specs=[grp, grp, grp, grp,
                  pl.BlockSpec((1, k), lambda i, j: (0, 0)),
                  pl.BlockSpec((k, tn), lambda i, j: (0, j)),
                  pl.BlockSpec((tm, tn), lambda i, j: (i, j))],
        out_specs=pl.BlockSpec((tm, tn), lambda i, j: (i, j)),
        out_shape=jax.ShapeDtypeStruct((m, d), F32),
        scratch_shapes=[pltpu.VMEM((tm, k), BF16)],
        compiler_params=_params(("parallel", "arbitrary"), 56),
        name="out_proj",
    )(*outs, gain.reshape(1, k).astype(F32), w, resid)


def _fox_prep_kernel(x_ref, b_ref, cumt_ref, carry_ref, *, ts):
    @pl.when(pl.program_id(1) == 0)
    def _():
        carry_ref[...] = jnp.zeros_like(carry_ref)

    z = x_ref[...] + b_ref[...]
    lf = jnp.minimum(z, 0.0) - jnp.log1p(jnp.exp(-jnp.abs(z)))
    row = lax.broadcasted_iota(I32, (ts, ts), 0)
    col = lax.broadcasted_iota(I32, (ts, ts), 1)
    tri = jnp.where(col <= row, 1.0, 0.0).astype(BF16)
    hi, mid, lo = _split3(lf)
    cum = (_dot(tri, hi) + _dot(tri, mid)) + _dot(tri, lo) + carry_ref[...]
    carry_ref[...] = cum[ts - 1:ts, :]
    cum_t = jnp.concatenate([cum[i * LANES:(i + 1) * LANES, :].T for i in range(ts // LANES)], axis=1)
    cumt_ref[0] = cum_t[MISC_FORGET_LANE:MISC_FORGET_LANE + FOX_HEADS, :]


def _fox_prep(small, f_bias, bsz, seq, *, ts=512):
    ts = min(ts, seq)
    n = bsz * seq
    ns = seq // ts
    bias = jnp.zeros((1, LANES), F32).at[0, MISC_FORGET_LANE:MISC_FORGET_LANE + FOX_HEADS].set(f_bias.astype(F32))
    misc_blk = SMALL_MISC // LANES
    return pl.pallas_call(
        functools.partial(_fox_prep_kernel, ts=ts),
        grid=(bsz, ns),
        in_specs=[pl.BlockSpec((ts, LANES), lambda b, i: (b * ns + i, misc_blk)),
                  pl.BlockSpec((1, LANES), lambda b, i: (0, 0))],
        out_specs=pl.BlockSpec((1, FOX_HEADS, ts), lambda b, i: (b, 0, i)),
        out_shape=jax.ShapeDtypeStruct((bsz, FOX_HEADS, seq), F32),
        scratch_shapes=[pltpu.VMEM((1, LANES), F32)],
        compiler_params=_params(("parallel", "arbitrary")),
        name="fox_prep",
    )(small, bias)


def _with_ones(v):
    return jnp.concatenate([v, jnp.ones((v.shape[0], LANES), v.dtype)], axis=1)


def _flash_rows(s2, v1, m_sc, l_sc, acc_sc, r0, rows):
    sl = slice(r0, r0 + rows)
    dv = v1.shape[1] - LANES
    m_prev = m_sc[sl]
    m_new = jnp.maximum(m_prev, jnp.max(s2, axis=1, keepdims=True))
    alpha = jnp.exp2(m_prev - m_new)
    p = jnp.concatenate([jnp.exp2(s2[:, j * LANES:(j + 1) * LANES] - m_new)
                         for j in range(s2.shape[1] // LANES)], axis=1)
    pv = _dot(p.astype(BF16), v1)
    l_sc[sl] = alpha * l_sc[sl] + pv[:, dv:]
    acc_sc[sl] = alpha * acc_sc[sl] + pv[:, :dv]
    m_sc[sl] = m_new


def _flash_init(m_sc, l_sc, acc_sc):
    m_sc[...] = jnp.full(m_sc.shape, NEG_INF, F32)
    l_sc[...] = jnp.zeros(l_sc.shape, F32)
    acc_sc[...] = jnp.zeros(acc_sc.shape, F32)


def _pipelined_sweep(n_full, issue, softmax_pv, last_tile, sa_sc, sb_sc):
    def full_tile(ki, src, dst):
        issue(ki + 1, dst)
        softmax_pv(ki, src)

    issue(0, sa_sc)

    def pair(j, carry):
        full_tile(2 * j, sa_sc, sb_sc)
        full_tile(2 * j + 1, sb_sc, sa_sc)
        return carry

    lax.fori_loop(0, n_full // 2, pair, 0)

    @pl.when(n_full % 2 == 1)
    def _():
        full_tile(n_full - 1, sa_sc, sb_sc)
        last_tile(sb_sc)

    @pl.when(n_full % 2 == 0)
    def _():
        last_tile(sa_sc)


def _causal_sweep(qi, tq, score_fn, load_v, sa_sc, sb_sc, m_sc, l_sc, acc_sc):
    nch = tq // ROW_CHUNK
    chunk = lambda c: slice(c * ROW_CHUNK, (c + 1) * ROW_CHUNK)
    score_rows = min(SCORE_ROWS, tq)

    def issue(ki, dst):
        for c in range(tq // score_rows):
            rows = slice(c * score_rows, (c + 1) * score_rows)
            dst[rows, :] = score_fn(rows, ki)

    def softmax_pv(ki, src):
        v = _with_ones(load_v(ki, tq))
        for c in range(nch):
            _flash_rows(src[chunk(c), :], v, m_sc, l_sc, acc_sc, c * ROW_CHUNK, ROW_CHUNK)

    def diag_tile(src):
        for c in range(nch):
            ncols = (c + 1) * ROW_CHUNK
            row = lax.broadcasted_iota(I32, (ROW_CHUNK, ncols), 0) + c * ROW_CHUNK
            col = lax.broadcasted_iota(I32, (ROW_CHUNK, ncols), 1)
            s2 = jnp.where(col <= row, src[chunk(c), 0:ncols], NEG_INF)
            _flash_rows(s2, _with_ones(load_v(qi, ncols)), m_sc, l_sc, acc_sc, c * ROW_CHUNK, ROW_CHUNK)

    _pipelined_sweep(qi, issue, softmax_pv, diag_tile, sa_sc, sb_sc)


def _fox_attn_kernel(q_ref, k_ref, v_ref, ck_ref, o_ref, sa_sc, sb_sc, m_sc, l_sc, acc_sc, *, tq, scale):
    qi = pl.program_id(2)
    _flash_init(m_sc, l_sc, acc_sc)
    c0 = ck_ref[0, pl.ds(qi, 1), :][:, 0:1]

    def score(rows, ki):
        k0 = pl.multiple_of(ki * tq, tq)
        k = k_ref[pl.ds(k0, tq), :]
        ck2 = (ck_ref[0, pl.ds(ki, 1), :] - c0) * LOG2E
        return _dot_nt(q_ref[rows, :], k) * (scale * LOG2E) - ck2

    def load_v(ki, ncols):
        return v_ref[pl.ds(pl.multiple_of(ki * tq, tq), ncols), :]

    _causal_sweep(qi, tq, score, load_v, sa_sc, sb_sc, m_sc, l_sc, acc_sc)
    o_ref[...] = acc_sc[...] / l_sc[...]


def _fox_attn(main, cumt, bsz, seq, *, tq=512):
    tq = min(tq, seq)
    nq = seq // tq
    n = bsz * seq
    d = FOX_HEAD_DIM
    qb, kb, vb = MAIN_BQKV // d, MAIN_BQKV // d + FOX_HEADS, MAIN_BQKV // d + 2 * FOX_HEADS
    cumt3 = cumt.reshape(bsz * FOX_HEADS, nq, tq)
    return pl.pallas_call(
        functools.partial(_fox_attn_kernel, tq=tq, scale=d ** -0.5),
        grid=(bsz, FOX_HEADS, nq),
        in_specs=[pl.BlockSpec((tq, d), lambda b, h, i: (b * nq + i, qb + h)),
                  pl.BlockSpec((seq, d), lambda b, h, i: (b, kb + h)),
                  pl.BlockSpec((seq, d), lambda b, h, i: (b, vb + h)),
                  pl.BlockSpec((1, nq, tq), lambda b, h, i: (b * FOX_HEADS + h, 0, 0))],
        out_specs=pl.BlockSpec((tq, d), lambda b, h, i: (b * nq + i, h)),
        out_shape=jax.ShapeDtypeStruct((n, FOX_HEADS * d), F32),
        scratch_shapes=[pltpu.VMEM((tq, tq), F32), pltpu.VMEM((tq, tq), F32),
                        pltpu.VMEM((tq, LANES), F32), pltpu.VMEM((tq, LANES), F32), pltpu.VMEM((tq, d), F32)],
        compiler_params=_params(("parallel", "parallel", "arbitrary")),
        name="fox_attn",
    )(main, main, main, cumt3)


def _mla_attn_kernel(qn_ref, qr_ref, kn_ref, kr_ref, v_ref, o_ref, q_sc, sa_sc, sb_sc, m_sc, l_sc, acc_sc,
                     *, tq, scale):
    qi = pl.program_id(2)
    _flash_init(m_sc, l_sc, acc_sc)
    q_sc[:, :LANES] = qn_ref[...].astype(BF16)
    q_sc[:, LANES:] = qr_ref[...]

    def score(rows, ki):
        k0 = pl.multiple_of(ki * tq, tq)
        k = jnp.concatenate([kn_ref[pl.ds(k0, tq), :], kr_ref[pl.ds(k0, tq), :]], axis=1)
        return _dot_nt(q_sc[rows, :], k) * (scale * LOG2E)

    def load_v(ki, ncols):
        return v_ref[pl.ds(pl.multiple_of(ki * tq, tq), ncols), :]

    _causal_sweep(qi, tq, score, load_v, sa_sc, sb_sc, m_sc, l_sc, acc_sc)
    o_ref[...] = acc_sc[...] / l_sc[...]


def _mla_attn(qup, qrope, kvup, krope, bsz, seq, *, tq=512):
    tq = min(tq, seq)
    nq = seq // tq
    n = bsz * seq
    d = LANES
    return pl.pallas_call(
        functools.partial(_mla_attn_kernel, tq=tq, scale=(MLA_NOPE_DIM + MLA_ROPE_DIM) ** -0.5),
        grid=(bsz, MLA_HEADS, nq),
        in_specs=[pl.BlockSpec((tq, d), lambda b, h, i: (b * nq + i, h)),
                  pl.BlockSpec((tq, d), lambda b, h, i: (b * nq + i, h)),
                  pl.BlockSpec((seq, d), lambda b, h, i: (b, 2 * h)),
                  pl.BlockSpec((seq, d), lambda b, h, i: (b, 0)),
                  pl.BlockSpec((seq, d), lambda b, h, i: (b, 2 * h + 1))],
        out_specs=pl.BlockSpec((tq, d), lambda b, h, i: (b * nq + i, h)),
        out_shape=jax.ShapeDtypeStruct((n, MLA_HEADS * MLA_V_DIM), F32),
        scratch_shapes=[pltpu.VMEM((tq, 2 * LANES), BF16),
                        pltpu.VMEM((tq, tq), F32), pltpu.VMEM((tq, tq), F32),
                        pltpu.VMEM((tq, LANES), F32), pltpu.VMEM((tq, LANES), F32), pltpu.VMEM((tq, d), F32)],
        compiler_params=_params(("parallel", "parallel", "arbitrary")),
        name="mla_attn",
    )(qup, qrope, kvup, krope, kvup)


def _rope_kernel(q_ref, k_ref, cos_ref, sin_ref, qo_ref, ko_ref):
    half = MLA_ROPE_DIM // 2

    def rope(x):
        lane = lax.broadcasted_iota(I32, x.shape, 1)
        first = (lane % MLA_ROPE_DIM) < half
        swapped = jnp.where(first, pltpu.roll(x, LANES - half, 1), pltpu.roll(x, half, 1))
        return x * cos_ref[...] + swapped * sin_ref[...]

    for t in range(q_ref.shape[1] // LANES):
        qo_ref[:, t * LANES:(t + 1) * LANES] = rope(q_ref[:, t * LANES:(t + 1) * LANES]).astype(BF16)
    k = rope(k_ref[...])
    lane = lax.broadcasted_iota(I32, k.shape, 1)
    ko_ref[...] = jnp.where(lane < MLA_ROPE_DIM, k, 0.0).astype(BF16)


def _rope(qup, small, bsz, seq, *, ts=512):
    ts = min(ts, seq)
    ns = seq // ts
    n = bsz * seq
    pos = jnp.arange(seq, dtype=F32)
    inv = ROPE_THETA ** (-jnp.arange(0, MLA_ROPE_DIM, 2, dtype=F32) / MLA_ROPE_DIM)
    ang = pos[:, None] * inv[None, :]
    cos, sin = jnp.cos(ang), jnp.sin(ang)
    cos_t = jnp.concatenate([cos, cos, cos, cos], axis=1)
    sin_t = jnp.concatenate([-sin, sin, -sin, sin], axis=1)
    w = MLA_HEADS * LANES
    return pl.pallas_call(
        _rope_kernel,
        grid=(bsz, ns),
        in_specs=[pl.BlockSpec((ts, w), lambda b, i: (b * ns + i, 1)),
                  pl.BlockSpec((ts, LANES), lambda b, i: (b * ns + i, SMALL_MISC // LANES)),
                  pl.BlockSpec((ts, LANES), lambda b, i: (i, 0)),
                  pl.BlockSpec((ts, LANES), lambda b, i: (i, 0))],
        out_specs=[pl.BlockSpec((ts, w), lambda b, i: (b * ns + i, 0)),
                   pl.BlockSpec((ts, LANES), lambda b, i: (b * ns + i, 0))],
        out_shape=[jax.ShapeDtypeStruct((n, w), BF16), jax.ShapeDtypeStruct((n, LANES), BF16)],
        compiler_params=_params(("parallel", "parallel")),
        name="mla_rope",
    )(qup, small, cos_t, sin_t)


def _banded_kernel(*refs, tq, nr, hd, window, slab, scale, has_sinks, seq):
    if has_sinks:
        slopes_ref, sinks_ref, q_ref, k_ref, v_ref, o_ref = refs
    else:
        slopes_ref, q_ref, k_ref, v_ref, o_ref = refs
    g = pl.program_id(1)
    qi = pl.program_id(2)
    q0 = qi * tq
    start = pl.multiple_of(jnp.minimum(jnp.maximum(q0 + tq - slab, 0), seq - slab), tq)
    kt = k_ref[pl.ds(start, slab), :]
    v1 = _with_ones(v_ref[pl.ds(start, slab), :])
    q = q_ref[...]
    low = lax.broadcasted_iota(I32, (tq, LANES), 1) < hd
    if hd == LANES:
        qs = jnp.concatenate([q[:, r * hd:(r + 1) * hd] for r in range(nr)], axis=0)
    else:
        zero = jnp.zeros((), q.dtype)
        parts = []
        for j in range(nr // 2):
            pair = q[:, j * LANES:(j + 1) * LANES]
            parts += [jnp.where(low, pair, zero), jnp.where(low, zero, pair)]
        qs = jnp.concatenate(parts, axis=0)
    s = _dot_nt(qs, kt) * (scale * LOG2E)
    qpos = q0 + lax.broadcasted_iota(I32, (tq, slab), 0)
    kpos = start + lax.broadcasted_iota(I32, (tq, slab), 1)
    dist = qpos - kpos
    valid = (dist >= 0) & (dist < window)
    krel = (start - q0 + lax.broadcasted_iota(I32, (1, slab), 1)).astype(F32)
    qrel = lax.broadcasted_iota(I32, (tq, 1), 0).astype(F32)
    es, sinks_e = [], []
    for r in range(nr):
        slope2 = slopes_ref[g * nr + r] * LOG2E
        sr = jnp.where(valid, s[r * tq:(r + 1) * tq] + slope2 * krel, NEG_INF)
        m = jnp.max(sr, axis=1, keepdims=True)
        if has_sinks:
            sk = sinks_ref[g * nr + r] * LOG2E + slope2 * qrel
            m = jnp.maximum(m, sk)
            sinks_e.append(jnp.exp2(sk - m))
        es.append(jnp.exp2(sr - m).astype(BF16))
    pv = _dot(jnp.concatenate(es, axis=0), v1)
    den = pv[:, LANES:]
    if has_sinks:
        den = den + jnp.concatenate(sinks_e, axis=0)
    o = pv[:, :LANES] / den
    if hd == LANES:
        o_ref[...] = jnp.concatenate([o[r * tq:(r + 1) * tq] for r in range(nr)], axis=1)
    else:
        o_ref[...] = jnp.concatenate(
            [jnp.where(low, o[2 * j * tq:(2 * j + 1) * tq], o[(2 * j + 1) * tq:(2 * j + 2) * tq])
             for j in range(nr // 2)], axis=1)


def _banded_attn(main, slopes, sinks, bsz, seq, *, ng, nr, hd, window, q_col, k_col, v_col, name, tq=128):
    tq = min(tq, seq)
    nq = seq // tq
    n = bsz * seq
    slab = min(-(-(window - 1) // tq) * tq + tq, seq)
    qw = nr * hd
    has_sinks = sinks is not None
    smem = pl.BlockSpec(memory_space=pltpu.SMEM)
    in_specs = [smem] + ([smem] if has_sinks else []) + [
        pl.BlockSpec((tq, qw), lambda b, g, i: (b * nq + i, q_col // qw + g)),
        pl.BlockSpec((seq, LANES), lambda b, g, i: (b, k_col // LANES + g)),
        pl.BlockSpec((seq, LANES), lambda b, g, i: (b, v_col // LANES + g))]
    args = [slopes] + ([sinks.astype(F32)] if has_sinks else []) + [main, main, main]
    return pl.pallas_call(
        functools.partial(_banded_kernel, tq=tq, nr=nr, hd=hd, window=window, slab=slab,
                          scale=hd ** -0.5, has_sinks=has_sinks, seq=seq),
        grid=(bsz, ng, nq),
        in_specs=in_specs,
        out_specs=pl.BlockSpec((tq, qw), lambda b, g, i: (b * nq + i, g)),
        out_shape=jax.ShapeDtypeStruct((n, ng * qw), F32),
        compiler_params=_params(("parallel", "parallel", "parallel")),
        name=name,
    )(*args)


def _compress_kernel(x_ref, pos_ref, w1_ref, w2_ref, o_ref, xf_ref, *, seq):
    nc = seq // NSA_CMP_STRIDE
    hd = NSA_HEAD_DIM
    xf_ref[0:seq, :] = x_ref[...].astype(F32)
    xf_ref[seq:seq + NSA_CMP_STRIDE, :] = jnp.zeros((NSA_CMP_STRIDE, hd), F32)
    acc = jnp.zeros((nc, hd), F32)
    for j in range(NSA_CMP_LEN):
        rows = xf_ref[pl.ds(j, nc, stride=NSA_CMP_STRIDE), :] + pos_ref[0, j:j + 1, :]
        acc = acc + _dot(rows.astype(BF16), w1_ref[0, j * hd:(j + 1) * hd, :])
    hid = jax.nn.gelu(acc)
    o_ref[0, 0, 0] = _dot(hid.astype(BF16), w2_ref[0]).astype(BF16)


def _compress(main, pos, w1, w2, bsz, seq):
    nc = seq // NSA_CMP_STRIDE
    hd = NSA_HEAD_DIM
    ng = NSA_KV_HEADS
    col0 = MAIN_AKV // hd
    return pl.pallas_call(
        functools.partial(_compress_kernel, seq=seq),
        grid=(bsz, ng, 2),
        in_specs=[pl.BlockSpec((seq, hd), lambda b, g, t: (b, col0 + t * ng + g)),
                  pl.BlockSpec((1, NSA_CMP_LEN, hd), lambda b, g, t: (t, 0, 0)),
                  pl.BlockSpec((1, NSA_CMP_LEN * hd, hd), lambda b, g, t: (t, 0, 0)),
                  pl.BlockSpec((1, hd, hd), lambda b, g, t: (t, 0, 0))],
        out_specs=pl.BlockSpec((1, 1, 1, nc, hd), lambda b, g, t: (b, g, t, 0, 0)),
        out_shape=jax.ShapeDtypeStruct((bsz, ng, 2, nc, hd), BF16),
        scratch_shapes=[pltpu.VMEM((seq + NSA_CMP_STRIDE, hd), F32)],
        compiler_params=_params(("parallel", "parallel", "parallel")),
        name="nsa_compress",
    )(main, pos, w1, w2)


def _nsa_cmp_kernel(slopes_ref, q_ref, kc_ref, vc_ref, o_ref, sel_ref, *, tq, nc, nr):
    g = pl.program_id(1)
    qi = pl.program_id(2)
    hd = NSA_HEAD_DIM
    q0 = qi * tq
    q = q_ref[...]
    qs = jnp.concatenate([q[:, r * hd:(r + 1) * hd] for r in range(nr)], axis=0)
    kc = kc_ref[0, 0, 0]
    vc = vc_ref[0, 0, 0]
    s = _dot_nt(qs, kc) * (hd ** -0.5)
    tpos = q0 + lax.broadcasted_iota(I32, (tq, nc), 0)
    cend = lax.broadcasted_iota(I32, (tq, nc), 1) * NSA_CMP_STRIDE + (NSA_CMP_LEN - 1)
    dist = tpos - cend
    valid = dist >= 0
    distf = dist.astype(F32)
    ps = []
    psum = jnp.zeros((tq, nc), F32)
    for r in range(nr):
        sr = s[r * tq:(r + 1) * tq] - slopes_ref[g * nr + r] * distf
        sr = jnp.where(valid, sr, NEG_INF)
        m = jnp.max(sr, axis=1, keepdims=True)
        e = jnp.exp(sr - m)
        p = jnp.where(valid, e / jnp.sum(e, axis=1, keepdims=True), 0.0)
        psum = psum + p
        ps.append(p.astype(BF16))
    o = _dot(jnp.concatenate(ps, axis=0), vc)
    o_ref[...] = jnp.concatenate([o[r * tq:(r + 1) * tq] for r in range(nr)], axis=1)

    nb = LANES
    n_slc = nc * NSA_CMP_STRIDE // NSA_SEL_BLOCK
    per = NSA_SEL_BLOCK // NSA_CMP_STRIDE
    blk = lax.broadcasted_iota(I32, (nb, nc), 0)
    cidx = lax.broadcasted_iota(I32, (nb, nc), 1)
    overlap = (cidx <= per * blk + per - 1) & (cidx >= per * blk - 1) & (cidx < nc - 1) & (blk < n_slc)
    ov = jnp.where(overlap, 1.0, 0.0).astype(BF16)
    hi, mid, lo = _split3(psum)
    imp = (_dot_nt(ov, hi) + _dot_nt(ov, mid)) + _dot_nt(ov, lo)
    j = lax.broadcasted_iota(I32, (nb, tq), 0)
    cur = (q0 + lax.broadcasted_iota(I32, (nb, tq), 1)) // NSA_SEL_BLOCK
    forced = (j == 0) | (j == cur) | (j == cur - 1)
    imp = jnp.where(forced, 1e6, imp)
    imp = jnp.where(j > cur, -1e6, imp)
    imp = jnp.where(j >= n_slc, -3e38, imp)
    rank = jnp.zeros((nb, tq), F32)
    for i in range(n_slc):
        vi = imp[i:i + 1, :]
        ahead = (vi > imp) | ((vi == imp) & (j > i))
        rank = rank + jnp.where(ahead, 1.0, 0.0)
    sel = jnp.where(rank < float(min(NSA_TOP_N, n_slc)), 0.0, NEG_INF)
    sel_ref[...] = sel.T.astype(BF16)


def _nsa_cmp(main, kvc, slopes, bsz, seq, *, tq=128):
    tq = min(tq, seq)
    nq = seq // tq
    n = bsz * seq
    ng, nr, hd = NSA_KV_HEADS, NSA_HEADS // NSA_KV_HEADS, NSA_HEAD_DIM
    nc = seq // NSA_CMP_STRIDE
    qw = nr * hd
    smem = pl.BlockSpec(memory_space=pltpu.SMEM)
    return pl.pallas_call(
        functools.partial(_nsa_cmp_kernel, tq=tq, nc=nc, nr=nr),
        grid=(bsz, ng, nq),
        in_specs=[smem,
                  pl.BlockSpec((tq, qw), lambda b, g, i: (b * nq + i, g)),
                  pl.BlockSpec((1, 1, 1, nc, hd), lambda b, g, i: (b, g, 0, 0, 0)),
                  pl.BlockSpec((1, 1, 1, nc, hd), lambda b, g, i: (b, g, 1, 0, 0))],
        out_specs=[pl.BlockSpec((tq, qw), lambda b, g, i: (b * nq + i, g)),
                   pl.BlockSpec((tq, LANES), lambda b, g, i: ((b * ng + g) * nq + i, 0))],
        out_shape=[jax.ShapeDtypeStruct((n, ng * qw), F32),
                   jax.ShapeDtypeStruct((bsz * ng * seq, LANES), BF16)],
        compiler_params=_params(("parallel", "parallel", "parallel")),
        name="nsa_cmp",
    )(slopes, main, kvc, kvc)


def _nsa_slc_kernel(slopes_ref, q_ref, k_ref, v_ref, sel_ref, o_ref, sa_sc, sb_sc, m_sc, l_sc, acc_sc,
                    *, tq, tk, nr):
    g = pl.program_id(1)
    qi = pl.program_id(2)
    hd = NSA_HEAD_DIM
    q0 = qi * tq
    _flash_init(m_sc, l_sc, acc_sc)
    last = q0 // tk
    c2 = (hd ** -0.5) * LOG2E
    head = lambda r: slice(r * tq, (r + 1) * tq)

    shift = int(math.log2(NSA_SEL_BLOCK))
    lane_minus_blk = (lax.broadcasted_iota(I32, (tk, LANES), 1)
                      - lax.shift_right_logical(lax.broadcasted_iota(I32, (tk, LANES), 0), shift))

    def issue(ki, dst):
        k0 = pl.multiple_of(ki * tk, tk)
        onehot = jnp.where(lane_minus_blk == ki * (tk // NSA_SEL_BLOCK), 1.0, 0.0).astype(BF16)
        ka = jnp.concatenate([k_ref[pl.ds(k0, tk), :], onehot], axis=1)
        rel = (k0 - q0 + lax.broadcasted_iota(I32, (1, tk), 1)).astype(F32)
        for r in range(nr):
            qa = jnp.concatenate([q_ref[:, r * hd:(r + 1) * hd], sel_ref[...]], axis=1)
            dst[head(r), :] = _dot_nt(qa, ka) * c2 + (slopes_ref[g * nr + r] * LOG2E) * rel

    def load_v(ki):
        return _with_ones(v_ref[pl.ds(pl.multiple_of(ki * tk, tk), tk), :])

    def softmax_pv(ki, src):
        v = load_v(ki)
        for r in range(nr):
            _flash_rows(src[head(r), :], v, m_sc, l_sc, acc_sc, r * tq, tq)

    def last_tile(src):
        v = load_v(last)
        k0 = last * tk
        ahead = (k0 + lax.broadcasted_iota(I32, (tq, tk), 1)) > (q0 + lax.broadcasted_iota(I32, (tq, tk), 0))
        for r in range(nr):
            _flash_rows(jnp.where(ahead, NEG_INF, src[head(r), :]), v, m_sc, l_sc, acc_sc, r * tq, tq)

    _pipelined_sweep(last, issue, softmax_pv, last_tile, sa_sc, sb_sc)
    o = acc_sc[...] / l_sc[...]
    o_ref[...] = jnp.concatenate([o[r * tq:(r + 1) * tq] for r in range(nr)], axis=1)


def _nsa_slc(main, sel, slopes, bsz, seq, *, tq=128, tk=512):
    tq = min(tq, seq)
    tk = min(tk, seq)
    nq = seq // tq
    n = bsz * seq
    ng, nr, hd = NSA_KV_HEADS, NSA_HEADS // NSA_KV_HEADS, NSA_HEAD_DIM
    qw = nr * hd
    kb = MAIN_AKV // hd + 2 * ng
    vb = MAIN_AKV // hd + 3 * ng
    smem = pl.BlockSpec(memory_space=pltpu.SMEM)
    return pl.pallas_call(
        functools.partial(_nsa_slc_kernel, tq=tq, tk=tk, nr=nr),
        grid=(bsz, ng, nq),
        in_specs=[smem,
                  pl.BlockSpec((tq, qw), lambda b, g, i: (b * nq + i, g)),
                  pl.BlockSpec((seq, hd), lambda b, g, i: (b, kb + g)),
                  pl.BlockSpec((seq, hd), lambda b, g, i: (b, vb + g)),
                  pl.BlockSpec((tq, LANES), lambda b, g, i: ((b * ng + g) * nq + i, 0))],
        out_specs=pl.BlockSpec((tq, qw), lambda b, g, i: (b * nq + i, g)),
        out_shape=jax.ShapeDtypeStruct((n, ng * qw), F32),
        scratch_shapes=[pltpu.VMEM((nr * tq, tk), F32), pltpu.VMEM((nr * tq, tk), F32),
                        pltpu.VMEM((nr * tq, LANES), F32), pltpu.VMEM((nr * tq, LANES), F32),
                        pltpu.VMEM((nr * tq, hd), F32)],
        compiler_params=_params(("parallel", "parallel", "arbitrary")),
        name="nsa_slc",
    )(slopes, main, main, main, sel)


def _nsa_gate_kernel(c_ref, s_ref, w_ref, g_ref, o_ref):
    gates = jax.nn.sigmoid(g_ref[...])
    hd = NSA_HEAD_DIM
    for h in range(NSA_HEADS):
        sl = slice(h * hd, (h + 1) * hd)
        lane = MISC_GATE_LANE + h
        o_ref[:, sl] = (gates[:, lane:lane + 1] * c_ref[:, sl]
                        + gates[:, lane + NSA_HEADS:lane + NSA_HEADS + 1] * s_ref[:, sl]
                        + gates[:, lane + 2 * NSA_HEADS:lane + 2 * NSA_HEADS + 1] * w_ref[:, sl])


def _nsa_gate(o_cmp, o_slc, o_win, small, *, tm=512):
    n, w = o_cmp.shape
    tm = min(tm, n)
    row = pl.BlockSpec((tm, w), lambda i: (i, 0))
    return pl.pallas_call(
        _nsa_gate_kernel,
        grid=(n // tm,),
        in_specs=[row, row, row, pl.BlockSpec((tm, LANES), lambda i: (i, SMALL_MISC // LANES))],
        out_specs=row,
        out_shape=jax.ShapeDtypeStruct((n, w), F32),
        compiler_params=_params(("parallel",)),
        name="nsa_gate",
    )(o_cmp, o_slc, o_win, small)


def _router_kernel(x_ref, g_ref, wh_ref, wm_ref, br_ref, xn_ref, route_ref, gate_ref, *, tm):
    x = x_ref[...]
    ms = jnp.mean(x * x, axis=-1, keepdims=True)
    xn = x * lax.rsqrt(ms + NORM_EPS) * g_ref[...]
    xn_ref[...] = xn
    xh = xn.astype(BF16)
    xm = (xn - xh.astype(F32)).astype(BF16)
    wh = wh_ref[...]
    lg_rows = _dot(xh, wh) + (_dot(xm, wh) + _dot(xh, wm_ref[...]))
    logits = jnp.concatenate([lg_rows[i * LANES:(i + 1) * LANES, :].T for i in range(tm // LANES)],
                             axis=1) + br_ref[:, 0:1]
    ng, ne = N_GROUPS, EXPERTS_PER_GROUP
    lg = logits[0:ng, :]
    sub = lax.broadcasted_iota(I32, (ng, tm), 0)
    mg = jnp.max(lg, axis=0, keepdims=True)
    eg = jnp.exp(lg - mg)
    pg = eg / jnp.sum(eg, axis=0, keepdims=True)
    pg_top = jnp.max(pg, axis=0, keepdims=True)
    g_idx = jnp.min(jnp.where(pg == pg_top, sub, ng), axis=0, keepdims=True)
    le = jnp.zeros((ne, tm), F32)
    for gi in range(ng):
        le = jnp.where(g_idx == gi, logits[ng + gi * ne:ng + (gi + 1) * ne, :], le)
    v1 = jnp.max(le, axis=0, keepdims=True)
    i1 = jnp.min(jnp.where(le == v1, sub, ne), axis=0, keepdims=True)
    rest = jnp.where(sub == i1, -jnp.inf, le)
    v2 = jnp.max(rest, axis=0, keepdims=True)
    i2 = jnp.min(jnp.where(rest == v2, sub, ne), axis=0, keepdims=True)
    e2 = jnp.exp(v2 - v1)
    den = 1.0 + e2
    w1 = pg_top * (1.0 / den)
    w2 = pg_top * (e2 / den)
    ex1 = (g_idx * ne + i1).astype(F32)
    ex2 = (g_idx * ne + i2).astype(F32)
    zero = jnp.zeros((1, tm), F32)
    route = jnp.concatenate([ex1, ex2, w1, w2, zero, zero, zero, zero], axis=0)
    route_ref[...] = route
    pad = jnp.concatenate([route, jnp.zeros((LANES - 8, tm), F32)], axis=0)
    gate_ref[...] = jnp.concatenate([pad[:, i * LANES:(i + 1) * LANES].T for i in range(tm // LANES)], axis=0)


def _router(x, gain, rg_w, rg_b, re_w, re_b, *, tm=256):
    n, d = x.shape
    tm = min(tm, n)
    nl = N_GROUPS + N_EXPERTS
    wr = jnp.pad(jnp.concatenate([rg_w, re_w], axis=1).astype(F32), ((0, 0), (0, LANES - nl)))
    br = jnp.zeros((LANES, LANES), F32).at[:nl, :].set(
        jnp.concatenate([rg_b, re_b]).astype(F32)[:, None] * jnp.ones((1, LANES), F32))
    wr_hi = wr.astype(BF16)
    wr_mid = (wr - wr_hi.astype(F32)).astype(BF16)
    return pl.pallas_call(
        functools.partial(_router_kernel, tm=tm),
        grid=(n // tm,),
        in_specs=[pl.BlockSpec((tm, d), lambda i: (i, 0)),
                  pl.BlockSpec((1, d), lambda i: (0, 0)),
                  pl.BlockSpec((d, LANES), lambda i: (0, 0)),
                  pl.BlockSpec((d, LANES), lambda i: (0, 0)),
                  pl.BlockSpec((LANES, LANES), lambda i: (0, 0))],
        out_specs=[pl.BlockSpec((tm, d), lambda i: (i, 0)),
                   pl.BlockSpec((8, tm), lambda i: (0, i)),
                   pl.BlockSpec((tm, LANES), lambda i: (i, 0))],
        out_shape=[jax.ShapeDtypeStruct((n, d), F32),
                   jax.ShapeDtypeStruct((8, n), F32),
                   jax.ShapeDtypeStruct((n, LANES), F32)],
        compiler_params=_params(("parallel",), 48),
        name="moe_router",
    )(x, gain.reshape(1, d).astype(F32), wr_hi, wr_mid, br)


def _slot_kernel(route_ref, pos_ref, cnt_ref, start_ref, carry_ref, *, tm):
    phase = pl.program_id(0)
    i = pl.program_id(1)
    e1 = route_ref[0:1, :].astype(I32)
    e2 = route_ref[1:2, :].astype(I32)
    sub = lax.broadcasted_iota(I32, (N_EXPERTS, tm), 0)
    oh1 = jnp.where(sub == e1, 1.0, 0.0)
    oh2 = jnp.where(sub == e2, 1.0, 0.0)
    ohs = oh1 + oh2

    @pl.when((phase == 0) & (i == 0))
    def _():
        carry_ref[...] = jnp.zeros_like(carry_ref)

    @pl.when(phase == 0)
    def _():
        carry_ref[...] = carry_ref[...] + jnp.sum(ohs, axis=1, keepdims=True)
        pos_ref[...] = jnp.zeros(pos_ref.shape, I32)

    @pl.when((phase == 1) & (i == 0))
    def _():
        cnt = carry_ref[...]
        cnt_ref[...] = cnt
        padded = jnp.floor((cnt + (MOE_ROWS - 1)) / MOE_ROWS) * MOE_ROWS
        row = lax.broadcasted_iota(I32, (N_EXPERTS, N_EXPERTS), 0)
        col = lax.broadcasted_iota(I32, (N_EXPERTS, N_EXPERTS), 1)
        lower = jnp.where(col < row, 1.0, 0.0).astype(BF16)
        hi, mid, lo = _split3(padded)
        start_ref[...] = (_dot(lower, hi) + _dot(lower, mid)) + _dot(lower, lo)
        carry_ref[...] = jnp.zeros_like(carry_ref)

    @pl.when(phase == 1)
    def _():
        row = lax.broadcasted_iota(I32, (tm, tm), 0)
        col = lax.broadcasted_iota(I32, (tm, tm), 1)
        upper = jnp.where(row < col, 1.0, 0.0).astype(BF16)
        before = _dot(ohs.astype(BF16), upper) + (carry_ref[:, 0:1] + start_ref[:, 0:1])
        p1 = jnp.sum(oh1 * before, axis=0, keepdims=True)
        p2 = jnp.sum(oh2 * before, axis=0, keepdims=True)
        zero = jnp.zeros((1, tm), F32)
        pos_ref[...] = jnp.concatenate([p1, p2, zero, zero, zero, zero, zero, zero], axis=0).astype(I32)
        carry_ref[...] = carry_ref[...] + jnp.sum(ohs, axis=1, keepdims=True)


def _slots(route, *, tm=512):
    n = route.shape[1]
    tm = min(tm, n)
    const = pl.BlockSpec((N_EXPERTS, LANES), lambda p, i: (0, 0))
    return pl.pallas_call(
        functools.partial(_slot_kernel, tm=tm),
        grid=(2, n // tm),
        in_specs=[pl.BlockSpec((8, tm), lambda p, i: (0, i))],
        out_specs=[pl.BlockSpec((8, tm), lambda p, i: (0, i * p)), const, const],
        out_shape=[jax.ShapeDtypeStruct((8, n), I32), jax.ShapeDtypeStruct((N_EXPERTS, LANES), F32),
                   jax.ShapeDtypeStruct((N_EXPERTS, LANES), F32)],
        scratch_shapes=[pltpu.VMEM((N_EXPERTS, LANES), F32)],
        compiler_params=_params(("arbitrary", "arbitrary")),
        name="moe_slots",
    )(route)


def _experts_kernel(be_ref, br_ref, dst_ref, dstn_ref, x_hbm, wg_ref, wu_ref, wd_ref, y_hbm,
                    xbuf, xb16, wgu16, wd16, acc, sem_in, sem_out, *, nblk, n_tok):
    i = pl.program_id(0)
    c = pl.program_id(1)
    nch = pl.num_programs(1)
    slot = i % 2
    rows = br_ref[i]

    def in_copy(tok, r, s):
        return pltpu.make_async_copy(x_hbm.at[pl.ds(tok, 1), :], xbuf.at[s, pl.ds(r, 1), :], sem_in.at[s])

    def out_copy(dst, r, s):
        return pltpu.make_async_copy(acc.at[s, pl.ds(r, 1), :], y_hbm.at[pl.ds(dst, 1), :], sem_out.at[s])

    def grouped(n_rows, fn, exact):
        def group(gi, carry):
            for u in range(DMA_UNROLL):
                fn(gi * DMA_UNROLL + u)
            return carry

        def single(r, carry):
            fn(r)
            return carry

        if exact:
            full = n_rows // DMA_UNROLL
            lax.fori_loop(0, full, group, 0)
            lax.fori_loop(full * DMA_UNROLL, n_rows, single, 0)
        else:
            lax.fori_loop(0, (n_rows + DMA_UNROLL - 1) // DMA_UNROLL, group, 0)

    def start_gather(tbl, n_rows, s):
        def fn(r):
            v = tbl[0, 0, r]
            in_copy(jnp.where(v >= n_tok, v - n_tok, v), r, s).start()
        grouped(n_rows, fn, False)

    def wait_gather(n_rows, s):
        grouped(n_rows, lambda r: in_copy(0, r, s).wait(), False)

    def start_scatter(n_rows, s):
        grouped(n_rows, lambda r: out_copy(dst_ref[0, 0, r], r, s).start(), True)

    def wait_scatter(n_rows, s):
        grouped(n_rows, lambda r: out_copy(0, r, s).wait(), True)

    @pl.when((i == 0) & (c == 0))
    def _():
        xbuf[...] = jnp.zeros(xbuf.shape, F32)
        start_gather(dst_ref, rows, 0)

    @pl.when(c == 0)
    def _():
        wait_gather(rows, slot)
        xb16[...] = xbuf[slot].astype(BF16)
        acc[slot] = jnp.zeros(acc.shape[1:], F32)

    @pl.when((c == 1) & (i + 1 < nblk))
    def _():
        start_gather(dstn_ref, br_ref[jnp.minimum(i + 1, nblk - 1)], 1 - slot)

    @pl.when(rows > 0)
    def _():
        wgu16[:, :DE_CHUNK] = wg_ref[0, 0].astype(BF16)
        wgu16[:, DE_CHUNK:] = wu_ref[0, 0].astype(BF16)
        wd16[...] = wd_ref[0, 0].astype(BF16)

    def sub_blocks(n_sub):
        rss = [slice(sb * MOE_SUB_ROWS, (sb + 1) * MOE_SUB_ROWS) for sb in range(n_sub)]
        hgus = [_dot(xb16[rs, :], wgu16[...]) for rs in rss]
        hs = [(jax.nn.silu(hgu[:, :DE_CHUNK]) * hgu[:, DE_CHUNK:]).astype(BF16) for hgu in hgus]
        for rs, h in zip(rss, hs):
            acc[slot, rs, :] += _dot(h, wd16[...])

    n_sub_max = MOE_ROWS // MOE_SUB_ROWS
    for n_sub in range(1, n_sub_max + 1):
        lo = (n_sub - 1) * MOE_SUB_ROWS
        cond = (rows > lo) if n_sub == n_sub_max else ((rows > lo) & (rows <= lo + MOE_SUB_ROWS))
        pl.when(cond)(functools.partial(sub_blocks, n_sub))

    @pl.when(c == nch - 1)
    def _():
        @pl.when(i > 0)
        def _():
            wait_scatter(br_ref[jnp.maximum(i - 1, 0)], 1 - slot)

        start_scatter(rows, slot)

        @pl.when(i == nblk - 1)
        def _():
            wait_scatter(rows, slot)


def _experts(xn, dst_row, blk_e, blk_rows, w_gate, w_up, w_down, layer):
    n, d = xn.shape
    nblk = blk_e.shape[0]
    nch = D_EXPERT // DE_CHUNK
    dst3 = dst_row.reshape(nblk, 1, MOE_ROWS)

    def chunk_of(i, c, br):
        return jnp.where(br[i] > 0, c, nch - 1)

    grid_spec = pltpu.PrefetchScalarGridSpec(
        num_scalar_prefetch=2,
        grid=(nblk, nch),
        in_specs=[
            pl.BlockSpec((1, 1, MOE_ROWS), lambda i, c, be, br: (i, 0, 0), memory_space=pltpu.SMEM),
            pl.BlockSpec((1, 1, MOE_ROWS), lambda i, c, be, br: (jnp.minimum(i + 1, nblk - 1), 0, 0),
                         memory_space=pltpu.SMEM),
            pl.BlockSpec(memory_space=pl.ANY),
            pl.BlockSpec((1, 1, d, DE_CHUNK), lambda i, c, be, br: (layer, be[i], 0, chunk_of(i, c, br))),
            pl.BlockSpec((1, 1, d, DE_CHUNK), lambda i, c, be, br: (layer, be[i], 0, chunk_of(i, c, br))),
            pl.BlockSpec((1, 1, DE_CHUNK, d), lambda i, c, be, br: (layer, be[i], chunk_of(i, c, br), 0)),
        ],
        out_specs=pl.BlockSpec(memory_space=pl.ANY),
        scratch_shapes=[pltpu.VMEM((2, MOE_ROWS, d), F32), pltpu.VMEM((MOE_ROWS, d), BF16),
                        pltpu.VMEM((d, 2 * DE_CHUNK), BF16), pltpu.VMEM((DE_CHUNK, d), BF16),
                        pltpu.VMEM((2, MOE_ROWS, d), F32),
                        pltpu.SemaphoreType.DMA((2,)), pltpu.SemaphoreType.DMA((2,))],
    )
    return pl.pallas_call(
        functools.partial(_experts_kernel, nblk=nblk, n_tok=n),
        grid_spec=grid_spec,
        out_shape=jax.ShapeDtypeStruct((TOP_K * n, d), F32),
        compiler_params=_params(("arbitrary", "arbitrary"), 60),
        name="moe_experts",
    )(blk_e, blk_rows, dst3, dst3, xn, w_gate, w_up, w_down)


def _moe_finish_kernel(x_ref, y0_ref, y1_ref, gate_ref, fg_ref, o_ref, *, final_norm):
    g = gate_ref[...]
    y = x_ref[...] + (g[:, 2:3] * y0_ref[...] + g[:, 3:4] * y1_ref[...])
    if final_norm:
        ms = jnp.mean(y * y, axis=-1, keepdims=True)
        y = y * lax.rsqrt(ms + NORM_EPS) * fg_ref[...]
    o_ref[...] = y


def _moe_finish(y, x, gates, final_gain=None, *, tm=256):
    n, d = x.shape
    tm = min(tm, n)
    nt = n // tm
    final_norm = final_gain is not None
    fg = (final_gain if final_norm else jnp.ones((d,), F32)).reshape(1, d).astype(F32)
    return pl.pallas_call(
        functools.partial(_moe_finish_kernel, final_norm=final_norm),
        grid=(nt,),
        in_specs=[pl.BlockSpec((tm, d), lambda i: (i, 0)),
                  pl.BlockSpec((tm, d), lambda i: (i, 0)),
                  pl.BlockSpec((tm, d), lambda i: (nt + i, 0)),
                  pl.BlockSpec((tm, LANES), lambda i: (i, 0)),
                  pl.BlockSpec((1, d), lambda i: (0, 0))],
        out_specs=pl.BlockSpec((tm, d), lambda i: (i, 0)),
        out_shape=jax.ShapeDtypeStruct((n, d), F32),
        compiler_params=_params(("parallel",), 48),
        name="moe_finish",
    )(x, y, y, gates, fg)


def _in_proj_weights(w_in):
    off = np.concatenate([[0], np.cumsum(IN_SPLITS)])
    seg = lambda i: w_in[:, off[i]:off[i + 1]]
    dkv = seg(9).reshape(-1, 2 * SWA_KV_HEADS, 1, SWA_HEAD_DIM)
    dkv = jnp.broadcast_to(dkv, dkv.shape[:2] + (LANES // SWA_HEAD_DIM, SWA_HEAD_DIM)).reshape(w_in.shape[0], -1)
    w_main = jnp.concatenate([seg(0), seg(1), seg(3), seg(8), dkv], axis=1).astype(BF16)
    pad = jnp.zeros((w_in.shape[0], SMALL_COLS - (SMALL_MISC + 64 + 24 + 8)), w_in.dtype)
    w_small = jnp.concatenate([seg(5), seg(6), seg(7), seg(2), seg(4), pad], axis=1).astype(BF16)
    return w_main, w_small


def _mla_weights(w_uq, w_ukv):
    per = MLA_NOPE_DIM + MLA_ROPE_DIM
    w3 = w_uq.reshape(MLA_Q_RANK, MLA_HEADS, per)
    nope = w3[:, :, :MLA_NOPE_DIM].reshape(MLA_Q_RANK, MLA_HEADS * MLA_NOPE_DIM)
    rope = jnp.pad(w3[:, :, MLA_NOPE_DIM:], ((0, 0), (0, 0), (0, LANES - MLA_ROPE_DIM)))
    rope = rope.reshape(MLA_Q_RANK, MLA_HEADS * LANES)
    return jnp.concatenate([nope, rope], axis=1).astype(BF16), w_ukv.astype(BF16)


def _moe_tables(pos, counts, starts, n_tok):
    n_assign = n_tok * TOP_K
    nblk = (n_assign + N_EXPERTS * (MOE_ROWS - 1)) // MOE_ROWS
    cnt = counts[:, 0].astype(I32)
    pstart = starts[:, 0].astype(I32)
    pend = pstart + (cnt + MOE_ROWS - 1) // MOE_ROWS * MOE_ROWS
    dst_row = jnp.zeros((nblk * MOE_ROWS,), I32).at[pos[0:TOP_K].reshape(-1)].set(jnp.arange(n_assign, dtype=I32))
    row0 = jnp.arange(nblk, dtype=I32) * MOE_ROWS
    blk_e = jnp.minimum(jnp.sum((pend[None, :] <= row0[:, None]).astype(I32), axis=1), N_EXPERTS - 1)
    blk_rows = jnp.clip(cnt[blk_e] - (row0 - pstart[blk_e]), 0, MOE_ROWS).astype(I32)
    return dst_row, blk_e, blk_rows


def kernel(x, norm_mix_g, w_in, nsa_kc_pos, nsa_kc_w1, nsa_kc_w2, nsa_vc_pos, nsa_vc_w1, nsa_vc_w2, fox_f_bias,
           mla_q_norm_g, mla_kv_norm_g, mla_w_uq, mla_w_ukv, swa_sinks, out_norm_g, w_out, norm_ffn_g,
           router_group_w, router_group_b, router_expert_w, router_expert_b, exp_w_gate, exp_w_up, exp_w_down,
           final_norm_g):
    bsz, seq, d_model = x.shape
    n = bsz * seq
    depth = w_in.shape[0]
    xs = x.reshape(n, d_model).astype(F32)
    nsa_slopes = _alibi_slopes(NSA_HEADS)
    swa_slopes = _alibi_slopes(SWA_HEADS)
    nsa_nr = NSA_HEADS // NSA_KV_HEADS
    for l in range(depth):
        w_main, w_small = _in_proj_weights(w_in[l])
        main = _norm_matmul(xs, norm_mix_g[l], w_main, out_dtype=BF16, tn=1024, name="in_proj_main")
        small = _norm_matmul(xs, norm_mix_g[l], w_small, out_dtype=F32, tn=768, name="in_proj_small")

        pos = jnp.stack([nsa_kc_pos[l], nsa_vc_pos[l]]).astype(F32)
        w1 = jnp.stack([nsa_kc_w1[l], nsa_vc_w1[l]]).astype(BF16)
        w2 = jnp.stack([nsa_kc_w2[l], nsa_vc_w2[l]]).astype(BF16)
        kvc = _compress(main, pos, w1, w2, bsz, seq)
        o_cmp, sel = _nsa_cmp(main, kvc, nsa_slopes, bsz, seq)
        o_slc = _nsa_slc(main, sel, nsa_slopes, bsz, seq)
        o_win = _banded_attn(main, nsa_slopes, None, bsz, seq, ng=NSA_KV_HEADS, nr=nsa_nr, hd=NSA_HEAD_DIM,
                             window=NSA_WINDOW, q_col=MAIN_AQ, k_col=MAIN_AKV + 8 * NSA_HEAD_DIM,
                             v_col=MAIN_AKV + 10 * NSA_HEAD_DIM, name="nsa_win")
        out_a = _nsa_gate(o_cmp, o_slc, o_win, small)

        cumt = _fox_prep(small, fox_f_bias[l], bsz, seq)
        out_b = _fox_attn(main, cumt, bsz, seq)

        w_uq, w_ukv = _mla_weights(mla_w_uq[l], mla_w_ukv[l])
        qup = _norm_matmul(small, mla_q_norm_g[l], w_uq, out_dtype=F32, col_off=SMALL_CQ, k=MLA_Q_RANK,
                           name="mla_q_up")
        kvup = _norm_matmul(small, mla_kv_norm_g[l], w_ukv, out_dtype=BF16, col_off=SMALL_CKV, k=MLA_KV_RANK,
                            name="mla_kv_up")
        qrope, krope = _rope(qup, small, bsz, seq)
        out_c = _mla_attn(qup, qrope, kvup, krope, bsz, seq)

        out_d = _banded_attn(main, swa_slopes, swa_sinks[l], bsz, seq, ng=SWA_KV_HEADS,
                             nr=SWA_HEADS // SWA_KV_HEADS, hd=SWA_HEAD_DIM, window=SWA_WINDOW, q_col=MAIN_DQ,
                             k_col=MAIN_DKV, v_col=MAIN_DKV + SWA_KV_HEADS * LANES, name="swa")

        xs = _out_proj((out_a, out_b, out_c, out_d), out_norm_g[l], w_out[l].astype(BF16), xs)

        xn, route, gates = _router(xs, norm_ffn_g[l], router_group_w[l], router_group_b[l],
                                   router_expert_w[l], router_expert_b[l])
        pos_rows, counts, starts = _slots(route)
        dst_row, blk_e, blk_rows = _moe_tables(pos_rows, counts, starts, n)
        y = _experts(xn, dst_row, blk_e, blk_rows, exp_w_gate, exp_w_up, exp_w_down, l)
        xs = _moe_finish(y, xs, gates, final_norm_g if l == depth - 1 else None)
    return xs.reshape(bsz, seq, d_model)
```

```python
import functools
import math

import numpy as np
import jax
import jax.numpy as jnp
from jax import lax
from jax.experimental import pallas as pl
from jax.experimental.pallas import tpu as pltpu

F32 = jnp.float32
BF16 = jnp.bfloat16
I32 = jnp.int32

NEG_INF = -1e30
NORM_EPS = 1e-6
LANES = 128
ROW_CHUNK = 128
SCORE_ROWS = 512
LOG2E = 1.4426950408889634

Q_BLOCK = 128
GROUP_WIDTH = 1024
NSA_HEADS, NSA_KV_HEADS, NSA_HEAD_DIM = 8, 2, 128
NSA_CMP_STRIDE, NSA_CMP_LEN, NSA_SEL_BLOCK, NSA_TOP_N, NSA_WINDOW = 16, 32, 64, 16, 512
FOX_HEADS, FOX_HEAD_DIM = 8, 128
MLA_HEADS, MLA_Q_RANK, MLA_KV_RANK, MLA_NOPE_DIM, MLA_ROPE_DIM, MLA_V_DIM = 8, 768, 512, 128, 64, 128
ROPE_THETA = 10000.0
SWA_HEADS, SWA_KV_HEADS, SWA_HEAD_DIM, SWA_WINDOW = 16, 2, 64, 128
N_GROUPS, EXPERTS_PER_GROUP, TOP_K, D_EXPERT = 8, 8, 2, 384
N_EXPERTS = N_GROUPS * EXPERTS_PER_GROUP
IN_SPLITS = (1024, 1536, 24, 3072, 8, 768, 512, 64, 1024, 256)

MAIN_AQ, MAIN_AKV, MAIN_BQKV, MAIN_DQ, MAIN_DKV, MAIN_COLS = 0, 1024, 2560, 5632, 6656, 7168
SMALL_CQ, SMALL_CKV, SMALL_MISC, SMALL_COLS = 0, 768, 1280, 1536
MISC_GATE_LANE, MISC_FORGET_LANE = 64, 88

MOE_ROWS = 512
MOE_SUB_ROWS = 128
DE_CHUNK = 128
DMA_UNROLL = 8


def _dot(a, b):
    return jnp.dot(a, b, preferred_element_type=F32)


def _dot_nt(a, b):
    return lax.dot_general(a, b, (((1,), (1,)), ((), ())), preferred_element_type=F32)


def _alibi_slopes(n_heads):
    return jnp.exp2(-8.0 * jnp.arange(1, n_heads + 1, dtype=F32) / n_heads)


def _split3(x):
    hi = x.astype(BF16)
    r1 = x - hi.astype(F32)
    mid = r1.astype(BF16)
    lo = (r1 - mid.astype(F32)).astype(BF16)
    return hi, mid, lo


def _params(sem, vmem_mb=None):
    kw = dict(dimension_semantics=sem)
    if vmem_mb is not None:
        kw["vmem_limit_bytes"] = vmem_mb * 1024 * 1024
    return pltpu.CompilerParams(**kw)


def _norm_matmul_kernel(x_ref, g_ref, w_ref, o_ref, xn_ref, *, col_off, k):
    @pl.when(pl.program_id(1) == 0)
    def _():
        x = x_ref[:, col_off:col_off + k].astype(F32)
        ms = jnp.mean(x * x, axis=-1, keepdims=True)
        xn_ref[...] = (x * lax.rsqrt(ms + NORM_EPS) * g_ref[...]).astype(BF16)

    o_ref[...] = _dot(xn_ref[...], w_ref[...]).astype(o_ref.dtype)


def _norm_matmul(x, gain, w, *, out_dtype, col_off=0, k=None, w_col0=0, n=None, tm=512, tn=512,
                 name="norm_matmul"):
    m, kfull = x.shape
    k = kfull if k is None else k
    n = w.shape[1] if n is None else n
    tm = min(tm, m)
    assert m % tm == 0 and n % tn == 0 and w_col0 % tn == 0 and w.shape[0] == k
    wb0 = w_col0 // tn
    return pl.pallas_call(
        functools.partial(_norm_matmul_kernel, col_off=col_off, k=k),
        grid=(m // tm, n // tn),
        in_specs=[
            pl.BlockSpec((tm, kfull), lambda i, j: (i, 0)),
            pl.BlockSpec((1, k), lambda i, j: (0, 0)),
            pl.BlockSpec((k, tn), lambda i, j: (0, wb0 + j)),
        ],
        out_specs=pl.BlockSpec((tm, tn), lambda i, j: (i, j)),
        out_shape=jax.ShapeDtypeStruct((m, n), out_dtype),
        scratch_shapes=[pltpu.VMEM((tm, k), BF16)],
        compiler_params=_params(("parallel", "arbitrary"), 56),
        name=name,
    )(x, gain.reshape(1, k).astype(F32), w)


def _out_proj_kernel(a_ref, b_ref, c_ref, d_ref, g_ref, w_ref, r_ref, o_ref, xn_ref):
    @pl.when(pl.program_id(1) == 0)
    def _():
        for i, ref in enumerate((a_ref, b_ref, c_ref, d_ref)):
            x = ref[...]
            ms = jnp.mean(x * x, axis=-1, keepdims=True)
            g = g_ref[:, i * GROUP_WIDTH:(i + 1) * GROUP_WIDTH]
            xn_ref[:, i * GROUP_WIDTH:(i + 1) * GROUP_WIDTH] = (x * lax.rsqrt(ms + NORM_EPS) * g).astype(BF16)

    o_ref[...] = r_ref[...] + _dot(xn_ref[...], w_ref[...])


def _out_proj(outs, gain, w, resid, *, tm=512, tn=1024):
    m, d = resid.shape
    tm = min(tm, m)
    k = 4 * GROUP_WIDTH
    grp = pl.BlockSpec((tm, GROUP_WIDTH), lambda i, j: (i, 0))
    return pl.pallas_call(
        _out_proj_kernel,
        grid=(m // tm, d // tn),
        in_specs=[grp, grp, grp, grp,
                  pl.BlockSpec((1, k), lambda i, j: (0, 0)),
                  pl.BlockSpec((k, tn), lambda i, j: (0, j)),
                  pl.BlockSpec((tm, tn), lambda i, j: (i, j))],
        out_specs=pl.BlockSpec((tm, tn), lambda i, j: (i, j)),
        out_shape=jax.ShapeDtypeStruct((m, d), F32),
        scratch_shapes=[pltpu.VMEM((tm, k), BF16)],
        compiler_params=_params(("parallel", "arbitrary"), 56),
        name="out_proj",
    )(*outs, gain.reshape(1, k).astype(F32), w, resid)


def _fox_prep_kernel(x_ref, b_ref, cumt_ref, carry_ref, *, ts):
    @pl.when(pl.program_id(1) == 0)
    def _():
        carry_ref[...] = jnp.zeros_like(carry_ref)

    z = x_ref[...] + b_ref[...]
    lf = jnp.minimum(z, 0.0) - jnp.log1p(jnp.exp(-jnp.abs(z)))
    row = lax.broadcasted_iota(I32, (ts, ts), 0)
    col = lax.broadcasted_iota(I32, (ts, ts), 1)
    tri = jnp.where(col <= row, 1.0, 0.0).astype(BF16)
    hi, mid, lo = _split3(lf)
    cum = (_dot(tri, hi) + _dot(tri, mid)) + _dot(tri, lo) + carry_ref[...]
    carry_ref[...] = cum[ts - 1:ts, :]
    cum_t = jnp.concatenate([cum[i * LANES:(i + 1) * LANES, :].T for i in range(ts // LANES)], axis=1)
    cumt_ref[0] = cum_t[MISC_FORGET_LANE:MISC_FORGET_LANE + FOX_HEADS, :]


def _fox_prep(small, f_bias, bsz, seq, *, ts=512):
    ts = min(ts, seq)
    n = bsz * seq
    ns = seq // ts
    bias = jnp.zeros((1, LANES), F32).at[0, MISC_FORGET_LANE:MISC_FORGET_LANE + FOX_HEADS].set(f_bias.astype(F32))
    misc_blk = SMALL_MISC // LANES
    return pl.pallas_call(
        functools.partial(_fox_prep_kernel, ts=ts),
        grid=(bsz, ns),
        in_specs=[pl.BlockSpec((ts, LANES), lambda b, i: (b * ns + i, misc_blk)),
                  pl.BlockSpec((1, LANES), lambda b, i: (0, 0))],
        out_specs=pl.BlockSpec((1, FOX_HEADS, ts), lambda b, i: (b, 0, i)),
        out_shape=jax.ShapeDtypeStruct((bsz, FOX_HEADS, seq), F32),
        scratch_shapes=[pltpu.VMEM((1, LANES), F32)],
        compiler_params=_params(("parallel", "arbitrary")),
        name="fox_prep",
    )(small, bias)


def _with_ones(v):
    return jnp.concatenate([v, jnp.ones((v.shape[0], LANES), v.dtype)], axis=1)


def _flash_rows(s2, v1, m_sc, l_sc, acc_sc, r0, rows):
    sl = slice(r0, r0 + rows)
    dv = v1.shape[1] - LANES
    m_prev = m_sc[sl]
    m_new = jnp.maximum(m_prev, jnp.max(s2, axis=1, keepdims=True))
    alpha = jnp.exp2(m_prev - m_new)
    p = jnp.concatenate([jnp.exp2(s2[:, j * LANES:(j + 1) * LANES] - m_new)
                         for j in range(s2.shape[1] // LANES)], axis=1)
    pv = _dot(p.astype(BF16), v1)
    l_sc[sl] = alpha * l_sc[sl] + pv[:, dv:]
    acc_sc[sl] = alpha * acc_sc[sl] + pv[:, :dv]
    m_sc[sl] = m_new


def _flash_init(m_sc, l_sc, acc_sc):
    m_sc[...] = jnp.full(m_sc.shape, NEG_INF, F32)
    l_sc[...] = jnp.zeros(l_sc.shape, F32)
    acc_sc[...] = jnp.zeros(acc_sc.shape, F32)


def _pipelined_sweep(n_full, issue, softmax_pv, last_tile, sa_sc, sb_sc):
    def full_tile(ki, src, dst):
        issue(ki + 1, dst)
        softmax_pv(ki, src)

    issue(0, sa_sc)

    def pair(j, carry):
        full_tile(2 * j, sa_sc, sb_sc)
        full_tile(2 * j + 1, sb_sc, sa_sc)
        return carry

    lax.fori_loop(0, n_full // 2, pair, 0)

    @pl.when(n_full % 2 == 1)
    def _():
        full_tile(n_full - 1, sa_sc, sb_sc)
        last_tile(sb_sc)

    @pl.when(n_full % 2 == 0)
    def _():
        last_tile(sa_sc)


def _causal_sweep(qi, tq, score_fn, load_v, sa_sc, sb_sc, m_sc, l_sc, acc_sc):
    nch = tq // ROW_CHUNK
    chunk = lambda c: slice(c * ROW_CHUNK, (c + 1) * ROW_CHUNK)
    score_rows = min(SCORE_ROWS, tq)

    def issue(ki, dst):
        for c in range(tq // score_rows):
            rows = slice(c * score_rows, (c + 1) * score_rows)
            dst[rows, :] = score_fn(rows, ki)

    def softmax_pv(ki, src):
        v = _with_ones(load_v(ki, tq))
        for c in range(nch):
            _flash_rows(src[chunk(c), :], v, m_sc, l_sc, acc_sc, c * ROW_CHUNK, ROW_CHUNK)

    def diag_tile(src):
        for c in range(nch):
            ncols = (c + 1) * ROW_CHUNK
            row = lax.broadcasted_iota(I32, (ROW_CHUNK, ncols), 0) + c * ROW_CHUNK
            col = lax.broadcasted_iota(I32, (ROW_CHUNK, ncols), 1)
            s2 = jnp.where(col <= row, src[chunk(c), 0:ncols], NEG_INF)
            _flash_rows(s2, _with_ones(load_v(qi, ncols)), m_sc, l_sc, acc_sc, c * ROW_CHUNK, ROW_CHUNK)

    _pipelined_sweep(qi, issue, softmax_pv, diag_tile, sa_sc, sb_sc)


def _fox_attn_kernel(q_ref, k_ref, v_ref, ck_ref, o_ref, sa_sc, sb_sc, m_sc, l_sc, acc_sc, *, tq, scale):
    qi = pl.program_id(2)
    _flash_init(m_sc, l_sc, acc_sc)
    c0 = ck_ref[0, pl.ds(qi, 1), :][:, 0:1]

    def score(rows, ki):
        k0 = pl.multiple_of(ki * tq, tq)
        k = k_ref[pl.ds(k0, tq), :]
        ck2 = (ck_ref[0, pl.ds(ki, 1), :] - c0) * LOG2E
        return _dot_nt(q_ref[rows, :], k) * (scale * LOG2E) - ck2

    def load_v(ki, ncols):
        return v_ref[pl.ds(pl.multiple_of(ki * tq, tq), ncols), :]

    _causal_sweep(qi, tq, score, load_v, sa_sc, sb_sc, m_sc, l_sc, acc_sc)
    o_ref[...] = acc_sc[...] / l_sc[...]


def _fox_attn(main, cumt, bsz, seq, *, tq=512):
    tq = min(tq, seq)
    nq = seq // tq
    n = bsz * seq
    d = FOX_HEAD_DIM
    qb, kb, vb = MAIN_BQKV // d, MAIN_BQKV // d + FOX_HEADS, MAIN_BQKV // d + 2 * FOX_HEADS
    cumt3 = cumt.reshape(bsz * FOX_HEADS, nq, tq)
    return pl.pallas_call(
        functools.partial(_fox_attn_kernel, tq=tq, scale=d ** -0.5),
        grid=(bsz, FOX_HEADS, nq),
        in_specs=[pl.BlockSpec((tq, d), lambda b, h, i: (b * nq + i, qb + h)),
                  pl.BlockSpec((seq, d), lambda b, h, i: (b, kb + h)),
                  pl.BlockSpec((seq, d), lambda b, h, i: (b, vb + h)),
                  pl.BlockSpec((1, nq, tq), lambda b, h, i: (b * FOX_HEADS + h, 0, 0))],
        out_specs=pl.BlockSpec((tq, d), lambda b, h, i: (b * nq + i, h)),
        out_shape=jax.ShapeDtypeStruct((n, FOX_HEADS * d), F32),
        scratch_shapes=[pltpu.VMEM((tq, tq), F32), pltpu.VMEM((tq, tq), F32),
                        pltpu.VMEM((tq, LANES), F32), pltpu.VMEM((tq, LANES), F32), pltpu.VMEM((tq, d), F32)],
        compiler_params=_params(("parallel", "parallel", "arbitrary")),
        name="fox_attn",
    )(main, main, main, cumt3)


def _mla_attn_kernel(qn_ref, qr_ref, kn_ref, kr_ref, v_ref, o_ref, q_sc, sa_sc, sb_sc, m_sc, l_sc, acc_sc,
                     *, tq, scale):
    qi = pl.program_id(2)
    _flash_init(m_sc, l_sc, acc_sc)
    q_sc[:, :LANES] = qn_ref[...].astype(BF16)
    q_sc[:, LANES:] = qr_ref[...]

    def score(rows, ki):
        k0 = pl.multiple_of(ki * tq, tq)
        k = jnp.concatenate([kn_ref[pl.ds(k0, tq), :], kr_ref[pl.ds(k0, tq), :]], axis=1)
        return _dot_nt(q_sc[rows, :], k) * (scale * LOG2E)

    def load_v(ki, ncols):
        return v_ref[pl.ds(pl.multiple_of(ki * tq, tq), ncols), :]

    _causal_sweep(qi, tq, score, load_v, sa_sc, sb_sc, m_sc, l_sc, acc_sc)
    o_ref[...] = acc_sc[...] / l_sc[...]


def _mla_attn(qup, qrope, kvup, krope, bsz, seq, *, tq=512):
    tq = min(tq, seq)
    nq = seq // tq
    n = bsz * seq
    d = LANES
    return pl.pallas_call(
        functools.partial(_mla_attn_kernel, tq=tq, scale=(MLA_NOPE_DIM + MLA_ROPE_DIM) ** -0.5),
        grid=(bsz, MLA_HEADS, nq),
        in_specs=[pl.BlockSpec((tq, d), lambda b, h, i: (b * nq + i, h)),
                  pl.BlockSpec((tq, d), lambda b, h, i: (b * nq + i, h)),
                  pl.BlockSpec((seq, d), lambda b, h, i: (b, 2 * h)),
                  pl.BlockSpec((seq, d), lambda b, h, i: (b, 0)),
                  pl.BlockSpec((seq, d), lambda b, h, i: (b, 2 * h + 1))],
        out_specs=pl.BlockSpec((tq, d), lambda b, h, i: (b * nq + i, h)),
        out_shape=jax.ShapeDtypeStruct((n, MLA_HEADS * MLA_V_DIM), F32),
        scratch_shapes=[pltpu.VMEM((tq, 2 * LANES), BF16),
                        pltpu.VMEM((tq, tq), F32), pltpu.VMEM((tq, tq), F32),
                        pltpu.VMEM((tq, LANES), F32), pltpu.VMEM((tq, LANES), F32), pltpu.VMEM((tq, d), F32)],
        compiler_params=_params(("parallel", "parallel", "arbitrary")),
        name="mla_attn",
    )(qup, qrope, kvup, krope, kvup)


def _rope_kernel(q_ref, k_ref, cos_ref, sin_ref, qo_ref, ko_ref):
    half = MLA_ROPE_DIM // 2

    def rope(x):
        lane = lax.broadcasted_iota(I32, x.shape, 1)
        first = (lane % MLA_ROPE_DIM) < half
        swapped = jnp.where(first, pltpu.roll(x, LANES - half, 1), pltpu.roll(x, half, 1))
        return x * cos_ref[...] + swapped * sin_ref[...]

    for t in range(q_ref.shape[1] // LANES):
        qo_ref[:, t * LANES:(t + 1) * LANES] = rope(q_ref[:, t * LANES:(t + 1) * LANES]).astype(BF16)
    k = rope(k_ref[...])
    lane = lax.broadcasted_iota(I32, k.shape, 1)
    ko_ref[...] = jnp.where(lane < MLA_ROPE_DIM, k, 0.0).astype(BF16)


def _rope(qup, small, bsz, seq, *, ts=512):
    ts = min(ts, seq)
    ns = seq // ts
    n = bsz * seq
    pos = jnp.arange(seq, dtype=F32)
    inv = ROPE_THETA ** (-jnp.arange(0, MLA_ROPE_DIM, 2, dtype=F32) / MLA_ROPE_DIM)
    ang = pos[:, None] * inv[None, :]
    cos, sin = jnp.cos(ang), jnp.sin(ang)
    cos_t = jnp.concatenate([cos, cos, cos, cos], axis=1)
    sin_t = jnp.concatenate([-sin, sin, -sin, sin], axis=1)
    w = MLA_HEADS * LANES
    return pl.pallas_call(
        _rope_kernel,
        grid=(bsz, ns),
        in_specs=[pl.BlockSpec((ts, w), lambda b, i: (b * ns + i, 1)),
                  pl.BlockSpec((ts, LANES), lambda b, i: (b * ns + i, SMALL_MISC // LANES)),
                  pl.BlockSpec((ts, LANES), lambda b, i: (i, 0)),
                  pl.BlockSpec((ts, LANES), lambda b, i: (i, 0))],
        out_specs=[pl.BlockSpec((ts, w), lambda b, i: (b * ns + i, 0)),
                   pl.BlockSpec((ts, LANES), lambda b, i: (b * ns + i, 0))],
        out_shape=[jax.ShapeDtypeStruct((n, w), BF16), jax.ShapeDtypeStruct((n, LANES), BF16)],
        compiler_params=_params(("parallel", "parallel")),
        name="mla_rope",
    )(qup, small, cos_t, sin_t)


def _banded_kernel(*refs, tq, nr, hd, window, slab, scale, has_sinks, seq):
    if has_sinks:
        slopes_ref, sinks_ref, q_ref, k_ref, v_ref, o_ref = refs
    else:
        slopes_ref, q_ref, k_ref, v_ref, o_ref = refs
    g = pl.program_id(1)
    qi = pl.program_id(2)
    q0 = qi * tq
    start = pl.multiple_of(jnp.minimum(jnp.maximum(q0 + tq - slab, 0), seq - slab), tq)
    kt = k_ref[pl.ds(start, slab), :]
    v1 = _with_ones(v_ref[pl.ds(start, slab), :])
    q = q_ref[...]
    low = lax.broadcasted_iota(I32, (tq, LANES), 1) < hd
    if hd == LANES:
        qs = jnp.concatenate([q[:, r * hd:(r + 1) * hd] for r in range(nr)], axis=0)
    else:
        zero = jnp.zeros((), q.dtype)
        parts = []
        for j in range(nr // 2):
            pair = q[:, j * LANES:(j + 1) * LANES]
            parts += [jnp.where(low, pair, zero), jnp.where(low, zero, pair)]
        qs = jnp.concatenate(parts, axis=0)
    s = _dot_nt(qs, kt) * (scale * LOG2E)
    qpos = q0 + lax.broadcasted_iota(I32, (tq, slab), 0)
    kpos = start + lax.broadcasted_iota(I32, (tq, slab), 1)
    dist = qpos - kpos
    valid = (dist >= 0) & (dist < window)
    krel = (start - q0 + lax.broadcasted_iota(I32, (1, slab), 1)).astype(F32)
    qrel = lax.broadcasted_iota(I32, (tq, 1), 0).astype(F32)
    es, sinks_e = [], []
    for r in range(nr):
        slope2 = slopes_ref[g * nr + r] * LOG2E
        sr = jnp.where(valid, s[r * tq:(r + 1) * tq] + slope2 * krel, NEG_INF)
        m = jnp.max(sr, axis=1, keepdims=True)
        if has_sinks:
            sk = sinks_ref[g * nr + r] * LOG2E + slope2 * qrel
            m = jnp.maximum(m, sk)
            sinks_e.append(jnp.exp2(sk - m))
        es.append(jnp.exp2(sr - m).astype(BF16))
    pv = _dot(jnp.concatenate(es, axis=0), v1)
    den = pv[:, LANES:]
    if has_sinks:
        den = den + jnp.concatenate(sinks_e, axis=0)
    o = pv[:, :LANES] / den
    if hd == LANES:
        o_ref[...] = jnp.concatenate([o[r * tq:(r + 1) * tq] for r in range(nr)], axis=1)
    else:
        o_ref[...] = jnp.concatenate(
            [jnp.where(low, o[2 * j * tq:(2 * j + 1) * tq], o[(2 * j + 1) * tq:(2 * j + 2) * tq])
             for j in range(nr // 2)], axis=1)


def _banded_attn(main, slopes, sinks, bsz, seq, *, ng, nr, hd, window, q_col, k_col, v_col, name, tq=128):
    tq = min(tq, seq)
    nq = seq // tq
    n = bsz * seq
    slab = min(-(-(window - 1) // tq) * tq + tq, seq)
    qw = nr * hd
    has_sinks = sinks is not None
    smem = pl.BlockSpec(memory_space=pltpu.SMEM)
    in_specs = [smem] + ([smem] if has_sinks else []) + [
        pl.BlockSpec((tq, qw), lambda b, g, i: (b * nq + i, q_col // qw + g)),
        pl.BlockSpec((seq, LANES), lambda b, g, i: (b, k_col // LANES + g)),
        pl.BlockSpec((seq, LANES), lambda b, g, i: (b, v_col // LANES + g))]
    args = [slopes] + ([sinks.astype(F32)] if has_sinks else []) + [main, main, main]
    return pl.pallas_call(
        functools.partial(_banded_kernel, tq=tq, nr=nr, hd=hd, window=window, slab=slab,
                          scale=hd ** -0.5, has_sinks=has_sinks, seq=seq),
        grid=(bsz, ng, nq),
        in_specs=in_specs,
        out_specs=pl.BlockSpec((tq, qw), lambda b, g, i: (b * nq + i, g)),
        out_shape=jax.ShapeDtypeStruct((n, ng * qw), F32),
        compiler_params=_params(("parallel", "parallel", "parallel")),
        name=name,
    )(*args)


def _compress_kernel(x_ref, pos_ref, w1_ref, w2_ref, o_ref, xf_ref, *, seq):
    nc = seq // NSA_CMP_STRIDE
    hd = NSA_HEAD_DIM
    xf_ref[0:seq, :] = x_ref[...].astype(F32)
    xf_ref[seq:seq + NSA_CMP_STRIDE, :] = jnp.zeros((NSA_CMP_STRIDE, hd), F32)
    acc = jnp.zeros((nc, hd), F32)
    for j in range(NSA_CMP_LEN):
        rows = xf_ref[pl.ds(j, nc, stride=NSA_CMP_STRIDE), :] + pos_ref[0, j:j + 1, :]
        acc = acc + _dot(rows.astype(BF16), w1_ref[0, j * hd:(j + 1) * hd, :])
    hid = jax.nn.gelu(acc)
    o_ref[0, 0, 0] = _dot(hid.astype(BF16), w2_ref[0]).astype(BF16)


def _compress(main, pos, w1, w2, bsz, seq):
    nc = seq // NSA_CMP_STRIDE
    hd = NSA_HEAD_DIM
    ng = NSA_KV_HEADS
    col0 = MAIN_AKV // hd
    return pl.pallas_call(
        functools.partial(_compress_kernel, seq=seq),
        grid=(bsz, ng, 2),
        in_specs=[pl.BlockSpec((seq, hd), lambda b, g, t: (b, col0 + t * ng + g)),
                  pl.BlockSpec((1, NSA_CMP_LEN, hd), lambda b, g, t: (t, 0, 0)),
                  pl.BlockSpec((1, NSA_CMP_LEN * hd, hd), lambda b, g, t: (t, 0, 0)),
                  pl.BlockSpec((1, hd, hd), lambda b, g, t: (t, 0, 0))],
        out_specs=pl.BlockSpec((1, 1, 1, nc, hd), lambda b, g, t: (b, g, t, 0, 0)),
        out_shape=jax.ShapeDtypeStruct((bsz, ng, 2, nc, hd), BF16),
        scratch_shapes=[pltpu.VMEM((seq + NSA_CMP_STRIDE, hd), F32)],
        compiler_params=_params(("parallel", "parallel", "parallel")),
        name="nsa_compress",
    )(main, pos, w1, w2)


def _nsa_cmp_kernel(slopes_ref, q_ref, kc_ref, vc_ref, o_ref, sel_ref, *, tq, nc, nr):
    g = pl.program_id(1)
    qi = pl.program_id(2)
    hd = NSA_HEAD_DIM
    q0 = qi * tq
    q = q_ref[...]
    qs = jnp.concatenate([q[:, r * hd:(r + 1) * hd] for r in range(nr)], axis=0)
    kc = kc_ref[0, 0, 0]
    vc = vc_ref[0, 0, 0]
    s = _dot_nt(qs, kc) * (hd ** -0.5)
    tpos = q0 + lax.broadcasted_iota(I32, (tq, nc), 0)
    cend = lax.broadcasted_iota(I32, (tq, nc), 1) * NSA_CMP_STRIDE + (NSA_CMP_LEN - 1)
    dist = tpos - cend
    valid = dist >= 0
    distf = dist.astype(F32)
    ps = []
    psum = jnp.zeros((tq, nc), F32)
    for r in range(nr):
        sr = s[r * tq:(r + 1) * tq] - slopes_ref[g * nr + r] * distf
        sr = jnp.where(valid, sr, NEG_INF)
        m = jnp.max(sr, axis=1, keepdims=True)
        e = jnp.exp(sr - m)
        p = jnp.where(valid, e / jnp.sum(e, axis=1, keepdims=True), 0.0)
        psum = psum + p
        ps.append(p.astype(BF16))
    o = _dot(jnp.concatenate(ps, axis=0), vc)
    o_ref[...] = jnp.concatenate([o[r * tq:(r + 1) * tq] for r in range(nr)], axis=1)

    nb = LANES
    n_slc = nc * NSA_CMP_STRIDE // NSA_SEL_BLOCK
    per = NSA_SEL_BLOCK // NSA_CMP_STRIDE
    blk = lax.broadcasted_iota(I32, (nb, nc), 0)
    cidx = lax.broadcasted_iota(I32, (nb, nc), 1)
    overlap = (cidx <= per * blk + per - 1) & (cidx >= per * blk - 1) & (cidx < nc - 1) & (blk < n_slc)
    ov = jnp.where(overlap, 1.0, 0.0).astype(BF16)
    hi, mid, lo = _split3(psum)
    imp = (_dot_nt(ov, hi) + _dot_nt(ov, mid)) + _dot_nt(ov, lo)
    j = lax.broadcasted_iota(I32, (nb, tq), 0)
    cur = (q0 + lax.broadcasted_iota(I32, (nb, tq), 1)) // NSA_SEL_BLOCK
    forced = (j == 0) | (j == cur) | (j == cur - 1)
    imp = jnp.where(forced, 1e6, imp)
    imp = jnp.where(j > cur, -1e6, imp)
    imp = jnp.where(j >= n_slc, -3e38, imp)
    rank = jnp.zeros((nb, tq), F32)
    for i in range(n_slc):
        vi = imp[i:i + 1, :]
        ahead = (vi > imp) | ((vi == imp) & (j > i))
        rank = rank + jnp.where(ahead, 1.0, 0.0)
    sel = jnp.where(rank < float(min(NSA_TOP_N, n_slc)), 0.0, NEG_INF)
    sel_ref[...] = sel.T.astype(BF16)


def _nsa_cmp(main, kvc, slopes, bsz, seq, *, tq=128):
    tq = min(tq, seq)
    nq = seq // tq
    n = bsz * seq
    ng, nr, hd = NSA_KV_HEADS, NSA_HEADS // NSA_KV_HEADS, NSA_HEAD_DIM
    nc = seq // NSA_CMP_STRIDE
    qw = nr * hd
    smem = pl.BlockSpec(memory_space=pltpu.SMEM)
    return pl.pallas_call(
        functools.partial(_nsa_cmp_kernel, tq=tq, nc=nc, nr=nr),
        grid=(bsz, ng, nq),
        in_specs=[smem,
                  pl.BlockSpec((tq, qw), lambda b, g, i: (b * nq + i, g)),
                  pl.BlockSpec((1, 1, 1, nc, hd), lambda b, g, i: (b, g, 0, 0, 0)),
                  pl.BlockSpec((1, 1, 1, nc, hd), lambda b, g, i: (b, g, 1, 0, 0))],
        out_specs=[pl.BlockSpec((tq, qw), lambda b, g, i: (b * nq + i, g)),
                   pl.BlockSpec((tq, LANES), lambda b, g, i: ((b * ng + g) * nq + i, 0))],
        out_shape=[jax.ShapeDtypeStruct((n, ng * qw), F32),
                   jax.ShapeDtypeStruct((bsz * ng * seq, LANES), BF16)],
        compiler_params=_params(("parallel", "parallel", "parallel")),
        name="nsa_cmp",
    )(slopes, main, kvc, kvc)


def _nsa_slc_kernel(slopes_ref, q_ref, k_ref, v_ref, sel_ref, o_ref, sa_sc, sb_sc, m_sc, l_sc, acc_sc,
                    *, tq, tk, nr):
    g = pl.program_id(1)
    qi = pl.program_id(2)
    hd = NSA_HEAD_DIM
    q0 = qi * tq
    _flash_init(m_sc, l_sc, acc_sc)
    last = q0 // tk
    c2 = (hd ** -0.5) * LOG2E
    head = lambda r: slice(r * tq, (r + 1) * tq)

    shift = int(math.log2(NSA_SEL_BLOCK))
    lane_minus_blk = (lax.broadcasted_iota(I32, (tk, LANES), 1)
                      - lax.shift_right_logical(lax.broadcasted_iota(I32, (tk, LANES), 0), shift))

    def issue(ki, dst):
        k0 = pl.multiple_of(ki * tk, tk)
        onehot = jnp.where(lane_minus_blk == ki * (tk // NSA_SEL_BLOCK), 1.0, 0.0).astype(BF16)
        ka = jnp.concatenate([k_ref[pl.ds(k0, tk), :], onehot], axis=1)
        rel = (k0 - q0 + lax.broadcasted_iota(I32, (1, tk), 1)).astype(F32)
        for r in range(nr):
            qa = jnp.concatenate([q_ref[:, r * hd:(r + 1) * hd], sel_ref[...]], axis=1)
            dst[head(r), :] = _dot_nt(qa, ka) * c2 + (slopes_ref[g * nr + r] * LOG2E) * rel

    def load_v(ki):
        return _with_ones(v_ref[pl.ds(pl.multiple_of(ki * tk, tk), tk), :])

    rc = min(ROW_CHUNK, tq)
    chains = [(r * tq + c * rc, c * rc) for r in range(nr) for c in range(tq // rc)]

    def softmax_pv(ki, src):
        v = load_v(ki)
        for row0, _ in chains:
            _flash_rows(src[row0:row0 + rc, :], v, m_sc, l_sc, acc_sc, row0, rc)

    def last_tile(src):
        v = load_v(last)
        k0 = last * tk
        ahead = (k0 + lax.broadcasted_iota(I32, (tq, tk), 1)) > (q0 + lax.broadcasted_iota(I32, (tq, tk), 0))
        for row0, qrow in chains:
            s2 = jnp.where(ahead[qrow:qrow + rc], NEG_INF, src[row0:row0 + rc, :])
            _flash_rows(s2, v, m_sc, l_sc, acc_sc, row0, rc)

    _pipelined_sweep(last, issue, softmax_pv, last_tile, sa_sc, sb_sc)
    o = acc_sc[...] / l_sc[...]
    o_ref[...] = jnp.concatenate([o[r * tq:(r + 1) * tq] for r in range(nr)], axis=1)


def _nsa_slc(main, sel, slopes, bsz, seq, *, tq=256, tk=512):
    tq = min(tq, seq)
    tk = min(tk, seq)
    nq = seq // tq
    n = bsz * seq
    ng, nr, hd = NSA_KV_HEADS, NSA_HEADS // NSA_KV_HEADS, NSA_HEAD_DIM
    qw = nr * hd
    kb = MAIN_AKV // hd + 2 * ng
    vb = MAIN_AKV // hd + 3 * ng
    smem = pl.BlockSpec(memory_space=pltpu.SMEM)
    return pl.pallas_call(
        functools.partial(_nsa_slc_kernel, tq=tq, tk=tk, nr=nr),
        grid=(bsz, ng, nq),
        in_specs=[smem,
                  pl.BlockSpec((tq, qw), lambda b, g, i: (b * nq + i, g)),
                  pl.BlockSpec((seq, hd), lambda b, g, i: (b, kb + g)),
                  pl.BlockSpec((seq, hd), lambda b, g, i: (b, vb + g)),
                  pl.BlockSpec((tq, LANES), lambda b, g, i: ((b * ng + g) * nq + i, 0))],
        out_specs=pl.BlockSpec((tq, qw), lambda b, g, i: (b * nq + i, g)),
        out_shape=jax.ShapeDtypeStruct((n, ng * qw), F32),
        scratch_shapes=[pltpu.VMEM((nr * tq, tk), F32), pltpu.VMEM((nr * tq, tk), F32),
                        pltpu.VMEM((nr * tq, LANES), F32), pltpu.VMEM((nr * tq, LANES), F32),
                        pltpu.VMEM((nr * tq, hd), F32)],
        compiler_params=_params(("parallel", "parallel", "arbitrary")),
        name="nsa_slc",
    )(slopes, main, main, main, sel)


def _nsa_gate_kernel(c_ref, s_ref, w_ref, g_ref, o_ref):
    gates = jax.nn.sigmoid(g_ref[...])
    hd = NSA_HEAD_DIM
    for h in range(NSA_HEADS):
        sl = slice(h * hd, (h + 1) * hd)
        lane = MISC_GATE_LANE + h
        o_ref[:, sl] = (gates[:, lane:lane + 1] * c_ref[:, sl]
                        + gates[:, lane + NSA_HEADS:lane + NSA_HEADS + 1] * s_ref[:, sl]
                        + gates[:, lane + 2 * NSA_HEADS:lane + 2 * NSA_HEADS + 1] * w_ref[:, sl])


def _nsa_gate(o_cmp, o_slc, o_win, small, *, tm=512):
    n, w = o_cmp.shape
    tm = min(tm, n)
    row = pl.BlockSpec((tm, w), lambda i: (i, 0))
    return pl.pallas_call(
        _nsa_gate_kernel,
        grid=(n // tm,),
        in_specs=[row, row, row, pl.BlockSpec((tm, LANES), lambda i: (i, SMALL_MISC // LANES))],
        out_specs=row,
        out_shape=jax.ShapeDtypeStruct((n, w), F32),
        compiler_params=_params(("parallel",)),
        name="nsa_gate",
    )(o_cmp, o_slc, o_win, small)


def _router_kernel(x_ref, g_ref, wh_ref, wm_ref, br_ref, xn_ref, route_ref, gate_ref, *, tm):
    x = x_ref[...]
    ms = jnp.mean(x * x, axis=-1, keepdims=True)
    xn = x * lax.rsqrt(ms + NORM_EPS) * g_ref[...]
    xn_ref[...] = xn
    xh = xn.astype(BF16)
    xm = (xn - xh.astype(F32)).astype(BF16)
    wh = wh_ref[...]
    lg_rows = _dot(xh, wh) + (_dot(xm, wh) + _dot(xh, wm_ref[...]))
    logits = jnp.concatenate([lg_rows[i * LANES:(i + 1) * LANES, :].T for i in range(tm // LANES)],
                             axis=1) + br_ref[:, 0:1]
    ng, ne = N_GROUPS, EXPERTS_PER_GROUP
    lg = logits[0:ng, :]
    sub = lax.broadcasted_iota(I32, (ng, tm), 0)
    mg = jnp.max(lg, axis=0, keepdims=True)
    eg = jnp.exp(lg - mg)
    pg = eg / jnp.sum(eg, axis=0, keepdims=True)
    pg_top = jnp.max(pg, axis=0, keepdims=True)
    g_idx = jnp.min(jnp.where(pg == pg_top, sub, ng), axis=0, keepdims=True)
    le = jnp.zeros((ne, tm), F32)
    for gi in range(ng):
        le = jnp.where(g_idx == gi, logits[ng + gi * ne:ng + (gi + 1) * ne, :], le)
    v1 = jnp.max(le, axis=0, keepdims=True)
    i1 = jnp.min(jnp.where(le == v1, sub, ne), axis=0, keepdims=True)
    rest = jnp.where(sub == i1, -jnp.inf, le)
    v2 = jnp.max(rest, axis=0, keepdims=True)
    i2 = jnp.min(jnp.where(rest == v2, sub, ne), axis=0, keepdims=True)
    e2 = jnp.exp(v2 - v1)
    den = 1.0 + e2
    w1 = pg_top * (1.0 / den)
    w2 = pg_top * (e2 / den)
    ex1 = (g_idx * ne + i1).astype(F32)
    ex2 = (g_idx * ne + i2).astype(F32)
    zero = jnp.zeros((1, tm), F32)
    route = jnp.concatenate([ex1, ex2, w1, w2, zero, zero, zero, zero], axis=0)
    route_ref[...] = route
    pad = jnp.concatenate([route, jnp.zeros((LANES - 8, tm), F32)], axis=0)
    gate_ref[...] = jnp.concatenate([pad[:, i * LANES:(i + 1) * LANES].T for i in range(tm // LANES)], axis=0)


def _router(x, gain, rg_w, rg_b, re_w, re_b, *, tm=256):
    n, d = x.shape
    tm = min(tm, n)
    nl = N_GROUPS + N_EXPERTS
    wr = jnp.pad(jnp.concatenate([rg_w, re_w], axis=1).astype(F32), ((0, 0), (0, LANES - nl)))
    br = jnp.zeros((LANES, LANES), F32).at[:nl, :].set(
        jnp.concatenate([rg_b, re_b]).astype(F32)[:, None] * jnp.ones((1, LANES), F32))
    wr_hi = wr.astype(BF16)
    wr_mid = (wr - wr_hi.astype(F32)).astype(BF16)
    return pl.pallas_call(
        functools.partial(_router_kernel, tm=tm),
        grid=(n // tm,),
        in_specs=[pl.BlockSpec((tm, d), lambda i: (i, 0)),
                  pl.BlockSpec((1, d), lambda i: (0, 0)),
                  pl.BlockSpec((d, LANES), lambda i: (0, 0)),
                  pl.BlockSpec((d, LANES), lambda i: (0, 0)),
                  pl.BlockSpec((LANES, LANES), lambda i: (0, 0))],
        out_specs=[pl.BlockSpec((tm, d), lambda i: (i, 0)),
                   pl.BlockSpec((8, tm), lambda i: (0, i)),
                   pl.BlockSpec((tm, LANES), lambda i: (i, 0))],
        out_shape=[jax.ShapeDtypeStruct((n, d), F32),
                   jax.ShapeDtypeStruct((8, n), F32),
                   jax.ShapeDtypeStruct((n, LANES), F32)],
        compiler_params=_params(("parallel",), 48),
        name="moe_router",
    )(x, gain.reshape(1, d).astype(F32), wr_hi, wr_mid, br)


def _slot_kernel(route_ref, pos_ref, cnt_ref, start_ref, carry_ref, *, tm):
    phase = pl.program_id(0)
    i = pl.program_id(1)
    e1 = route_ref[0:1, :].astype(I32)
    e2 = route_ref[1:2, :].astype(I32)
    sub = lax.broadcasted_iota(I32, (N_EXPERTS, tm), 0)
    oh1 = jnp.where(sub == e1, 1.0, 0.0)
    oh2 = jnp.where(sub == e2, 1.0, 0.0)
    ohs = oh1 + oh2

    @pl.when((phase == 0) & (i == 0))
    def _():
        carry_ref[...] = jnp.zeros_like(carry_ref)

    @pl.when(phase == 0)
    def _():
        carry_ref[...] = carry_ref[...] + jnp.sum(ohs, axis=1, keepdims=True)
        pos_ref[...] = jnp.zeros(pos_ref.shape, I32)

    @pl.when((phase == 1) & (i == 0))
    def _():
        cnt = carry_ref[...]
        cnt_ref[...] = cnt
        padded = jnp.floor((cnt + (MOE_ROWS - 1)) / MOE_ROWS) * MOE_ROWS
        row = lax.broadcasted_iota(I32, (N_EXPERTS, N_EXPERTS), 0)
        col = lax.broadcasted_iota(I32, (N_EXPERTS, N_EXPERTS), 1)
        lower = jnp.where(col < row, 1.0, 0.0).astype(BF16)
        hi, mid, lo = _split3(padded)
        start_ref[...] = (_dot(lower, hi) + _dot(lower, mid)) + _dot(lower, lo)
        carry_ref[...] = jnp.zeros_like(carry_ref)

    @pl.when(phase == 1)
    def _():
        row = lax.broadcasted_iota(I32, (tm, tm), 0)
        col = lax.broadcasted_iota(I32, (tm, tm), 1)
        upper = jnp.where(row < col, 1.0, 0.0).astype(BF16)
        before = _dot(ohs.astype(BF16), upper) + (carry_ref[:, 0:1] + start_ref[:, 0:1])
        p1 = jnp.sum(oh1 * before, axis=0, keepdims=True)
        p2 = jnp.sum(oh2 * before, axis=0, keepdims=True)
        zero = jnp.zeros((1, tm), F32)
        pos_ref[...] = jnp.concatenate([p1, p2, zero, zero, zero, zero, zero, zero], axis=0).astype(I32)
        carry_ref[...] = carry_ref[...] + jnp.sum(ohs, axis=1, keepdims=True)


def _slots(route, *, tm=512):
    n = route.shape[1]
    tm = min(tm, n)
    const = pl.BlockSpec((N_EXPERTS, LANES), lambda p, i: (0, 0))
    return pl.pallas_call(
        functools.partial(_slot_kernel, tm=tm),
        grid=(2, n // tm),
        in_specs=[pl.BlockSpec((8, tm), lambda p, i: (0, i))],
        out_specs=[pl.BlockSpec((8, tm), lambda p, i: (0, i * p)), const, const],
        out_shape=[jax.ShapeDtypeStruct((8, n), I32), jax.ShapeDtypeStruct((N_EXPERTS, LANES), F32),
                   jax.ShapeDtypeStruct((N_EXPERTS, LANES), F32)],
        scratch_shapes=[pltpu.VMEM((N_EXPERTS, LANES), F32)],
        compiler_params=_params(("arbitrary", "arbitrary")),
        name="moe_slots",
    )(route)


def _experts_kernel(be_ref, br_ref, dst_ref, dstn_ref, x_hbm, wg_ref, wu_ref, wd_ref, y_hbm,
                    xbuf, xb16, wgu16, wd16, acc, sem_in, sem_out, *, nblk, n_tok):
    i = pl.program_id(0)
    c = pl.program_id(1)
    nch = pl.num_programs(1)
    slot = i % 2
    rows = br_ref[i]

    def in_copy(tok, r, s):
        return pltpu.make_async_copy(x_hbm.at[pl.ds(tok, 1), :], xbuf.at[s, pl.ds(r, 1), :], sem_in.at[s])

    def out_copy(dst, r, s):
        return pltpu.make_async_copy(acc.at[s, pl.ds(r, 1), :], y_hbm.at[pl.ds(dst, 1), :], sem_out.at[s])

    def grouped(n_rows, fn, exact):
        def group(gi, carry):
            for u in range(DMA_UNROLL):
                fn(gi * DMA_UNROLL + u)
            return carry

        def single(r, carry):
            fn(r)
            return carry

        if exact:
            full = n_rows // DMA_UNROLL
            lax.fori_loop(0, full, group, 0)
            lax.fori_loop(full * DMA_UNROLL, n_rows, single, 0)
        else:
            lax.fori_loop(0, (n_rows + DMA_UNROLL - 1) // DMA_UNROLL, group, 0)

    def start_gather(tbl, n_rows, s):
        def fn(r):
            v = tbl[0, 0, r]
            in_copy(jnp.where(v >= n_tok, v - n_tok, v), r, s).start()
        grouped(n_rows, fn, False)

    def wait_gather(n_rows, s):
        grouped(n_rows, lambda r: in_copy(0, r, s).wait(), False)

    def start_scatter(n_rows, s):
        grouped(n_rows, lambda r: out_copy(dst_ref[0, 0, r], r, s).start(), True)

    def wait_scatter(n_rows, s):
        grouped(n_rows, lambda r: out_copy(0, r, s).wait(), True)

    @pl.when((i == 0) & (c == 0))
    def _():
        xbuf[...] = jnp.zeros(xbuf.shape, F32)
        start_gather(dst_ref, rows, 0)

    @pl.when(c == 0)
    def _():
        wait_gather(rows, slot)
        xb16[...] = xbuf[slot].astype(BF16)
        acc[slot] = jnp.zeros(acc.shape[1:], F32)

    @pl.when((c == 1) & (i + 1 < nblk))
    def _():
        start_gather(dstn_ref, br_ref[jnp.minimum(i + 1, nblk - 1)], 1 - slot)

    @pl.when(rows > 0)
    def _():
        wgu16[:, :DE_CHUNK] = wg_ref[0, 0].astype(BF16)
        wgu16[:, DE_CHUNK:] = wu_ref[0, 0].astype(BF16)
        wd16[...] = wd_ref[0, 0].astype(BF16)

    def sub_blocks(n_sub):
        rss = [slice(sb * MOE_SUB_ROWS, (sb + 1) * MOE_SUB_ROWS) for sb in range(n_sub)]
        hgus = [_dot(xb16[rs, :], wgu16[...]) for rs in rss]
        hs = [(jax.nn.silu(hgu[:, :DE_CHUNK]) * hgu[:, DE_CHUNK:]).astype(BF16) for hgu in hgus]
        for rs, h in zip(rss, hs):
            acc[slot, rs, :] += _dot(h, wd16[...])

    n_sub_max = MOE_ROWS // MOE_SUB_ROWS
    for n_sub in range(1, n_sub_max + 1):
        lo = (n_sub - 1) * MOE_SUB_ROWS
        cond = (rows > lo) if n_sub == n_sub_max else ((rows > lo) & (rows <= lo + MOE_SUB_ROWS))
        pl.when(cond)(functools.partial(sub_blocks, n_sub))

    @pl.when(c == nch - 1)
    def _():
        @pl.when(i > 0)
        def _():
            wait_scatter(br_ref[jnp.maximum(i - 1, 0)], 1 - slot)

        start_scatter(rows, slot)

        @pl.when(i == nblk - 1)
        def _():
            wait_scatter(rows, slot)


def _experts(xn, dst_row, blk_e, blk_rows, w_gate, w_up, w_down, layer):
    n, d = xn.shape
    nblk = blk_e.shape[0]
    nch = D_EXPERT // DE_CHUNK
    dst3 = dst_row.reshape(nblk, 1, MOE_ROWS)

    def chunk_of(i, c, br):
        return jnp.where(br[i] > 0, c, nch - 1)

    grid_spec = pltpu.PrefetchScalarGridSpec(
        num_scalar_prefetch=2,
        grid=(nblk, nch),
        in_specs=[
            pl.BlockSpec((1, 1, MOE_ROWS), lambda i, c, be, br: (i, 0, 0), memory_space=pltpu.SMEM),
            pl.BlockSpec((1, 1, MOE_ROWS), lambda i, c, be, br: (jnp.minimum(i + 1, nblk - 1), 0, 0),
                         memory_space=pltpu.SMEM),
            pl.BlockSpec(memory_space=pl.ANY),
            pl.BlockSpec((1, 1, d, DE_CHUNK), lambda i, c, be, br: (layer, be[i], 0, chunk_of(i, c, br))),
            pl.BlockSpec((1, 1, d, DE_CHUNK), lambda i, c, be, br: (layer, be[i], 0, chunk_of(i, c, br))),
            pl.BlockSpec((1, 1, DE_CHUNK, d), lambda i, c, be, br: (layer, be[i], chunk_of(i, c, br), 0)),
        ],
        out_specs=pl.BlockSpec(memory_space=pl.ANY),
        scratch_shapes=[pltpu.VMEM((2, MOE_ROWS, d), F32), pltpu.VMEM((MOE_ROWS, d), BF16),
                        pltpu.VMEM((d, 2 * DE_CHUNK), BF16), pltpu.VMEM((DE_CHUNK, d), BF16),
                        pltpu.VMEM((2, MOE_ROWS, d), F32),
                        pltpu.SemaphoreType.DMA((2,)), pltpu.SemaphoreType.DMA((2,))],
    )
    return pl.pallas_call(
        functools.partial(_experts_kernel, nblk=nblk, n_tok=n),
        grid_spec=grid_spec,
        out_shape=jax.ShapeDtypeStruct((TOP_K * n, d), F32),
        compiler_params=_params(("arbitrary", "arbitrary"), 60),
        name="moe_experts",
    )(blk_e, blk_rows, dst3, dst3, xn, w_gate, w_up, w_down)


def _moe_finish_kernel(x_ref, y0_ref, y1_ref, gate_ref, fg_ref, o_ref, *, final_norm):
    g = gate_ref[...]
    y = x_ref[...] + (g[:, 2:3] * y0_ref[...] + g[:, 3:4] * y1_ref[...])
    if final_norm:
        ms = jnp.mean(y * y, axis=-1, keepdims=True)
        y = y * lax.rsqrt(ms + NORM_EPS) * fg_ref[...]
    o_ref[...] = y


def _moe_finish(y, x, gates, final_gain=None, *, tm=256):
    n, d = x.shape
    tm = min(tm, n)
    nt = n // tm
    final_norm = final_gain is not None
    fg = (final_gain if final_norm else jnp.ones((d,), F32)).reshape(1, d).astype(F32)
    return pl.pallas_call(
        functools.partial(_moe_finish_kernel, final_norm=final_norm),
        grid=(nt,),
        in_specs=[pl.BlockSpec((tm, d), lambda i: (i, 0)),
                  pl.BlockSpec((tm, d), lambda i: (i, 0)),
                  pl.BlockSpec((tm, d), lambda i: (nt + i, 0)),
                  pl.BlockSpec((tm, LANES), lambda i: (i, 0)),
                  pl.BlockSpec((1, d), lambda i: (0, 0))],
        out_specs=pl.BlockSpec((tm, d), lambda i: (i, 0)),
        out_shape=jax.ShapeDtypeStruct((n, d), F32),
        compiler_params=_params(("parallel",), 48),
        name="moe_finish",
    )(x, y, y, gates, fg)


def _in_proj_weights(w_in):
    off = np.concatenate([[0], np.cumsum(IN_SPLITS)])
    seg = lambda i: w_in[:, off[i]:off[i + 1]]
    dkv = seg(9).reshape(-1, 2 * SWA_KV_HEADS, 1, SWA_HEAD_DIM)
    dkv = jnp.broadcast_to(dkv, dkv.shape[:2] + (LANES // SWA_HEAD_DIM, SWA_HEAD_DIM)).reshape(w_in.shape[0], -1)
    pad = jnp.zeros((w_in.shape[0], SMALL_COLS - (SMALL_MISC + 64 + 24 + 8)), w_in.dtype)
    return jnp.concatenate([seg(0), seg(1), seg(3), seg(8), dkv,
                            seg(5), seg(6), seg(7), seg(2), seg(4), pad], axis=1).astype(BF16)


def _mla_weights(w_uq, w_ukv):
    per = MLA_NOPE_DIM + MLA_ROPE_DIM
    w3 = w_uq.reshape(MLA_Q_RANK, MLA_HEADS, per)
    nope = w3[:, :, :MLA_NOPE_DIM].reshape(MLA_Q_RANK, MLA_HEADS * MLA_NOPE_DIM)
    rope = jnp.pad(w3[:, :, MLA_NOPE_DIM:], ((0, 0), (0, 0), (0, LANES - MLA_ROPE_DIM)))
    rope = rope.reshape(MLA_Q_RANK, MLA_HEADS * LANES)
    return jnp.concatenate([nope, rope], axis=1).astype(BF16), w_ukv.astype(BF16)


def _moe_tables(pos, counts, starts, n_tok):
    n_assign = n_tok * TOP_K
    nblk = (n_assign + N_EXPERTS * (MOE_ROWS - 1)) // MOE_ROWS
    cnt = counts[:, 0].astype(I32)
    pstart = starts[:, 0].astype(I32)
    pend = pstart + (cnt + MOE_ROWS - 1) // MOE_ROWS * MOE_ROWS
    dst_row = jnp.zeros((nblk * MOE_ROWS,), I32).at[pos[0:TOP_K].reshape(-1)].set(jnp.arange(n_assign, dtype=I32))
    row0 = jnp.arange(nblk, dtype=I32) * MOE_ROWS
    blk_e = jnp.minimum(jnp.sum((pend[None, :] <= row0[:, None]).astype(I32), axis=1), N_EXPERTS - 1)
    blk_rows = jnp.clip(cnt[blk_e] - (row0 - pstart[blk_e]), 0, MOE_ROWS).astype(I32)
    return dst_row, blk_e, blk_rows


def kernel(x, norm_mix_g, w_in, nsa_kc_pos, nsa_kc_w1, nsa_kc_w2, nsa_vc_pos, nsa_vc_w1, nsa_vc_w2, fox_f_bias,
           mla_q_norm_g, mla_kv_norm_g, mla_w_uq, mla_w_ukv, swa_sinks, out_norm_g, w_out, norm_ffn_g,
           router_group_w, router_group_b, router_expert_w, router_expert_b, exp_w_gate, exp_w_up, exp_w_down,
           final_norm_g):
    bsz, seq, d_model = x.shape
    n = bsz * seq
    depth = w_in.shape[0]
    xs = x.reshape(n, d_model).astype(F32)
    nsa_slopes = _alibi_slopes(NSA_HEADS)
    swa_slopes = _alibi_slopes(SWA_HEADS)
    nsa_nr = NSA_HEADS // NSA_KV_HEADS
    for l in range(depth):
        w_proj = _in_proj_weights(w_in[l])
        main = _norm_matmul(xs, norm_mix_g[l], w_proj, out_dtype=BF16, n=MAIN_COLS, tn=1024, name="in_proj_main")
        small = _norm_matmul(xs, norm_mix_g[l], w_proj, out_dtype=F32, w_col0=MAIN_COLS, n=SMALL_COLS, tn=512,
                             name="in_proj_small")

        pos = jnp.stack([nsa_kc_pos[l], nsa_vc_pos[l]]).astype(F32)
        w1 = jnp.stack([nsa_kc_w1[l], nsa_vc_w1[l]]).astype(BF16)
        w2 = jnp.stack([nsa_kc_w2[l], nsa_vc_w2[l]]).astype(BF16)
        kvc = _compress(main, pos, w1, w2, bsz, seq)
        o_cmp, sel = _nsa_cmp(main, kvc, nsa_slopes, bsz, seq)
        o_slc = _nsa_slc(main, sel, nsa_slopes, bsz, seq)
        o_win = _banded_attn(main, nsa_slopes, None, bsz, seq, ng=NSA_KV_HEADS, nr=nsa_nr, hd=NSA_HEAD_DIM,
                             window=NSA_WINDOW, q_col=MAIN_AQ, k_col=MAIN_AKV + 8 * NSA_HEAD_DIM,
                             v_col=MAIN_AKV + 10 * NSA_HEAD_DIM, name="nsa_win", tq=256)
        out_a = _nsa_gate(o_cmp, o_slc, o_win, small)

        cumt = _fox_prep(small, fox_f_bias[l], bsz, seq)
        out_b = _fox_attn(main, cumt, bsz, seq)

        w_uq, w_ukv = _mla_weights(mla_w_uq[l], mla_w_ukv[l])
        qup = _norm_matmul(small, mla_q_norm_g[l], w_uq, out_dtype=F32, col_off=SMALL_CQ, k=MLA_Q_RANK,
                           name="mla_q_up")
        kvup = _norm_matmul(small, mla_kv_norm_g[l], w_ukv, out_dtype=BF16, col_off=SMALL_CKV, k=MLA_KV_RANK,
                            name="mla_kv_up")
        qrope, krope = _rope(qup, small, bsz, seq)
        out_c = _mla_attn(qup, qrope, kvup, krope, bsz, seq)

        out_d = _banded_attn(main, swa_slopes, swa_sinks[l], bsz, seq, ng=SWA_KV_HEADS,
                             nr=SWA_HEADS // SWA_KV_HEADS, hd=SWA_HEAD_DIM, window=SWA_WINDOW, q_col=MAIN_DQ,
                             k_col=MAIN_DKV, v_col=MAIN_DKV + SWA_KV_HEADS * LANES, name="swa")

        xs = _out_proj((out_a, out_b, out_c, out_d), out_norm_g[l], w_out[l].astype(BF16), xs)

        xn, route, gates = _router(xs, norm_ffn_g[l], router_group_w[l], router_group_b[l],
                                   router_expert_w[l], router_expert_b[l])
        pos_rows, counts, starts = _slots(route)
        dst_row, blk_e, blk_rows = _moe_tables(pos_rows, counts, starts, n)
        y = _experts(xn, dst_row, blk_e, blk_rows, exp_w_gate, exp_w_up, exp_w_down, l)
        xs = _moe_finish(y, xs, gates, final_norm_g if l == depth - 1 else None)
    return xs.reshape(bsz, seq, d_model)
```

```python
import functools
import math

import numpy as np
import jax
import jax.numpy as jnp
from jax import lax
from jax.experimental import pallas as pl
from jax.experimental.pallas import tpu as pltpu

F32 = jnp.float32
BF16 = jnp.bfloat16
I32 = jnp.int32

NEG_INF = -1e30
NORM_EPS = 1e-6
LANES = 128
ROW_CHUNK = 128
SCORE_ROWS = 512
LOG2E = 1.4426950408889634

Q_BLOCK = 128
GROUP_WIDTH = 1024
NSA_HEADS, NSA_KV_HEADS, NSA_HEAD_DIM = 8, 2, 128
NSA_CMP_STRIDE, NSA_CMP_LEN, NSA_SEL_BLOCK, NSA_TOP_N, NSA_WINDOW = 16, 32, 64, 16, 512
FOX_HEADS, FOX_HEAD_DIM = 8, 128
MLA_HEADS, MLA_Q_RANK, MLA_KV_RANK, MLA_NOPE_DIM, MLA_ROPE_DIM, MLA_V_DIM = 8, 768, 512, 128, 64, 128
ROPE_THETA = 10000.0
SWA_HEADS, SWA_KV_HEADS, SWA_HEAD_DIM, SWA_WINDOW = 16, 2, 64, 128
N_GROUPS, EXPERTS_PER_GROUP, TOP_K, D_EXPERT = 8, 8, 2, 384
N_EXPERTS = N_GROUPS * EXPERTS_PER_GROUP
IN_SPLITS = (1024, 1536, 24, 3072, 8, 768, 512, 64, 1024, 256)

MAIN_AQ, MAIN_AKV, MAIN_BQKV, MAIN_DQ, MAIN_DKV, MAIN_COLS = 0, 1024, 2560, 5632, 6656, 7168
SMALL_CQ, SMALL_CKV, SMALL_MISC, SMALL_COLS = 0, 768, 1280, 1536
MISC_GATE_LANE, MISC_FORGET_LANE = 64, 88

MOE_ROWS = 512
MOE_SUB_ROWS = 128
DE_CHUNK = 128
DMA_UNROLL = 8


def _dot(a, b):
    return jnp.dot(a, b, preferred_element_type=F32)


def _dot_nt(a, b):
    return lax.dot_general(a, b, (((1,), (1,)), ((), ())), preferred_element_type=F32)


def _alibi_slopes(n_heads):
    return jnp.exp2(-8.0 * jnp.arange(1, n_heads + 1, dtype=F32) / n_heads)


def _split3(x):
    hi = x.astype(BF16)
    r1 = x - hi.astype(F32)
    mid = r1.astype(BF16)
    lo = (r1 - mid.astype(F32)).astype(BF16)
    return hi, mid, lo


def _params(sem, vmem_mb=None):
    kw = dict(dimension_semantics=sem)
    if vmem_mb is not None:
        kw["vmem_limit_bytes"] = vmem_mb * 1024 * 1024
    return pltpu.CompilerParams(**kw)


def _norm_matmul_kernel(x_ref, g_ref, w_ref, o_ref, xn_ref, *, col_off, k):
    @pl.when(pl.program_id(1) == 0)
    def _():
        x = x_ref[:, col_off:col_off + k].astype(F32)
        ms = jnp.mean(x * x, axis=-1, keepdims=True)
        xn_ref[...] = (x * lax.rsqrt(ms + NORM_EPS) * g_ref[...]).astype(BF16)

    o_ref[...] = _dot(xn_ref[...], w_ref[...]).astype(o_ref.dtype)


def _norm_matmul(x, gain, w, *, out_dtype, col_off=0, k=None, w_col0=0, n=None, tm=512, tn=512,
                 name="norm_matmul"):
    m, kfull = x.shape
    k = kfull if k is None else k
    n = w.shape[1] if n is None else n
    tm = min(tm, m)
    assert m % tm == 0 and n % tn == 0 and w_col0 % tn == 0 and w.shape[0] == k
    wb0 = w_col0 // tn
    return pl.pallas_call(
        functools.partial(_norm_matmul_kernel, col_off=col_off, k=k),
        grid=(m // tm, n // tn),
        in_specs=[
            pl.BlockSpec((tm, kfull), lambda i, j: (i, 0)),
            pl.BlockSpec((1, k), lambda i, j: (0, 0)),
            pl.BlockSpec((k, tn), lambda i, j: (0, wb0 + j)),
        ],
        out_specs=pl.BlockSpec((tm, tn), lambda i, j: (i, j)),
        out_shape=jax.ShapeDtypeStruct((m, n), out_dtype),
        scratch_shapes=[pltpu.VMEM((tm, k), BF16)],
        compiler_params=_params(("parallel", "arbitrary"), 56),
        name=name,
    )(x, gain.reshape(1, k).astype(F32), w)


def _out_proj_kernel(a_ref, b_ref, c_ref, d_ref, g_ref, w_ref, r_ref, o_ref, xn_ref):
    @pl.when(pl.program_id(1) == 0)
    def _():
        for i, ref in enumerate((a_ref, b_ref, c_ref, d_ref)):
            x = ref[...]
            ms = jnp.mean(x * x, axis=-1, keepdims=True)
            g = g_ref[:, i * GROUP_WIDTH:(i + 1) * GROUP_WIDTH]
            xn_ref[:, i * GROUP_WIDTH:(i + 1) * GROUP_WIDTH] = (x * lax.rsqrt(ms + NORM_EPS) * g).astype(BF16)

    o_ref[...] = r_ref[...] + _dot(xn_ref[...], w_ref[...])


def _out_proj(outs, gain, w, resid, *, tm=512, tn=1024):
    m, d = resid.shape
    tm = min(tm, m)
    k = 4 * GROUP_WIDTH
    grp = pl.BlockSpec((tm, GROUP_WIDTH), lambda i, j: (i, 0))
    return pl.pallas_call(
        _out_proj_kernel,
        grid=(m // tm, d // tn),
        in_specs=[grp, grp, grp, grp,
                  pl.BlockSpec((1, k), lambda i, j: (0, 0)),
                  pl.BlockSpec((k, tn), lambda i, j: (0, j)),
                  pl.BlockSpec((tm, tn), lambda i, j: (i, j))],
        out_specs=pl.BlockSpec((tm, tn), lambda i, j: (i, j)),
        out_shape=jax.ShapeDtypeStruct((m, d), F32),
        scratch_shapes=[pltpu.VMEM((tm, k), BF16)],
        compiler_params=_params(("parallel", "arbitrary"), 56),
        name="out_proj",
    )(*outs, gain.reshape(1, k).astype(F32), w, resid)


def _fox_prep_kernel(x_ref, b_ref, cumt_ref, carry_ref, *, ts):
    @pl.when(pl.program_id(1) == 0)
    def _():
        carry_ref[...] = jnp.zeros_like(carry_ref)

    z = x_ref[...] + b_ref[...]
    lf = jnp.minimum(z, 0.0) - jnp.log1p(jnp.exp(-jnp.abs(z)))
    row = lax.broadcasted_iota(I32, (ts, ts), 0)
    col = lax.broadcasted_iota(I32, (ts, ts), 1)
    tri = jnp.where(col <= row, 1.0, 0.0).astype(BF16)
    hi, mid, lo = _split3(lf)
    cum = (_dot(tri, hi) + _dot(tri, mid)) + _dot(tri, lo) + carry_ref[...]
    carry_ref[...] = cum[ts - 1:ts, :]
    cum_t = jnp.concatenate([cum[i * LANES:(i + 1) * LANES, :].T for i in range(ts // LANES)], axis=1)
    cumt_ref[0] = cum_t[MISC_FORGET_LANE:MISC_FORGET_LANE + FOX_HEADS, :]


def _fox_prep(small, f_bias, bsz, seq, *, ts=512):
    ts = min(ts, seq)
    n = bsz * seq
    ns = seq // ts
    bias = jnp.zeros((1, LANES), F32).at[0, MISC_FORGET_LANE:MISC_FORGET_LANE + FOX_HEADS].set(f_bias.astype(F32))
    misc_blk = SMALL_MISC // LANES
    return pl.pallas_call(
        functools.partial(_fox_prep_kernel, ts=ts),
        grid=(bsz, ns),
        in_specs=[pl.BlockSpec((ts, LANES), lambda b, i: (b * ns + i, misc_blk)),
                  pl.BlockSpec((1, LANES), lambda b, i: (0, 0))],
        out_specs=pl.BlockSpec((1, FOX_HEADS, ts), lambda b, i: (b, 0, i)),
        out_shape=jax.ShapeDtypeStruct((bsz, FOX_HEADS, seq), F32),
        scratch_shapes=[pltpu.VMEM((1, LANES), F32)],
        compiler_params=_params(("parallel", "arbitrary")),
        name="fox_prep",
    )(small, bias)


def _with_ones(v):
    return jnp.concatenate([v, jnp.ones((v.shape[0], LANES), v.dtype)], axis=1)


def _flash_rows(s2, v1, m_sc, l_sc, acc_sc, r0, rows):
    sl = slice(r0, r0 + rows)
    dv = v1.shape[1] - LANES
    m_prev = m_sc[sl]
    m_new = jnp.maximum(m_prev, jnp.max(s2, axis=1, keepdims=True))
    alpha = jnp.exp2(m_prev - m_new)
    p = jnp.concatenate([jnp.exp2(s2[:, j * LANES:(j + 1) * LANES] - m_new)
                         for j in range(s2.shape[1] // LANES)], axis=1)
    pv = _dot(p.astype(BF16), v1)
    l_sc[sl] = alpha * l_sc[sl] + pv[:, dv:]
    acc_sc[sl] = alpha * acc_sc[sl] + pv[:, :dv]
    m_sc[sl] = m_new


def _flash_init(m_sc, l_sc, acc_sc):
    m_sc[...] = jnp.full(m_sc.shape, NEG_INF, F32)
    l_sc[...] = jnp.zeros(l_sc.shape, F32)
    acc_sc[...] = jnp.zeros(acc_sc.shape, F32)


def _pipelined_sweep(n_full, issue, softmax_pv, last_tile, sa_sc, sb_sc):
    def full_tile(ki, src, dst):
        issue(ki + 1, dst)
        softmax_pv(ki, src)

    issue(0, sa_sc)

    def pair(j, carry):
        full_tile(2 * j, sa_sc, sb_sc)
        full_tile(2 * j + 1, sb_sc, sa_sc)
        return carry

    lax.fori_loop(0, n_full // 2, pair, 0)

    @pl.when(n_full % 2 == 1)
    def _():
        full_tile(n_full - 1, sa_sc, sb_sc)
        last_tile(sb_sc)

    @pl.when(n_full % 2 == 0)
    def _():
        last_tile(sa_sc)


def _causal_sweep(qi, tq, score_fn, load_v, sa_sc, sb_sc, m_sc, l_sc, acc_sc):
    nch = tq // ROW_CHUNK
    chunk = lambda c: slice(c * ROW_CHUNK, (c + 1) * ROW_CHUNK)
    score_rows = min(SCORE_ROWS, tq)

    def issue(ki, dst):
        for c in range(tq // score_rows):
            rows = slice(c * score_rows, (c + 1) * score_rows)
            dst[rows, :] = score_fn(rows, ki)

    def softmax_pv(ki, src):
        v = _with_ones(load_v(ki, tq))
        for c in range(nch):
            _flash_rows(src[chunk(c), :], v, m_sc, l_sc, acc_sc, c * ROW_CHUNK, ROW_CHUNK)

    def diag_tile(src):
        for c in range(nch):
            ncols = (c + 1) * ROW_CHUNK
            row = lax.broadcasted_iota(I32, (ROW_CHUNK, ncols), 0) + c * ROW_CHUNK
            col = lax.broadcasted_iota(I32, (ROW_CHUNK, ncols), 1)
            s2 = jnp.where(col <= row, src[chunk(c), 0:ncols], NEG_INF)
            _flash_rows(s2, _with_ones(load_v(qi, ncols)), m_sc, l_sc, acc_sc, c * ROW_CHUNK, ROW_CHUNK)

    _pipelined_sweep(qi, issue, softmax_pv, diag_tile, sa_sc, sb_sc)


def _fox_attn_kernel(q_ref, k_ref, v_ref, ck_ref, o_ref, sa_sc, sb_sc, m_sc, l_sc, acc_sc, *, tq, scale):
    qi = pl.program_id(2)
    _flash_init(m_sc, l_sc, acc_sc)
    c0 = ck_ref[0, pl.ds(qi, 1), :][:, 0:1]

    def score(rows, ki):
        k0 = pl.multiple_of(ki * tq, tq)
        k = k_ref[pl.ds(k0, tq), :]
        ck2 = (ck_ref[0, pl.ds(ki, 1), :] - c0) * LOG2E
        return _dot_nt(q_ref[rows, :], k) * (scale * LOG2E) - ck2

    def load_v(ki, ncols):
        return v_ref[pl.ds(pl.multiple_of(ki * tq, tq), ncols), :]

    _causal_sweep(qi, tq, score, load_v, sa_sc, sb_sc, m_sc, l_sc, acc_sc)
    o_ref[...] = acc_sc[...] / l_sc[...]


def _fox_attn(main, cumt, bsz, seq, *, tq=512):
    tq = min(tq, seq)
    nq = seq // tq
    n = bsz * seq
    d = FOX_HEAD_DIM
    qb, kb, vb = MAIN_BQKV // d, MAIN_BQKV // d + FOX_HEADS, MAIN_BQKV // d + 2 * FOX_HEADS
    cumt3 = cumt.reshape(bsz * FOX_HEADS, nq, tq)
    return pl.pallas_call(
        functools.partial(_fox_attn_kernel, tq=tq, scale=d ** -0.5),
        grid=(bsz, FOX_HEADS, nq),
        in_specs=[pl.BlockSpec((tq, d), lambda b, h, i: (b * nq + i, qb + h)),
                  pl.BlockSpec((seq, d), lambda b, h, i: (b, kb + h)),
                  pl.BlockSpec((seq, d), lambda b, h, i: (b, vb + h)),
                  pl.BlockSpec((1, nq, tq), lambda b, h, i: (b * FOX_HEADS + h, 0, 0))],
        out_specs=pl.BlockSpec((tq, d), lambda b, h, i: (b * nq + i, h)),
        out_shape=jax.ShapeDtypeStruct((n, FOX_HEADS * d), F32),
        scratch_shapes=[pltpu.VMEM((tq, tq), F32), pltpu.VMEM((tq, tq), F32),
                        pltpu.VMEM((tq, LANES), F32), pltpu.VMEM((tq, LANES), F32), pltpu.VMEM((tq, d), F32)],
        compiler_params=_params(("parallel", "parallel", "arbitrary")),
        name="fox_attn",
    )(main, main, main, cumt3)


def _mla_attn_kernel(qn_ref, qr_ref, kn_ref, kr_ref, v_ref, o_ref, q_sc, sa_sc, sb_sc, m_sc, l_sc, acc_sc,
                     *, tq, scale):
    qi = pl.program_id(2)
    _flash_init(m_sc, l_sc, acc_sc)
    q_sc[:, :LANES] = qn_ref[...].astype(BF16)
    q_sc[:, LANES:] = qr_ref[...]

    def score(rows, ki):
        k0 = pl.multiple_of(ki * tq, tq)
        k = jnp.concatenate([kn_ref[pl.ds(k0, tq), :], kr_ref[pl.ds(k0, tq), :]], axis=1)
        return _dot_nt(q_sc[rows, :], k) * (scale * LOG2E)

    def load_v(ki, ncols):
        return v_ref[pl.ds(pl.multiple_of(ki * tq, tq), ncols), :]

    _causal_sweep(qi, tq, score, load_v, sa_sc, sb_sc, m_sc, l_sc, acc_sc)
    o_ref[...] = acc_sc[...] / l_sc[...]


def _mla_attn(qup, qrope, kvup, krope, bsz, seq, *, tq=512):
    tq = min(tq, seq)
    nq = seq // tq
    n = bsz * seq
    d = LANES
    return pl.pallas_call(
        functools.partial(_mla_attn_kernel, tq=tq, scale=(MLA_NOPE_DIM + MLA_ROPE_DIM) ** -0.5),
        grid=(bsz, MLA_HEADS, nq),
        in_specs=[pl.BlockSpec((tq, d), lambda b, h, i: (b * nq + i, h)),
                  pl.BlockSpec((tq, d), lambda b, h, i: (b * nq + i, h)),
                  pl.BlockSpec((seq, d), lambda b, h, i: (b, 2 * h)),
                  pl.BlockSpec((seq, d), lambda b, h, i: (b, 0)),
                  pl.BlockSpec((seq, d), lambda b, h, i: (b, 2 * h + 1))],
        out_specs=pl.BlockSpec((tq, d), lambda b, h, i: (b * nq + i, h)),
        out_shape=jax.ShapeDtypeStruct((n, MLA_HEADS * MLA_V_DIM), F32),
        scratch_shapes=[pltpu.VMEM((tq, 2 * LANES), BF16),
                        pltpu.VMEM((tq, tq), F32), pltpu.VMEM((tq, tq), F32),
                        pltpu.VMEM((tq, LANES), F32), pltpu.VMEM((tq, LANES), F32), pltpu.VMEM((tq, d), F32)],
        compiler_params=_params(("parallel", "parallel", "arbitrary")),
        name="mla_attn",
    )(qup, qrope, kvup, krope, kvup)


def _rope_kernel(q_ref, k_ref, cos_ref, sin_ref, qo_ref, ko_ref):
    half = MLA_ROPE_DIM // 2

    def rope(x):
        lane = lax.broadcasted_iota(I32, x.shape, 1)
        first = (lane % MLA_ROPE_DIM) < half
        swapped = jnp.where(first, pltpu.roll(x, LANES - half, 1), pltpu.roll(x, half, 1))
        return x * cos_ref[...] + swapped * sin_ref[...]

    for t in range(q_ref.shape[1] // LANES):
        qo_ref[:, t * LANES:(t + 1) * LANES] = rope(q_ref[:, t * LANES:(t + 1) * LANES]).astype(BF16)
    k = rope(k_ref[...])
    lane = lax.broadcasted_iota(I32, k.shape, 1)
    ko_ref[...] = jnp.where(lane < MLA_ROPE_DIM, k, 0.0).astype(BF16)


def _rope(qup, small, bsz, seq, *, ts=512):
    ts = min(ts, seq)
    ns = seq // ts
    n = bsz * seq
    pos = jnp.arange(seq, dtype=F32)
    inv = ROPE_THETA ** (-jnp.arange(0, MLA_ROPE_DIM, 2, dtype=F32) / MLA_ROPE_DIM)
    ang = pos[:, None] * inv[None, :]
    cos, sin = jnp.cos(ang), jnp.sin(ang)
    cos_t = jnp.concatenate([cos, cos, cos, cos], axis=1)
    sin_t = jnp.concatenate([-sin, sin, -sin, sin], axis=1)
    w = MLA_HEADS * LANES
    return pl.pallas_call(
        _rope_kernel,
        grid=(bsz, ns),
        in_specs=[pl.BlockSpec((ts, w), lambda b, i: (b * ns + i, 1)),
                  pl.BlockSpec((ts, LANES), lambda b, i: (b * ns + i, SMALL_MISC // LANES)),
                  pl.BlockSpec((ts, LANES), lambda b, i: (i, 0)),
                  pl.BlockSpec((ts, LANES), lambda b, i: (i, 0))],
        out_specs=[pl.BlockSpec((ts, w), lambda b, i: (b * ns + i, 0)),
                   pl.BlockSpec((ts, LANES), lambda b, i: (b * ns + i, 0))],
        out_shape=[jax.ShapeDtypeStruct((n, w), BF16), jax.ShapeDtypeStruct((n, LANES), BF16)],
        compiler_params=_params(("parallel", "parallel")),
        name="mla_rope",
    )(qup, small, cos_t, sin_t)


def _banded_kernel(*refs, tq, nr, hd, window, slab, scale, has_sinks, seq):
    if has_sinks:
        slopes_ref, sinks_ref, q_ref, k_ref, v_ref, o_ref = refs
    else:
        slopes_ref, q_ref, k_ref, v_ref, o_ref = refs
    g = pl.program_id(1)
    qi = pl.program_id(2)
    q0 = qi * tq
    start = pl.multiple_of(jnp.minimum(jnp.maximum(q0 + tq - slab, 0), seq - slab), tq)
    kt = k_ref[pl.ds(start, slab), :]
    v1 = _with_ones(v_ref[pl.ds(start, slab), :])
    q = q_ref[...]
    low = lax.broadcasted_iota(I32, (tq, LANES), 1) < hd
    if hd == LANES:
        qs = jnp.concatenate([q[:, r * hd:(r + 1) * hd] for r in range(nr)], axis=0)
    else:
        zero = jnp.zeros((), q.dtype)
        parts = []
        for j in range(nr // 2):
            pair = q[:, j * LANES:(j + 1) * LANES]
            parts += [jnp.where(low, pair, zero), jnp.where(low, zero, pair)]
        qs = jnp.concatenate(parts, axis=0)
    s = _dot_nt(qs, kt) * (scale * LOG2E)
    qpos = q0 + lax.broadcasted_iota(I32, (tq, slab), 0)
    kpos = start + lax.broadcasted_iota(I32, (tq, slab), 1)
    dist = qpos - kpos
    valid = (dist >= 0) & (dist < window)
    krel = (start - q0 + lax.broadcasted_iota(I32, (1, slab), 1)).astype(F32)
    qrel = lax.broadcasted_iota(I32, (tq, 1), 0).astype(F32)
    es, sinks_e = [], []
    for r in range(nr):
        slope2 = slopes_ref[g * nr + r] * LOG2E
        sr = jnp.where(valid, s[r * tq:(r + 1) * tq] + slope2 * krel, NEG_INF)
        m = jnp.max(sr, axis=1, keepdims=True)
        if has_sinks:
            sk = sinks_ref[g * nr + r] * LOG2E + slope2 * qrel
            m = jnp.maximum(m, sk)
            sinks_e.append(jnp.exp2(sk - m))
        es.append(jnp.exp2(sr - m).astype(BF16))
    pv = _dot(jnp.concatenate(es, axis=0), v1)
    den = pv[:, LANES:]
    if has_sinks:
        den = den + jnp.concatenate(sinks_e, axis=0)
    o = pv[:, :LANES] / den
    if hd == LANES:
        o_ref[...] = jnp.concatenate([o[r * tq:(r + 1) * tq] for r in range(nr)], axis=1)
    else:
        o_ref[...] = jnp.concatenate(
            [jnp.where(low, o[2 * j * tq:(2 * j + 1) * tq], o[(2 * j + 1) * tq:(2 * j + 2) * tq])
             for j in range(nr // 2)], axis=1)


def _banded_attn(main, slopes, sinks, bsz, seq, *, ng, nr, hd, window, q_col, k_col, v_col, name, tq=128):
    tq = min(tq, seq)
    nq = seq // tq
    n = bsz * seq
    slab = min(-(-(window - 1) // tq) * tq + tq, seq)
    qw = nr * hd
    has_sinks = sinks is not None
    smem = pl.BlockSpec(memory_space=pltpu.SMEM)
    in_specs = [smem] + ([smem] if has_sinks else []) + [
        pl.BlockSpec((tq, qw), lambda b, g, i: (b * nq + i, q_col // qw + g)),
        pl.BlockSpec((seq, LANES), lambda b, g, i: (b, k_col // LANES + g)),
        pl.BlockSpec((seq, LANES), lambda b, g, i: (b, v_col // LANES + g))]
    args = [slopes] + ([sinks.astype(F32)] if has_sinks else []) + [main, main, main]
    return pl.pallas_call(
        functools.partial(_banded_kernel, tq=tq, nr=nr, hd=hd, window=window, slab=slab,
                          scale=hd ** -0.5, has_sinks=has_sinks, seq=seq),
        grid=(bsz, ng, nq),
        in_specs=in_specs,
        out_specs=pl.BlockSpec((tq, qw), lambda b, g, i: (b * nq + i, g)),
        out_shape=jax.ShapeDtypeStruct((n, ng * qw), F32),
        compiler_params=_params(("parallel", "parallel", "parallel")),
        name=name,
    )(*args)


def _compress_kernel(x_ref, pos_ref, w1_ref, w2_ref, o_ref, xf_ref, *, seq):
    nc = seq // NSA_CMP_STRIDE
    hd = NSA_HEAD_DIM
    xf_ref[0:seq, :] = x_ref[...].astype(F32)
    xf_ref[seq:seq + NSA_CMP_STRIDE, :] = jnp.zeros((NSA_CMP_STRIDE, hd), F32)
    acc = jnp.zeros((nc, hd), F32)
    for j in range(NSA_CMP_LEN):
        rows = xf_ref[pl.ds(j, nc, stride=NSA_CMP_STRIDE), :] + pos_ref[0, j:j + 1, :]
        acc = acc + _dot(rows.astype(BF16), w1_ref[0, j * hd:(j + 1) * hd, :])
    hid = jax.nn.gelu(acc)
    o_ref[0, 0, 0] = _dot(hid.astype(BF16), w2_ref[0]).astype(BF16)


def _compress(main, pos, w1, w2, bsz, seq):
    nc = seq // NSA_CMP_STRIDE
    hd = NSA_HEAD_DIM
    ng = NSA_KV_HEADS
    col0 = MAIN_AKV // hd
    return pl.pallas_call(
        functools.partial(_compress_kernel, seq=seq),
        grid=(bsz, ng, 2),
        in_specs=[pl.BlockSpec((seq, hd), lambda b, g, t: (b, col0 + t * ng + g)),
                  pl.BlockSpec((1, NSA_CMP_LEN, hd), lambda b, g, t: (t, 0, 0)),
                  pl.BlockSpec((1, NSA_CMP_LEN * hd, hd), lambda b, g, t: (t, 0, 0)),
                  pl.BlockSpec((1, hd, hd), lambda b, g, t: (t, 0, 0))],
        out_specs=pl.BlockSpec((1, 1, 1, nc, hd), lambda b, g, t: (b, g, t, 0, 0)),
        out_shape=jax.ShapeDtypeStruct((bsz, ng, 2, nc, hd), BF16),
        scratch_shapes=[pltpu.VMEM((seq + NSA_CMP_STRIDE, hd), F32)],
        compiler_params=_params(("parallel", "parallel", "parallel")),
        name="nsa_compress",
    )(main, pos, w1, w2)


def _nsa_cmp_kernel(slopes_ref, q_ref, kc_ref, vc_ref, o_ref, sel_ref, *, tq, nc, nr):
    g = pl.program_id(1)
    qi = pl.program_id(2)
    hd = NSA_HEAD_DIM
    q0 = qi * tq
    q = q_ref[...]
    qs = jnp.concatenate([q[:, r * hd:(r + 1) * hd] for r in range(nr)], axis=0)
    kc = kc_ref[0, 0, 0]
    vc = vc_ref[0, 0, 0]
    s = _dot_nt(qs, kc) * (hd ** -0.5)
    tpos = q0 + lax.broadcasted_iota(I32, (tq, nc), 0)
    cend = lax.broadcasted_iota(I32, (tq, nc), 1) * NSA_CMP_STRIDE + (NSA_CMP_LEN - 1)
    dist = tpos - cend
    valid = dist >= 0
    distf = dist.astype(F32)
    ps = []
    psum = jnp.zeros((tq, nc), F32)
    for r in range(nr):
        sr = s[r * tq:(r + 1) * tq] - slopes_ref[g * nr + r] * distf
        sr = jnp.where(valid, sr, NEG_INF)
        m = jnp.max(sr, axis=1, keepdims=True)
        e = jnp.exp(sr - m)
        p = jnp.where(valid, e / jnp.sum(e, axis=1, keepdims=True), 0.0)
        psum = psum + p
        ps.append(p.astype(BF16))
    o = _dot(jnp.concatenate(ps, axis=0), vc)
    o_ref[...] = jnp.concatenate([o[r * tq:(r + 1) * tq] for r in range(nr)], axis=1)

    nb = LANES
    n_slc = nc * NSA_CMP_STRIDE // NSA_SEL_BLOCK
    per = NSA_SEL_BLOCK // NSA_CMP_STRIDE
    blk = lax.broadcasted_iota(I32, (nb, nc), 0)
    cidx = lax.broadcasted_iota(I32, (nb, nc), 1)
    overlap = (cidx <= per * blk + per - 1) & (cidx >= per * blk - 1) & (cidx < nc - 1) & (blk < n_slc)
    ov = jnp.where(overlap, 1.0, 0.0).astype(BF16)
    hi, mid, lo = _split3(psum)
    imp = (_dot_nt(ov, hi) + _dot_nt(ov, mid)) + _dot_nt(ov, lo)
    j = lax.broadcasted_iota(I32, (nb, tq), 0)
    cur = (q0 + lax.broadcasted_iota(I32, (nb, tq), 1)) // NSA_SEL_BLOCK
    forced = (j == 0) | (j == cur) | (j == cur - 1)
    imp = jnp.where(forced, 1e6, imp)
    imp = jnp.where(j > cur, -1e6, imp)
    imp = jnp.where(j >= n_slc, -3e38, imp)
    rank = jnp.zeros((nb, tq), F32)
    for i in range(n_slc):
        vi = imp[i:i + 1, :]
        ahead = (vi > imp) | ((vi == imp) & (j > i))
        rank = rank + jnp.where(ahead, 1.0, 0.0)
    sel = jnp.where(rank < float(min(NSA_TOP_N, n_slc)), 0.0, NEG_INF)
    sel_ref[...] = sel.T.astype(BF16)


def _nsa_cmp(main, kvc, slopes, bsz, seq, *, tq=128):
    tq = min(tq, seq)
    nq = seq // tq
    n = bsz * seq
    ng, nr, hd = NSA_KV_HEADS, NSA_HEADS // NSA_KV_HEADS, NSA_HEAD_DIM
    nc = seq // NSA_CMP_STRIDE
    qw = nr * hd
    smem = pl.BlockSpec(memory_space=pltpu.SMEM)
    return pl.pallas_call(
        functools.partial(_nsa_cmp_kernel, tq=tq, nc=nc, nr=nr),
        grid=(bsz, ng, nq),
        in_specs=[smem,
                  pl.BlockSpec((tq, qw), lambda b, g, i: (b * nq + i, g)),
                  pl.BlockSpec((1, 1, 1, nc, hd), lambda b, g, i: (b, g, 0, 0, 0)),
                  pl.BlockSpec((1, 1, 1, nc, hd), lambda b, g, i: (b, g, 1, 0, 0))],
        out_specs=[pl.BlockSpec((tq, qw), lambda b, g, i: (b * nq + i, g)),
                   pl.BlockSpec((tq, LANES), lambda b, g, i: ((b * ng + g) * nq + i, 0))],
        out_shape=[jax.ShapeDtypeStruct((n, ng * qw), F32),
                   jax.ShapeDtypeStruct((bsz * ng * seq, LANES), BF16)],
        compiler_params=_params(("parallel", "parallel", "parallel")),
        name="nsa_cmp",
    )(slopes, main, kvc, kvc)


def _nsa_slc_kernel(slopes_ref, q_ref, k_ref, v_ref, sel_ref, o_ref, sa_sc, sb_sc, m_sc, l_sc, acc_sc,
                    *, tq, tk, nr):
    g = pl.program_id(1)
    qi = pl.program_id(2)
    hd = NSA_HEAD_DIM
    q0 = qi * tq
    _flash_init(m_sc, l_sc, acc_sc)
    last = q0 // tk
    c2 = (hd ** -0.5) * LOG2E
    head = lambda r: slice(r * tq, (r + 1) * tq)

    shift = int(math.log2(NSA_SEL_BLOCK))
    lane_minus_blk = (lax.broadcasted_iota(I32, (tk, LANES), 1)
                      - lax.shift_right_logical(lax.broadcasted_iota(I32, (tk, LANES), 0), shift))

    def issue(ki, dst):
        k0 = pl.multiple_of(ki * tk, tk)
        onehot = jnp.where(lane_minus_blk == ki * (tk // NSA_SEL_BLOCK), 1.0, 0.0).astype(BF16)
        ka = jnp.concatenate([k_ref[pl.ds(k0, tk), :], onehot], axis=1)
        rel = (k0 - q0 + lax.broadcasted_iota(I32, (1, tk), 1)).astype(F32)
        for r in range(nr):
            qa = jnp.concatenate([q_ref[:, r * hd:(r + 1) * hd], sel_ref[...]], axis=1)
            dst[head(r), :] = _dot_nt(qa, ka) * c2 + (slopes_ref[g * nr + r] * LOG2E) * rel

    def load_v(ki):
        return _with_ones(v_ref[pl.ds(pl.multiple_of(ki * tk, tk), tk), :])

    rc = min(ROW_CHUNK, tq)
    chains = [(r * tq + c * rc, c * rc) for r in range(nr) for c in range(tq // rc)]

    def softmax_pv(ki, src):
        v = load_v(ki)
        for row0, _ in chains:
            _flash_rows(src[row0:row0 + rc, :], v, m_sc, l_sc, acc_sc, row0, rc)

    def last_tile(src):
        v = load_v(last)
        k0 = last * tk
        ahead = (k0 + lax.broadcasted_iota(I32, (tq, tk), 1)) > (q0 + lax.broadcasted_iota(I32, (tq, tk), 0))
        for row0, qrow in chains:
            s2 = jnp.where(ahead[qrow:qrow + rc], NEG_INF, src[row0:row0 + rc, :])
            _flash_rows(s2, v, m_sc, l_sc, acc_sc, row0, rc)

    _pipelined_sweep(last, issue, softmax_pv, last_tile, sa_sc, sb_sc)
    o = acc_sc[...] / l_sc[...]
    o_ref[...] = jnp.concatenate([o[r * tq:(r + 1) * tq] for r in range(nr)], axis=1)


def _nsa_slc(main, sel, slopes, bsz, seq, *, tq=256, tk=512):
    tq = min(tq, seq)
    tk = min(tk, seq)
    nq = seq // tq
    n = bsz * seq
    ng, nr, hd = NSA_KV_HEADS, NSA_HEADS // NSA_KV_HEADS, NSA_HEAD_DIM
    qw = nr * hd
    kb = MAIN_AKV // hd + 2 * ng
    vb = MAIN_AKV // hd + 3 * ng
    smem = pl.BlockSpec(memory_space=pltpu.SMEM)
    return pl.pallas_call(
        functools.partial(_nsa_slc_kernel, tq=tq, tk=tk, nr=nr),
        grid=(bsz, ng, nq),
        in_specs=[smem,
                  pl.BlockSpec((tq, qw), lambda b, g, i: (b * nq + i, g)),
                  pl.BlockSpec((seq, hd), lambda b, g, i: (b, kb + g)),
                  pl.BlockSpec((seq, hd), lambda b, g, i: (b, vb + g)),
                  pl.BlockSpec((tq, LANES), lambda b, g, i: ((b * ng + g) * nq + i, 0))],
        out_specs=pl.BlockSpec((tq, qw), lambda b, g, i: (b * nq + i, g)),
        out_shape=jax.ShapeDtypeStruct((n, ng * qw), F32),
        scratch_shapes=[pltpu.VMEM((nr * tq, tk), F32), pltpu.VMEM((nr * tq, tk), F32),
                        pltpu.VMEM((nr * tq, LANES), F32), pltpu.VMEM((nr * tq, LANES), F32),
                        pltpu.VMEM((nr * tq, hd), F32)],
        compiler_params=_params(("parallel", "parallel", "arbitrary")),
        name="nsa_slc",
    )(slopes, main, main, main, sel)


def _nsa_gate_kernel(c_ref, s_ref, w_ref, g_ref, o_ref):
    gates = jax.nn.sigmoid(g_ref[...])
    hd = NSA_HEAD_DIM
    for h in range(NSA_HEADS):
        sl = slice(h * hd, (h + 1) * hd)
        lane = MISC_GATE_LANE + h
        o_ref[:, sl] = (gates[:, lane:lane + 1] * c_ref[:, sl]
                        + gates[:, lane + NSA_HEADS:lane + NSA_HEADS + 1] * s_ref[:, sl]
                        + gates[:, lane + 2 * NSA_HEADS:lane + 2 * NSA_HEADS + 1] * w_ref[:, sl])


def _nsa_gate(o_cmp, o_slc, o_win, small, *, tm=512):
    n, w = o_cmp.shape
    tm = min(tm, n)
    row = pl.BlockSpec((tm, w), lambda i: (i, 0))
    return pl.pallas_call(
        _nsa_gate_kernel,
        grid=(n // tm,),
        in_specs=[row, row, row, pl.BlockSpec((tm, LANES), lambda i: (i, SMALL_MISC // LANES))],
        out_specs=row,
        out_shape=jax.ShapeDtypeStruct((n, w), F32),
        compiler_params=_params(("parallel",)),
        name="nsa_gate",
    )(o_cmp, o_slc, o_win, small)


def _router_kernel(x_ref, g_ref, wh_ref, wm_ref, br_ref, xn_ref, route_ref, gate_ref, *, tm):
    x = x_ref[...]
    ms = jnp.mean(x * x, axis=-1, keepdims=True)
    xn = x * lax.rsqrt(ms + NORM_EPS) * g_ref[...]
    xn_ref[...] = xn
    xh = xn.astype(BF16)
    xm = (xn - xh.astype(F32)).astype(BF16)
    wh = wh_ref[...]
    lg_rows = _dot(xh, wh) + (_dot(xm, wh) + _dot(xh, wm_ref[...]))
    logits = jnp.concatenate([lg_rows[i * LANES:(i + 1) * LANES, :].T for i in range(tm // LANES)],
                             axis=1) + br_ref[:, 0:1]
    ng, ne = N_GROUPS, EXPERTS_PER_GROUP
    lg = logits[0:ng, :]
    sub = lax.broadcasted_iota(I32, (ng, tm), 0)
    mg = jnp.max(lg, axis=0, keepdims=True)
    eg = jnp.exp(lg - mg)
    pg = eg / jnp.sum(eg, axis=0, keepdims=True)
    pg_top = jnp.max(pg, axis=0, keepdims=True)
    g_idx = jnp.min(jnp.where(pg == pg_top, sub, ng), axis=0, keepdims=True)
    le = jnp.zeros((ne, tm), F32)
    for gi in range(ng):
        le = jnp.where(g_idx == gi, logits[ng + gi * ne:ng + (gi + 1) * ne, :], le)
    v1 = jnp.max(le, axis=0, keepdims=True)
    i1 = jnp.min(jnp.where(le == v1, sub, ne), axis=0, keepdims=True)
    rest = jnp.where(sub == i1, -jnp.inf, le)
    v2 = jnp.max(rest, axis=0, keepdims=True)
    i2 = jnp.min(jnp.where(rest == v2, sub, ne), axis=0, keepdims=True)
    e2 = jnp.exp(v2 - v1)
    den = 1.0 + e2
    w1 = pg_top * (1.0 / den)
    w2 = pg_top * (e2 / den)
    ex1 = (g_idx * ne + i1).astype(F32)
    ex2 = (g_idx * ne + i2).astype(F32)
    zero = jnp.zeros((1, tm), F32)
    route = jnp.concatenate([ex1, ex2, w1, w2, zero, zero, zero, zero], axis=0)
    route_ref[...] = route
    pad = jnp.concatenate([route, jnp.zeros((LANES - 8, tm), F32)], axis=0)
    gate_ref[...] = jnp.concatenate([pad[:, i * LANES:(i + 1) * LANES].T for i in range(tm // LANES)], axis=0)


def _router(x, gain, rg_w, rg_b, re_w, re_b, *, tm=256):
    n, d = x.shape
    tm = min(tm, n)
    nl = N_GROUPS + N_EXPERTS
    wr = jnp.pad(jnp.concatenate([rg_w, re_w], axis=1).astype(F32), ((0, 0), (0, LANES - nl)))
    br = jnp.zeros((LANES, LANES), F32).at[:nl, :].set(
        jnp.concatenate([rg_b, re_b]).astype(F32)[:, None] * jnp.ones((1, LANES), F32))
    wr_hi = wr.astype(BF16)
    wr_mid = (wr - wr_hi.astype(F32)).astype(BF16)
    return pl.pallas_call(
        functools.partial(_router_kernel, tm=tm),
        grid=(n // tm,),
        in_specs=[pl.BlockSpec((tm, d), lambda i: (i, 0)),
                  pl.BlockSpec((1, d), lambda i: (0, 0)),
                  pl.BlockSpec((d, LANES), lambda i: (0, 0)),
                  pl.BlockSpec((d, LANES), lambda i: (0, 0)),
                  pl.BlockSpec((LANES, LANES), lambda i: (0, 0))],
        out_specs=[pl.BlockSpec((tm, d), lambda i: (i, 0)),
                   pl.BlockSpec((8, tm), lambda i: (0, i)),
                   pl.BlockSpec((tm, LANES), lambda i: (i, 0))],
        out_shape=[jax.ShapeDtypeStruct((n, d), F32),
                   jax.ShapeDtypeStruct((8, n), F32),
                   jax.ShapeDtypeStruct((n, LANES), F32)],
        compiler_params=_params(("parallel",), 48),
        name="moe_router",
    )(x, gain.reshape(1, d).astype(F32), wr_hi, wr_mid, br)


def _slot_kernel(route_ref, pos_ref, cnt_ref, start_ref, carry_ref, *, tm):
    phase = pl.program_id(0)
    i = pl.program_id(1)
    e1 = route_ref[0:1, :].astype(I32)
    e2 = route_ref[1:2, :].astype(I32)
    sub = lax.broadcasted_iota(I32, (N_EXPERTS, tm), 0)
    oh1 = jnp.where(sub == e1, 1.0, 0.0)
    oh2 = jnp.where(sub == e2, 1.0, 0.0)
    ohs = oh1 + oh2

    @pl.when((phase == 0) & (i == 0))
    def _():
        carry_ref[...] = jnp.zeros_like(carry_ref)

    @pl.when(phase == 0)
    def _():
        carry_ref[...] = carry_ref[...] + jnp.sum(ohs, axis=1, keepdims=True)
        pos_ref[...] = jnp.zeros(pos_ref.shape, I32)

    @pl.when((phase == 1) & (i == 0))
    def _():
        cnt = carry_ref[...]
        cnt_ref[...] = cnt
        padded = jnp.floor((cnt + (MOE_ROWS - 1)) / MOE_ROWS) * MOE_ROWS
        row = lax.broadcasted_iota(I32, (N_EXPERTS, N_EXPERTS), 0)
        col = lax.broadcasted_iota(I32, (N_EXPERTS, N_EXPERTS), 1)
        lower = jnp.where(col < row, 1.0, 0.0).astype(BF16)
        hi, mid, lo = _split3(padded)
        start_ref[...] = (_dot(lower, hi) + _dot(lower, mid)) + _dot(lower, lo)
        carry_ref[...] = jnp.zeros_like(carry_ref)

    @pl.when(phase == 1)
    def _():
        row = lax.broadcasted_iota(I32, (tm, tm), 0)
        col = lax.broadcasted_iota(I32, (tm, tm), 1)
        upper = jnp.where(row < col, 1.0, 0.0).astype(BF16)
        before = _dot(ohs.astype(BF16), upper) + (carry_ref[:, 0:1] + start_ref[:, 0:1])
        p1 = jnp.sum(oh1 * before, axis=0, keepdims=True)
        p2 = jnp.sum(oh2 * before, axis=0, keepdims=True)
        zero = jnp.zeros((1, tm), F32)
        pos_ref[...] = jnp.concatenate([p1, p2, zero, zero, zero, zero, zero, zero], axis=0).astype(I32)
        carry_ref[...] = carry_ref[...] + jnp.sum(ohs, axis=1, keepdims=True)


def _slots(route, *, tm=512):
    n = route.shape[1]
    tm = min(tm, n)
    const = pl.BlockSpec((N_EXPERTS, LANES), lambda p, i: (0, 0))
    return pl.pallas_call(
        functools.partial(_slot_kernel, tm=tm),
        grid=(2, n // tm),
        in_specs=[pl.BlockSpec((8, tm), lambda p, i: (0, i))],
        out_specs=[pl.BlockSpec((8, tm), lambda p, i: (0, i * p)), const, const],
        out_shape=[jax.ShapeDtypeStruct((8, n), I32), jax.ShapeDtypeStruct((N_EXPERTS, LANES), F32),
                   jax.ShapeDtypeStruct((N_EXPERTS, LANES), F32)],
        scratch_shapes=[pltpu.VMEM((N_EXPERTS, LANES), F32)],
        compiler_params=_params(("arbitrary", "arbitrary")),
        name="moe_slots",
    )(route)


def _experts_kernel(be_ref, br_ref, dst_ref, dstn_ref, x_hbm, wg_ref, wu_ref, wd_ref, y_hbm,
                    xbuf, xb16, wgu16, wd16, acc, sem_in, sem_out, *, nblk, n_tok):
    i = pl.program_id(0)
    c = pl.program_id(1)
    nch = pl.num_programs(1)
    slot = i % 2
    rows = br_ref[i]

    def in_copy(tok, r, s):
        return pltpu.make_async_copy(x_hbm.at[pl.ds(tok, 1), :], xbuf.at[s, pl.ds(r, 1), :], sem_in.at[s])

    def out_copy(dst, r, s):
        return pltpu.make_async_copy(acc.at[s, pl.ds(r, 1), :], y_hbm.at[pl.ds(dst, 1), :], sem_out.at[s])

    def grouped(n_rows, fn, exact):
        def group(gi, carry):
            for u in range(DMA_UNROLL):
                fn(gi * DMA_UNROLL + u)
            return carry

        def single(r, carry):
            fn(r)
            return carry

        if exact:
            full = n_rows // DMA_UNROLL
            lax.fori_loop(0, full, group, 0)
            lax.fori_loop(full * DMA_UNROLL, n_rows, single, 0)
        else:
            lax.fori_loop(0, (n_rows + DMA_UNROLL - 1) // DMA_UNROLL, group, 0)

    def start_gather(tbl, n_rows, s):
        def fn(r):
            v = tbl[0, 0, r]
            in_copy(jnp.where(v >= n_tok, v - n_tok, v), r, s).start()
        grouped(n_rows, fn, False)

    def wait_gather(n_rows, s):
        grouped(n_rows, lambda r: in_copy(0, r, s).wait(), False)

    def start_scatter(n_rows, s):
        grouped(n_rows, lambda r: out_copy(dst_ref[0, 0, r], r, s).start(), True)

    def wait_scatter(n_rows, s):
        grouped(n_rows, lambda r: out_copy(0, r, s).wait(), True)

    @pl.when((i == 0) & (c == 0))
    def _():
        xbuf[...] = jnp.zeros(xbuf.shape, F32)
        start_gather(dst_ref, rows, 0)

    @pl.when(c == 0)
    def _():
        wait_gather(rows, slot)
        xb16[...] = xbuf[slot].astype(BF16)
        acc[slot] = jnp.zeros(acc.shape[1:], F32)

    @pl.when((c == 1) & (i + 1 < nblk))
    def _():
        start_gather(dstn_ref, br_ref[jnp.minimum(i + 1, nblk - 1)], 1 - slot)

    @pl.when(rows > 0)
    def _():
        wgu16[:, :DE_CHUNK] = wg_ref[0, 0].astype(BF16)
        wgu16[:, DE_CHUNK:] = wu_ref[0, 0].astype(BF16)
        wd16[...] = wd_ref[0, 0].astype(BF16)

    def sub_blocks(n_sub):
        rss = [slice(sb * MOE_SUB_ROWS, (sb + 1) * MOE_SUB_ROWS) for sb in range(n_sub)]
        hgus = [_dot(xb16[rs, :], wgu16[...]) for rs in rss]
        hs = [(jax.nn.silu(hgu[:, :DE_CHUNK]) * hgu[:, DE_CHUNK:]).astype(BF16) for hgu in hgus]
        for rs, h in zip(rss, hs):
            acc[slot, rs, :] += _dot(h, wd16[...])

    n_sub_max = MOE_ROWS // MOE_SUB_ROWS
    for n_sub in range(1, n_sub_max + 1):
        lo = (n_sub - 1) * MOE_SUB_ROWS
        cond = (rows > lo) if n_sub == n_sub_max else ((rows > lo) & (rows <= lo + MOE_SUB_ROWS))
        pl.when(cond)(functools.partial(sub_blocks, n_sub))

    @pl.when(c == nch - 1)
    def _():
        @pl.when(i > 0)
        def _():
            wait_scatter(br_ref[jnp.maximum(i - 1, 0)], 1 - slot)

        start_scatter(rows, slot)

        @pl.when(i == nblk - 1)
        def _():
            wait_scatter(rows, slot)


def _experts(xn, dst_row, blk_e, blk_rows, w_gate, w_up, w_down, layer):
    n, d = xn.shape
    nblk = blk_e.shape[0]
    nch = D_EXPERT // DE_CHUNK
    dst3 = dst_row.reshape(nblk, 1, MOE_ROWS)

    def chunk_of(i, c, br):
        return jnp.where(br[i] > 0, c, nch - 1)

    grid_spec = pltpu.PrefetchScalarGridSpec(
        num_scalar_prefetch=2,
        grid=(nblk, nch),
        in_specs=[
            pl.BlockSpec((1, 1, MOE_ROWS), lambda i, c, be, br: (i, 0, 0), memory_space=pltpu.SMEM),
            pl.BlockSpec((1, 1, MOE_ROWS), lambda i, c, be, br: (jnp.minimum(i + 1, nblk - 1), 0, 0),
                         memory_space=pltpu.SMEM),
            pl.BlockSpec(memory_space=pl.ANY),
            pl.BlockSpec((1, 1, d, DE_CHUNK), lambda i, c, be, br: (layer, be[i], 0, chunk_of(i, c, br))),
            pl.BlockSpec((1, 1, d, DE_CHUNK), lambda i, c, be, br: (layer, be[i], 0, chunk_of(i, c, br))),
            pl.BlockSpec((1, 1, DE_CHUNK, d), lambda i, c, be, br: (layer, be[i], chunk_of(i, c, br), 0)),
        ],
        out_specs=pl.BlockSpec(memory_space=pl.ANY),
        scratch_shapes=[pltpu.VMEM((2, MOE_ROWS, d), F32), pltpu.VMEM((MOE_ROWS, d), BF16),
                        pltpu.VMEM((d, 2 * DE_CHUNK), BF16), pltpu.VMEM((DE_CHUNK, d), BF16),
                        pltpu.VMEM((2, MOE_ROWS, d), F32),
                        pltpu.SemaphoreType.DMA((2,)), pltpu.SemaphoreType.DMA((2,))],
    )
    return pl.pallas_call(
        functools.partial(_experts_kernel, nblk=nblk, n_tok=n),
        grid_spec=grid_spec,
        out_shape=jax.ShapeDtypeStruct((TOP_K * n, d), F32),
        compiler_params=_params(("arbitrary", "arbitrary"), 60),
        name="moe_experts",
    )(blk_e, blk_rows, dst3, dst3, xn, w_gate, w_up, w_down)


def _moe_finish_kernel(x_ref, y0_ref, y1_ref, gate_ref, fg_ref, o_ref, *, final_norm):
    g = gate_ref[...]
    y = x_ref[...] + (g[:, 2:3] * y0_ref[...] + g[:, 3:4] * y1_ref[...])
    if final_norm:
        ms = jnp.mean(y * y, axis=-1, keepdims=True)
        y = y * lax.rsqrt(ms + NORM_EPS) * fg_ref[...]
    o_ref[...] = y


def _moe_finish(y, x, gates, final_gain=None, *, tm=256):
    n, d = x.shape
    tm = min(tm, n)
    nt = n // tm
    final_norm = final_gain is not None
    fg = (final_gain if final_norm else jnp.ones((d,), F32)).reshape(1, d).astype(F32)
    return pl.pallas_call(
        functools.partial(_moe_finish_kernel, final_norm=final_norm),
        grid=(nt,),
        in_specs=[pl.BlockSpec((tm, d), lambda i: (i, 0)),
                  pl.BlockSpec((tm, d), lambda i: (i, 0)),
                  pl.BlockSpec((tm, d), lambda i: (nt + i, 0)),
                  pl.BlockSpec((tm, LANES), lambda i: (i, 0)),
                  pl.BlockSpec((1, d), lambda i: (0, 0))],
        out_specs=pl.BlockSpec((tm, d), lambda i: (i, 0)),
        out_shape=jax.ShapeDtypeStruct((n, d), F32),
        compiler_params=_params(("parallel",), 48),
        name="moe_finish",
    )(x, y, y, gates, fg)


def _in_proj_relayout_kernel(w_ref, o_ref):
    off = np.concatenate([[0], np.cumsum(IN_SPLITS)])
    seg = lambda i: w_ref[0, :, int(off[i]):int(off[i + 1])]
    d9 = int(off[9])
    dkv = []
    for j in range(2 * SWA_KV_HEADS):
        piece = w_ref[0, :, d9 + j * SWA_HEAD_DIM:d9 + (j + 1) * SWA_HEAD_DIM]
        dkv += [piece] * (LANES // SWA_HEAD_DIM)
    rows = o_ref.shape[0]
    pad = jnp.zeros((rows, SMALL_COLS - (SMALL_MISC + 64 + 24 + 8)), F32)
    o_ref[...] = jnp.concatenate([seg(0), seg(1), seg(3), seg(8)] + dkv
                                 + [seg(5), seg(6), seg(7), seg(2), seg(4), pad], axis=1).astype(BF16)


def _in_proj_weights(w_in, layer, *, tr=256):
    _, d, cols = w_in.shape
    n_out = MAIN_COLS + SMALL_COLS
    return pl.pallas_call(
        _in_proj_relayout_kernel,
        grid=(d // tr,),
        in_specs=[pl.BlockSpec((1, tr, cols), lambda i: (layer, i, 0))],
        out_specs=pl.BlockSpec((tr, n_out), lambda i: (i, 0)),
        out_shape=jax.ShapeDtypeStruct((d, n_out), BF16),
        compiler_params=_params(("parallel",), 48),
        name="in_proj_relayout",
    )(w_in)


def _to_bf16_kernel(w_ref, o_ref):
    o_ref[...] = w_ref[0].astype(BF16)


def _layer_to_bf16(w, layer, *, tr=512):
    _, d, cols = w.shape
    return pl.pallas_call(
        _to_bf16_kernel,
        grid=(d // tr,),
        in_specs=[pl.BlockSpec((1, tr, cols), lambda i: (layer, i, 0))],
        out_specs=pl.BlockSpec((tr, cols), lambda i: (i, 0)),
        out_shape=jax.ShapeDtypeStruct((d, cols), BF16),
        compiler_params=_params(("parallel",), 48),
        name="weights_to_bf16",
    )(w)


def _mla_weights(w_uq, w_ukv):
    per = MLA_NOPE_DIM + MLA_ROPE_DIM
    w3 = w_uq.reshape(MLA_Q_RANK, MLA_HEADS, per)
    nope = w3[:, :, :MLA_NOPE_DIM].reshape(MLA_Q_RANK, MLA_HEADS * MLA_NOPE_DIM)
    rope = jnp.pad(w3[:, :, MLA_NOPE_DIM:], ((0, 0), (0, 0), (0, LANES - MLA_ROPE_DIM)))
    rope = rope.reshape(MLA_Q_RANK, MLA_HEADS * LANES)
    return jnp.concatenate([nope, rope], axis=1).astype(BF16), w_ukv.astype(BF16)


def _moe_tables(pos, counts, starts, n_tok):
    n_assign = n_tok * TOP_K
    nblk = (n_assign + N_EXPERTS * (MOE_ROWS - 1)) // MOE_ROWS
    cnt = counts[:, 0].astype(I32)
    pstart = starts[:, 0].astype(I32)
    pend = pstart + (cnt + MOE_ROWS - 1) // MOE_ROWS * MOE_ROWS
    dst_row = jnp.zeros((nblk * MOE_ROWS,), I32).at[pos[0:TOP_K].reshape(-1)].set(jnp.arange(n_assign, dtype=I32))
    row0 = jnp.arange(nblk, dtype=I32) * MOE_ROWS
    blk_e = jnp.minimum(jnp.sum((pend[None, :] <= row0[:, None]).astype(I32), axis=1), N_EXPERTS - 1)
    blk_rows = jnp.clip(cnt[blk_e] - (row0 - pstart[blk_e]), 0, MOE_ROWS).astype(I32)
    return dst_row, blk_e, blk_rows


def kernel(x, norm_mix_g, w_in, nsa_kc_pos, nsa_kc_w1, nsa_kc_w2, nsa_vc_pos, nsa_vc_w1, nsa_vc_w2, fox_f_bias,
           mla_q_norm_g, mla_kv_norm_g, mla_w_uq, mla_w_ukv, swa_sinks, out_norm_g, w_out, norm_ffn_g,
           router_group_w, router_group_b, router_expert_w, router_expert_b, exp_w_gate, exp_w_up, exp_w_down,
           final_norm_g):
    bsz, seq, d_model = x.shape
    n = bsz * seq
    depth = w_in.shape[0]
    xs = x.reshape(n, d_model).astype(F32)
    nsa_slopes = _alibi_slopes(NSA_HEADS)
    swa_slopes = _alibi_slopes(SWA_HEADS)
    nsa_nr = NSA_HEADS // NSA_KV_HEADS
    for l in range(depth):
        w_proj = _in_proj_weights(w_in, l)
        main = _norm_matmul(xs, norm_mix_g[l], w_proj, out_dtype=BF16, n=MAIN_COLS, tn=1024, name="in_proj_main")
        small = _norm_matmul(xs, norm_mix_g[l], w_proj, out_dtype=F32, w_col0=MAIN_COLS, n=SMALL_COLS, tn=512,
                             name="in_proj_small")

        pos = jnp.stack([nsa_kc_pos[l], nsa_vc_pos[l]]).astype(F32)
        w1 = jnp.stack([nsa_kc_w1[l], nsa_vc_w1[l]]).astype(BF16)
        w2 = jnp.stack([nsa_kc_w2[l], nsa_vc_w2[l]]).astype(BF16)
        kvc = _compress(main, pos, w1, w2, bsz, seq)
        o_cmp, sel = _nsa_cmp(main, kvc, nsa_slopes, bsz, seq)
        o_slc = _nsa_slc(main, sel, nsa_slopes, bsz, seq)
        o_win = _banded_attn(main, nsa_slopes, None, bsz, seq, ng=NSA_KV_HEADS, nr=nsa_nr, hd=NSA_HEAD_DIM,
                             window=NSA_WINDOW, q_col=MAIN_AQ, k_col=MAIN_AKV + 8 * NSA_HEAD_DIM,
                             v_col=MAIN_AKV + 10 * NSA_HEAD_DIM, name="nsa_win", tq=256)
        out_a = _nsa_gate(o_cmp, o_slc, o_win, small)

        cumt = _fox_prep(small, fox_f_bias[l], bsz, seq)
        out_b = _fox_attn(main, cumt, bsz, seq)

        w_uq, w_ukv = _mla_weights(mla_w_uq[l], mla_w_ukv[l])
        qup = _norm_matmul(small, mla_q_norm_g[l], w_uq, out_dtype=F32, col_off=SMALL_CQ, k=MLA_Q_RANK,
                           name="mla_q_up")
        kvup = _norm_matmul(small, mla_kv_norm_g[l], w_ukv, out_dtype=BF16, col_off=SMALL_CKV, k=MLA_KV_RANK,
                            name="mla_kv_up")
        qrope, krope = _rope(qup, small, bsz, seq)
        out_c = _mla_attn(qup, qrope, kvup, krope, bsz, seq)

        out_d = _banded_attn(main, swa_slopes, swa_sinks[l], bsz, seq, ng=SWA_KV_HEADS,
                             nr=SWA_HEADS // SWA_KV_HEADS, hd=SWA_HEAD_DIM, window=SWA_WINDOW, q_col=MAIN_DQ,
                             k_col=MAIN_DKV, v_col=MAIN_DKV + SWA_KV_HEADS * LANES, name="swa")

        xs = _out_proj((out_a, out_b, out_c, out_d), out_norm_g[l], _layer_to_bf16(w_out, l), xs)

        xn, route, gates = _router(xs, norm_ffn_g[l], router_group_w[l], router_group_b[l],
                                   router_expert_w[l], router_expert_b[l])
        pos_rows, counts, starts = _slots(route)
        dst_row, blk_e, blk_rows = _moe_tables(pos_rows, counts, starts, n)
        y = _experts(xn, dst_row, blk_e, blk_rows, exp_w_gate, exp_w_up, exp_w_down, l)
        xs = _moe_finish(y, xs, gates, final_norm_g if l == depth - 1 else None)
    return xs.reshape(bsz, seq, d_model)
```

```python
import functools
import math

import numpy as np
import jax
import jax.numpy as jnp
from jax import lax
from jax.experimental import pallas as pl
from jax.experimental.pallas import tpu as pltpu

F32 = jnp.float32
BF16 = jnp.bfloat16
I32 = jnp.int32

NEG_INF = -1e30
NORM_EPS = 1e-6
LANES = 128
ROW_CHUNK = 128
SCORE_ROWS = 512
LOG2E = 1.4426950408889634

Q_BLOCK = 128
GROUP_WIDTH = 1024
NSA_HEADS, NSA_KV_HEADS, NSA_HEAD_DIM = 8, 2, 128
NSA_CMP_STRIDE, NSA_CMP_LEN, NSA_SEL_BLOCK, NSA_TOP_N, NSA_WINDOW = 16, 32, 64, 16, 512
FOX_HEADS, FOX_HEAD_DIM = 8, 128
MLA_HEADS, MLA_Q_RANK, MLA_KV_RANK, MLA_NOPE_DIM, MLA_ROPE_DIM, MLA_V_DIM = 8, 768, 512, 128, 64, 128
ROPE_THETA = 10000.0
SWA_HEADS, SWA_KV_HEADS, SWA_HEAD_DIM, SWA_WINDOW = 16, 2, 64, 128
N_GROUPS, EXPERTS_PER_GROUP, TOP_K, D_EXPERT = 8, 8, 2, 384
N_EXPERTS = N_GROUPS * EXPERTS_PER_GROUP
IN_SPLITS = (1024, 1536, 24, 3072, 8, 768, 512, 64, 1024, 256)

MAIN_AQ, MAIN_AKV, MAIN_BQKV, MAIN_DQ, MAIN_DKV, MAIN_COLS = 0, 1024, 2560, 5632, 6656, 7168
SMALL_CQ, SMALL_CKV, SMALL_MISC, SMALL_COLS = 0, 768, 1280, 1536
MISC_GATE_LANE, MISC_FORGET_LANE = 64, 88

MOE_ROWS = 512
MOE_SUB_ROWS = 128
DE_CHUNK = 128
DMA_UNROLL = 8


def _dot(a, b):
    return jnp.dot(a, b, preferred_element_type=F32)


def _dot_nt(a, b):
    return lax.dot_general(a, b, (((1,), (1,)), ((), ())), preferred_element_type=F32)


def _alibi_slopes(n_heads):
    return jnp.exp2(-8.0 * jnp.arange(1, n_heads + 1, dtype=F32) / n_heads)


def _split3(x):
    hi = x.astype(BF16)
    r1 = x - hi.astype(F32)
    mid = r1.astype(BF16)
    lo = (r1 - mid.astype(F32)).astype(BF16)
    return hi, mid, lo


def _params(sem, vmem_mb=None):
    kw = dict(dimension_semantics=sem)
    if vmem_mb is not None:
        kw["vmem_limit_bytes"] = vmem_mb * 1024 * 1024
    return pltpu.CompilerParams(**kw)


def _norm_matmul_kernel(x_ref, g_ref, w_ref, o_ref, xn_ref, *, col_off, k):
    @pl.when(pl.program_id(1) == 0)
    def _():
        x = x_ref[:, col_off:col_off + k].astype(F32)
        ms = jnp.mean(x * x, axis=-1, keepdims=True)
        xn_ref[...] = (x * lax.rsqrt(ms + NORM_EPS) * g_ref[...]).astype(BF16)

    o_ref[...] = _dot(xn_ref[...], w_ref[...]).astype(o_ref.dtype)


def _norm_matmul(x, gain, w, *, out_dtype, col_off=0, k=None, w_col0=0, n=None, tm=512, tn=512,
                 name="norm_matmul"):
    m, kfull = x.shape
    k = kfull if k is None else k
    n = w.shape[1] if n is None else n
    tm = min(tm, m)
    assert m % tm == 0 and n % tn == 0 and w_col0 % tn == 0 and w.shape[0] == k
    wb0 = w_col0 // tn
    return pl.pallas_call(
        functools.partial(_norm_matmul_kernel, col_off=col_off, k=k),
        grid=(m // tm, n // tn),
        in_specs=[
            pl.BlockSpec((tm, kfull), lambda i, j: (i, 0)),
            pl.BlockSpec((1, k), lambda i, j: (0, 0)),
            pl.BlockSpec((k, tn), lambda i, j: (0, wb0 + j)),
        ],
        out_specs=pl.BlockSpec((tm, tn), lambda i, j: (i, j)),
        out_shape=jax.ShapeDtypeStruct((m, n), out_dtype),
        scratch_shapes=[pltpu.VMEM((tm, k), BF16)],
        compiler_params=_params(("parallel", "arbitrary"), 56),
        name=name,
    )(x, gain.reshape(1, k).astype(F32), w)


def _out_proj_kernel(a_ref, b_ref, c_ref, d_ref, g_ref, w_ref, r_ref, o_ref, xn_ref):
    @pl.when(pl.program_id(1) == 0)
    def _():
        for i, ref in enumerate((a_ref, b_ref, c_ref, d_ref)):
            x = ref[...]
            ms = jnp.mean(x * x, axis=-1, keepdims=True)
            g = g_ref[:, i * GROUP_WIDTH:(i + 1) * GROUP_WIDTH]
            xn_ref[:, i * GROUP_WIDTH:(i + 1) * GROUP_WIDTH] = (x * lax.rsqrt(ms + NORM_EPS) * g).astype(BF16)

    o_ref[...] = r_ref[...] + _dot(xn_ref[...], w_ref[...])


def _out_proj(outs, gain, w, resid, *, tm=512, tn=1024):
    m, d = resid.shape
    tm = min(tm, m)
    k = 4 * GROUP_WIDTH
    grp = pl.BlockSpec((tm, GROUP_WIDTH), lambda i, j: (i, 0))
    return pl.pallas_call(
        _out_proj_kernel,
        grid=(m // tm, d // tn),
        in_specs=[grp, grp, grp, grp,
                  pl.BlockSpec((1, k), lambda i, j: (0, 0)),
                  pl.BlockSpec((k, tn), lambda i, j: (0, j)),
                  pl.BlockSpec((tm, tn), lambda i, j: (i, j))],
        out_specs=pl.BlockSpec((tm, tn), lambda i, j: (i, j)),
        out_shape=jax.ShapeDtypeStruct((m, d), F32),
        scratch_shapes=[pltpu.VMEM((tm, k), BF16)],
        compiler_params=_params(("parallel", "arbitrary"), 56),
        name="out_proj",
    )(*outs, gain.reshape(1, k).astype(F32), w, resid)


def _fox_prep_kernel(x_ref, b_ref, cumt_ref, carry_ref, *, ts):
    @pl.when(pl.program_id(1) == 0)
    def _():
        carry_ref[...] = jnp.zeros_like(carry_ref)

    z = x_ref[...] + b_ref[...]
    lf = jnp.minimum(z, 0.0) - jnp.log1p(jnp.exp(-jnp.abs(z)))
    row = lax.broadcasted_iota(I32, (ts, ts), 0)
    col = lax.broadcasted_iota(I32, (ts, ts), 1)
    tri = jnp.where(col <= row, 1.0, 0.0).astype(BF16)
    hi, mid, lo = _split3(lf)
    cum = (_dot(tri, hi) + _dot(tri, mid)) + _dot(tri, lo) + carry_ref[...]
    carry_ref[...] = cum[ts - 1:ts, :]
    cum_t = jnp.concatenate([cum[i * LANES:(i + 1) * LANES, :].T for i in range(ts // LANES)], axis=1)
    cumt_ref[0] = cum_t[MISC_FORGET_LANE:MISC_FORGET_LANE + FOX_HEADS, :]


def _fox_prep(small, f_bias, bsz, seq, *, ts=512):
    ts = min(ts, seq)
    n = bsz * seq
    ns = seq // ts
    bias = jnp.zeros((1, LANES), F32).at[0, MISC_FORGET_LANE:MISC_FORGET_LANE + FOX_HEADS].set(f_bias.astype(F32))
    misc_blk = SMALL_MISC // LANES
    return pl.pallas_call(
        functools.partial(_fox_prep_kernel, ts=ts),
        grid=(bsz, ns),
        in_specs=[pl.BlockSpec((ts, LANES), lambda b, i: (b * ns + i, misc_blk)),
                  pl.BlockSpec((1, LANES), lambda b, i: (0, 0))],
        out_specs=pl.BlockSpec((1, FOX_HEADS, ts), lambda b, i: (b, 0, i)),
        out_shape=jax.ShapeDtypeStruct((bsz, FOX_HEADS, seq), F32),
        scratch_shapes=[pltpu.VMEM((1, LANES), F32)],
        compiler_params=_params(("parallel", "arbitrary")),
        name="fox_prep",
    )(small, bias)


def _with_ones(v):
    return jnp.concatenate([v, jnp.ones((v.shape[0], LANES), v.dtype)], axis=1)


def _flash_rows(s2, v1, m_sc, l_sc, acc_sc, r0, rows):
    sl = slice(r0, r0 + rows)
    dv = v1.shape[1] - LANES
    m_prev = m_sc[sl]
    m_new = jnp.maximum(m_prev, jnp.max(s2, axis=1, keepdims=True))
    alpha = jnp.exp2(m_prev - m_new)
    p = jnp.concatenate([jnp.exp2(s2[:, j * LANES:(j + 1) * LANES] - m_new)
                         for j in range(s2.shape[1] // LANES)], axis=1)
    pv = _dot(p.astype(BF16), v1)
    l_sc[sl] = alpha * l_sc[sl] + pv[:, dv:]
    acc_sc[sl] = alpha * acc_sc[sl] + pv[:, :dv]
    m_sc[sl] = m_new


def _flash_init(m_sc, l_sc, acc_sc):
    m_sc[...] = jnp.full(m_sc.shape, NEG_INF, F32)
    l_sc[...] = jnp.zeros(l_sc.shape, F32)
    acc_sc[...] = jnp.zeros(acc_sc.shape, F32)


def _pipelined_sweep(n_full, issue, softmax_pv, last_tile, sa_sc, sb_sc):
    def full_tile(ki, src, dst):
        issue(ki + 1, dst)
        softmax_pv(ki, src)

    issue(0, sa_sc)

    def pair(j, carry):
        full_tile(2 * j, sa_sc, sb_sc)
        full_tile(2 * j + 1, sb_sc, sa_sc)
        return carry

    lax.fori_loop(0, n_full // 2, pair, 0)

    @pl.when(n_full % 2 == 1)
    def _():
        full_tile(n_full - 1, sa_sc, sb_sc)
        last_tile(sb_sc)

    @pl.when(n_full % 2 == 0)
    def _():
        last_tile(sa_sc)


def _causal_sweep(qi, tq, score_fn, load_v, sa_sc, sb_sc, m_sc, l_sc, acc_sc):
    nch = tq // ROW_CHUNK
    chunk = lambda c: slice(c * ROW_CHUNK, (c + 1) * ROW_CHUNK)
    score_rows = min(SCORE_ROWS, tq)

    def issue(ki, dst):
        for c in range(tq // score_rows):
            rows = slice(c * score_rows, (c + 1) * score_rows)
            dst[rows, :] = score_fn(rows, ki)

    def softmax_pv(ki, src):
        v = _with_ones(load_v(ki, tq))
        for c in range(nch):
            _flash_rows(src[chunk(c), :], v, m_sc, l_sc, acc_sc, c * ROW_CHUNK, ROW_CHUNK)

    def diag_tile(src):
        for c in range(nch):
            ncols = (c + 1) * ROW_CHUNK
            row = lax.broadcasted_iota(I32, (ROW_CHUNK, ncols), 0) + c * ROW_CHUNK
            col = lax.broadcasted_iota(I32, (ROW_CHUNK, ncols), 1)
            s2 = jnp.where(col <= row, src[chunk(c), 0:ncols], NEG_INF)
            _flash_rows(s2, _with_ones(load_v(qi, ncols)), m_sc, l_sc, acc_sc, c * ROW_CHUNK, ROW_CHUNK)

    _pipelined_sweep(qi, issue, softmax_pv, diag_tile, sa_sc, sb_sc)


def _fox_attn_kernel(q_ref, k_ref, v_ref, ck_ref, o_ref, sa_sc, sb_sc, m_sc, l_sc, acc_sc, *, tq, scale):
    qi = pl.program_id(2)
    _flash_init(m_sc, l_sc, acc_sc)
    c0 = ck_ref[0, pl.ds(qi, 1), :][:, 0:1]

    def score(rows, ki):
        k0 = pl.multiple_of(ki * tq, tq)
        k = k_ref[pl.ds(k0, tq), :]
        ck2 = (ck_ref[0, pl.ds(ki, 1), :] - c0) * LOG2E
        return _dot_nt(q_ref[rows, :], k) * (scale * LOG2E) - ck2

    def load_v(ki, ncols):
        return v_ref[pl.ds(pl.multiple_of(ki * tq, tq), ncols), :]

    _causal_sweep(qi, tq, score, load_v, sa_sc, sb_sc, m_sc, l_sc, acc_sc)
    o_ref[...] = acc_sc[...] / l_sc[...]


def _fox_attn(main, cumt, bsz, seq, *, tq=512):
    tq = min(tq, seq)
    nq = seq // tq
    n = bsz * seq
    d = FOX_HEAD_DIM
    qb, kb, vb = MAIN_BQKV // d, MAIN_BQKV // d + FOX_HEADS, MAIN_BQKV // d + 2 * FOX_HEADS
    cumt3 = cumt.reshape(bsz * FOX_HEADS, nq, tq)
    return pl.pallas_call(
        functools.partial(_fox_attn_kernel, tq=tq, scale=d ** -0.5),
        grid=(bsz, FOX_HEADS, nq),
        in_specs=[pl.BlockSpec((tq, d), lambda b, h, i: (b * nq + i, qb + h)),
                  pl.BlockSpec((seq, d), lambda b, h, i: (b, kb + h)),
                  pl.BlockSpec((seq, d), lambda b, h, i: (b, vb + h)),
                  pl.BlockSpec((1, nq, tq), lambda b, h, i: (b * FOX_HEADS + h, 0, 0))],
        out_specs=pl.BlockSpec((tq, d), lambda b, h, i: (b * nq + i, h)),
        out_shape=jax.ShapeDtypeStruct((n, FOX_HEADS * d), F32),
        scratch_shapes=[pltpu.VMEM((tq, tq), F32), pltpu.VMEM((tq, tq), F32),
                        pltpu.VMEM((tq, LANES), F32), pltpu.VMEM((tq, LANES), F32), pltpu.VMEM((tq, d), F32)],
        compiler_params=_params(("parallel", "parallel", "arbitrary")),
        name="fox_attn",
    )(main, main, main, cumt3)


def _mla_attn_kernel(qn_ref, qr_ref, kn_ref, kr_ref, v_ref, o_ref, q_sc, sa_sc, sb_sc, m_sc, l_sc, acc_sc,
                     *, tq, scale):
    qi = pl.program_id(2)
    _flash_init(m_sc, l_sc, acc_sc)
    q_sc[:, :LANES] = qn_ref[...].astype(BF16)
    q_sc[:, LANES:] = qr_ref[...]

    def score(rows, ki):
        k0 = pl.multiple_of(ki * tq, tq)
        k = jnp.concatenate([kn_ref[pl.ds(k0, tq), :], kr_ref[pl.ds(k0, tq), :]], axis=1)
        return _dot_nt(q_sc[rows, :], k) * (scale * LOG2E)

    def load_v(ki, ncols):
        return v_ref[pl.ds(pl.multiple_of(ki * tq, tq), ncols), :]

    _causal_sweep(qi, tq, score, load_v, sa_sc, sb_sc, m_sc, l_sc, acc_sc)
    o_ref[...] = acc_sc[...] / l_sc[...]


def _mla_attn(qup, qrope, kvup, krope, bsz, seq, *, tq=512):
    tq = min(tq, seq)
    nq = seq // tq
    n = bsz * seq
    d = LANES
    return pl.pallas_call(
        functools.partial(_mla_attn_kernel, tq=tq, scale=(MLA_NOPE_DIM + MLA_ROPE_DIM) ** -0.5),
        grid=(bsz, MLA_HEADS, nq),
        in_specs=[pl.BlockSpec((tq, d), lambda b, h, i: (b * nq + i, h)),
                  pl.BlockSpec((tq, d), lambda b, h, i: (b * nq + i, h)),
                  pl.BlockSpec((seq, d), lambda b, h, i: (b, 2 * h)),
                  pl.BlockSpec((seq, d), lambda b, h, i: (b, 0)),
                  pl.BlockSpec((seq, d), lambda b, h, i: (b, 2 * h + 1))],
        out_specs=pl.BlockSpec((tq, d), lambda b, h, i: (b * nq + i, h)),
        out_shape=jax.ShapeDtypeStruct((n, MLA_HEADS * MLA_V_DIM), F32),
        scratch_shapes=[pltpu.VMEM((tq, 2 * LANES), BF16),
                        pltpu.VMEM((tq, tq), F32), pltpu.VMEM((tq, tq), F32),
                        pltpu.VMEM((tq, LANES), F32), pltpu.VMEM((tq, LANES), F32), pltpu.VMEM((tq, d), F32)],
        compiler_params=_params(("parallel", "parallel", "arbitrary")),
        name="mla_attn",
    )(qup, qrope, kvup, krope, kvup)


def _rope_kernel(q_ref, k_ref, cos_ref, sin_ref, qo_ref, ko_ref):
    half = MLA_ROPE_DIM // 2

    def rope(x):
        lane = lax.broadcasted_iota(I32, x.shape, 1)
        first = (lane % MLA_ROPE_DIM) < half
        swapped = jnp.where(first, pltpu.roll(x, LANES - half, 1), pltpu.roll(x, half, 1))
        return x * cos_ref[...] + swapped * sin_ref[...]

    for t in range(q_ref.shape[1] // LANES):
        qo_ref[:, t * LANES:(t + 1) * LANES] = rope(q_ref[:, t * LANES:(t + 1) * LANES]).astype(BF16)
    k = rope(k_ref[...])
    lane = lax.broadcasted_iota(I32, k.shape, 1)
    ko_ref[...] = jnp.where(lane < MLA_ROPE_DIM, k, 0.0).astype(BF16)


def _rope(qup, small, bsz, seq, *, ts=512):
    ts = min(ts, seq)
    ns = seq // ts
    n = bsz * seq
    pos = jnp.arange(seq, dtype=F32)
    inv = ROPE_THETA ** (-jnp.arange(0, MLA_ROPE_DIM, 2, dtype=F32) / MLA_ROPE_DIM)
    ang = pos[:, None] * inv[None, :]
    cos, sin = jnp.cos(ang), jnp.sin(ang)
    cos_t = jnp.concatenate([cos, cos, cos, cos], axis=1)
    sin_t = jnp.concatenate([-sin, sin, -sin, sin], axis=1)
    w = MLA_HEADS * LANES
    return pl.pallas_call(
        _rope_kernel,
        grid=(bsz, ns),
        in_specs=[pl.BlockSpec((ts, w), lambda b, i: (b * ns + i, 1)),
                  pl.BlockSpec((ts, LANES), lambda b, i: (b * ns + i, SMALL_MISC // LANES)),
                  pl.BlockSpec((ts, LANES), lambda b, i: (i, 0)),
                  pl.BlockSpec((ts, LANES), lambda b, i: (i, 0))],
        out_specs=[pl.BlockSpec((ts, w), lambda b, i: (b * ns + i, 0)),
                   pl.BlockSpec((ts, LANES), lambda b, i: (b * ns + i, 0))],
        out_shape=[jax.ShapeDtypeStruct((n, w), BF16), jax.ShapeDtypeStruct((n, LANES), BF16)],
        compiler_params=_params(("parallel", "parallel")),
        name="mla_rope",
    )(qup, small, cos_t, sin_t)


def _banded_kernel(*refs, tq, nr, hd, window, slab, scale, has_sinks, seq):
    if has_sinks:
        slopes_ref, sinks_ref, q_ref, k_ref, v_ref, o_ref = refs
    else:
        slopes_ref, q_ref, k_ref, v_ref, o_ref = refs
    g = pl.program_id(1)
    qi = pl.program_id(2)
    q0 = qi * tq
    start = pl.multiple_of(jnp.minimum(jnp.maximum(q0 + tq - slab, 0), seq - slab), tq)
    kt = k_ref[pl.ds(start, slab), :]
    v1 = _with_ones(v_ref[pl.ds(start, slab), :])
    q = q_ref[...]
    low = lax.broadcasted_iota(I32, (tq, LANES), 1) < hd
    if hd == LANES:
        qs = jnp.concatenate([q[:, r * hd:(r + 1) * hd] for r in range(nr)], axis=0)
    else:
        zero = jnp.zeros((), q.dtype)
        parts = []
        for j in range(nr // 2):
            pair = q[:, j * LANES:(j + 1) * LANES]
            parts += [jnp.where(low, pair, zero), jnp.where(low, zero, pair)]
        qs = jnp.concatenate(parts, axis=0)
    s = _dot_nt(qs, kt) * (scale * LOG2E)
    qpos = q0 + lax.broadcasted_iota(I32, (tq, slab), 0)
    kpos = start + lax.broadcasted_iota(I32, (tq, slab), 1)
    dist = qpos - kpos
    valid = (dist >= 0) & (dist < window)
    krel = (start - q0 + lax.broadcasted_iota(I32, (1, slab), 1)).astype(F32)
    qrel = lax.broadcasted_iota(I32, (tq, 1), 0).astype(F32)
    es, sinks_e = [], []
    for r in range(nr):
        slope2 = slopes_ref[g * nr + r] * LOG2E
        sr = jnp.where(valid, s[r * tq:(r + 1) * tq] + slope2 * krel, NEG_INF)
        m = jnp.max(sr, axis=1, keepdims=True)
        if has_sinks:
            sk = sinks_ref[g * nr + r] * LOG2E + slope2 * qrel
            m = jnp.maximum(m, sk)
            sinks_e.append(jnp.exp2(sk - m))
        es.append(jnp.exp2(sr - m).astype(BF16))
    pv = _dot(jnp.concatenate(es, axis=0), v1)
    den = pv[:, LANES:]
    if has_sinks:
        den = den + jnp.concatenate(sinks_e, axis=0)
    o = pv[:, :LANES] / den
    if hd == LANES:
        o_ref[...] = jnp.concatenate([o[r * tq:(r + 1) * tq] for r in range(nr)], axis=1)
    else:
        o_ref[...] = jnp.concatenate(
            [jnp.where(low, o[2 * j * tq:(2 * j + 1) * tq], o[(2 * j + 1) * tq:(2 * j + 2) * tq])
             for j in range(nr // 2)], axis=1)


def _banded_attn(main, slopes, sinks, bsz, seq, *, ng, nr, hd, window, q_col, k_col, v_col, name, tq=128):
    tq = min(tq, seq)
    nq = seq // tq
    n = bsz * seq
    slab = min(-(-(window - 1) // tq) * tq + tq, seq)
    qw = nr * hd
    has_sinks = sinks is not None
    smem = pl.BlockSpec(memory_space=pltpu.SMEM)
    in_specs = [smem] + ([smem] if has_sinks else []) + [
        pl.BlockSpec((tq, qw), lambda b, g, i: (b * nq + i, q_col // qw + g)),
        pl.BlockSpec((seq, LANES), lambda b, g, i: (b, k_col // LANES + g)),
        pl.BlockSpec((seq, LANES), lambda b, g, i: (b, v_col // LANES + g))]
    args = [slopes] + ([sinks.astype(F32)] if has_sinks else []) + [main, main, main]
    return pl.pallas_call(
        functools.partial(_banded_kernel, tq=tq, nr=nr, hd=hd, window=window, slab=slab,
                          scale=hd ** -0.5, has_sinks=has_sinks, seq=seq),
        grid=(bsz, ng, nq),
        in_specs=in_specs,
        out_specs=pl.BlockSpec((tq, qw), lambda b, g, i: (b * nq + i, g)),
        out_shape=jax.ShapeDtypeStruct((n, ng * qw), F32),
        compiler_params=_params(("parallel", "parallel", "parallel")),
        name=name,
    )(*args)


def _compress_kernel(x_ref, pos_ref, w1_ref, w2_ref, o_ref, xf_ref, *, seq):
    nc = seq // NSA_CMP_STRIDE
    hd = NSA_HEAD_DIM
    xf_ref[0:seq, :] = x_ref[...].astype(F32)
    xf_ref[seq:seq + NSA_CMP_STRIDE, :] = jnp.zeros((NSA_CMP_STRIDE, hd), F32)
    acc = jnp.zeros((nc, hd), F32)
    for j in range(NSA_CMP_LEN):
        rows = xf_ref[pl.ds(j, nc, stride=NSA_CMP_STRIDE), :] + pos_ref[0, j:j + 1, :]
        acc = acc + _dot(rows.astype(BF16), w1_ref[0, j * hd:(j + 1) * hd, :])
    hid = jax.nn.gelu(acc)
    o_ref[0, 0, 0] = _dot(hid.astype(BF16), w2_ref[0]).astype(BF16)


def _compress(main, pos, w1, w2, bsz, seq):
    nc = seq // NSA_CMP_STRIDE
    hd = NSA_HEAD_DIM
    ng = NSA_KV_HEADS
    col0 = MAIN_AKV // hd
    return pl.pallas_call(
        functools.partial(_compress_kernel, seq=seq),
        grid=(bsz, ng, 2),
        in_specs=[pl.BlockSpec((seq, hd), lambda b, g, t: (b, col0 + t * ng + g)),
                  pl.BlockSpec((1, NSA_CMP_LEN, hd), lambda b, g, t: (t, 0, 0)),
                  pl.BlockSpec((1, NSA_CMP_LEN * hd, hd), lambda b, g, t: (t, 0, 0)),
                  pl.BlockSpec((1, hd, hd), lambda b, g, t: (t, 0, 0))],
        out_specs=pl.BlockSpec((1, 1, 1, nc, hd), lambda b, g, t: (b, g, t, 0, 0)),
        out_shape=jax.ShapeDtypeStruct((bsz, ng, 2, nc, hd), BF16),
        scratch_shapes=[pltpu.VMEM((seq + NSA_CMP_STRIDE, hd), F32)],
        compiler_params=_params(("parallel", "parallel", "parallel")),
        name="nsa_compress",
    )(main, pos, w1, w2)


def _nsa_cmp_kernel(slopes_ref, q_ref, kc_ref, vc_ref, o_ref, sel_ref, *, tq, nc, nr):
    g = pl.program_id(1)
    qi = pl.program_id(2)
    hd = NSA_HEAD_DIM
    q0 = qi * tq
    q = q_ref[...]
    qs = jnp.concatenate([q[:, r * hd:(r + 1) * hd] for r in range(nr)], axis=0)
    kc = kc_ref[0, 0, 0]
    vc = vc_ref[0, 0, 0]
    s = _dot_nt(qs, kc) * (hd ** -0.5)
    tpos = q0 + lax.broadcasted_iota(I32, (tq, nc), 0)
    cend = lax.broadcasted_iota(I32, (tq, nc), 1) * NSA_CMP_STRIDE + (NSA_CMP_LEN - 1)
    dist = tpos - cend
    valid = dist >= 0
    distf = dist.astype(F32)
    ps = []
    psum = jnp.zeros((tq, nc), F32)
    for r in range(nr):
        sr = s[r * tq:(r + 1) * tq] - slopes_ref[g * nr + r] * distf
        sr = jnp.where(valid, sr, NEG_INF)
        m = jnp.max(sr, axis=1, keepdims=True)
        e = jnp.exp(sr - m)
        p = jnp.where(valid, e / jnp.sum(e, axis=1, keepdims=True), 0.0)
        psum = psum + p
        ps.append(p.astype(BF16))
    o = _dot(jnp.concatenate(ps, axis=0), vc)
    o_ref[...] = jnp.concatenate([o[r * tq:(r + 1) * tq] for r in range(nr)], axis=1)

    nb = LANES
    n_slc = nc * NSA_CMP_STRIDE // NSA_SEL_BLOCK
    per = NSA_SEL_BLOCK // NSA_CMP_STRIDE
    blk = lax.broadcasted_iota(I32, (nb, nc), 0)
    cidx = lax.broadcasted_iota(I32, (nb, nc), 1)
    overlap = (cidx <= per * blk + per - 1) & (cidx >= per * blk - 1) & (cidx < nc - 1) & (blk < n_slc)
    ov = jnp.where(overlap, 1.0, 0.0).astype(BF16)
    hi, mid, lo = _split3(psum)
    imp = (_dot_nt(ov, hi) + _dot_nt(ov, mid)) + _dot_nt(ov, lo)
    j = lax.broadcasted_iota(I32, (nb, tq), 0)
    cur = (q0 + lax.broadcasted_iota(I32, (nb, tq), 1)) // NSA_SEL_BLOCK
    forced = (j == 0) | (j == cur) | (j == cur - 1)
    imp = jnp.where(forced, 1e6, imp)
    imp = jnp.where(j > cur, -1e6, imp)
    imp = jnp.where(j >= n_slc, -3e38, imp)
    rank = jnp.zeros((nb, tq), F32)
    for i in range(n_slc):
        vi = imp[i:i + 1, :]
        ahead = (vi > imp) | ((vi == imp) & (j > i))
        rank = rank + jnp.where(ahead, 1.0, 0.0)
    sel = jnp.where(rank < float(min(NSA_TOP_N, n_slc)), 0.0, NEG_INF)
    sel_ref[...] = sel.T.astype(BF16)


def _nsa_cmp(main, kvc, slopes, bsz, seq, *, tq=128):
    tq = min(tq, seq)
    nq = seq // tq
    n = bsz * seq
    ng, nr, hd = NSA_KV_HEADS, NSA_HEADS // NSA_KV_HEADS, NSA_HEAD_DIM
    nc = seq // NSA_CMP_STRIDE
    qw = nr * hd
    smem = pl.BlockSpec(memory_space=pltpu.SMEM)
    return pl.pallas_call(
        functools.partial(_nsa_cmp_kernel, tq=tq, nc=nc, nr=nr),
        grid=(bsz, ng, nq),
        in_specs=[smem,
                  pl.BlockSpec((tq, qw), lambda b, g, i: (b * nq + i, g)),
                  pl.BlockSpec((1, 1, 1, nc, hd), lambda b, g, i: (b, g, 0, 0, 0)),
                  pl.BlockSpec((1, 1, 1, nc, hd), lambda b, g, i: (b, g, 1, 0, 0))],
        out_specs=[pl.BlockSpec((tq, qw), lambda b, g, i: (b * nq + i, g)),
                   pl.BlockSpec((tq, LANES), lambda b, g, i: ((b * ng + g) * nq + i, 0))],
        out_shape=[jax.ShapeDtypeStruct((n, ng * qw), F32),
                   jax.ShapeDtypeStruct((bsz * ng * seq, LANES), BF16)],
        compiler_params=_params(("parallel", "parallel", "parallel")),
        name="nsa_cmp",
    )(slopes, main, kvc, kvc)


def _nsa_slc_kernel(slopes_ref, q_ref, k_ref, v_ref, sel_ref, o_ref, sa_sc, sb_sc, m_sc, l_sc, acc_sc,
                    *, tq, tk, nr):
    g = pl.program_id(1)
    qi = pl.program_id(2)
    hd = NSA_HEAD_DIM
    q0 = qi * tq
    _flash_init(m_sc, l_sc, acc_sc)
    last = q0 // tk
    c2 = (hd ** -0.5) * LOG2E
    head = lambda r: slice(r * tq, (r + 1) * tq)

    shift = int(math.log2(NSA_SEL_BLOCK))
    lane_minus_blk = (lax.broadcasted_iota(I32, (tk, LANES), 1)
                      - lax.shift_right_logical(lax.broadcasted_iota(I32, (tk, LANES), 0), shift))

    def issue(ki, dst):
        k0 = pl.multiple_of(ki * tk, tk)
        onehot = jnp.where(lane_minus_blk == ki * (tk // NSA_SEL_BLOCK), 1.0, 0.0).astype(BF16)
        ka = jnp.concatenate([k_ref[pl.ds(k0, tk), :], onehot], axis=1)
        rel = (k0 - q0 + lax.broadcasted_iota(I32, (1, tk), 1)).astype(F32)
        for r in range(nr):
            qa = jnp.concatenate([q_ref[:, r * hd:(r + 1) * hd], sel_ref[...]], axis=1)
            dst[head(r), :] = _dot_nt(qa, ka) * c2 + (slopes_ref[g * nr + r] * LOG2E) * rel

    def load_v(ki):
        return _with_ones(v_ref[pl.ds(pl.multiple_of(ki * tk, tk), tk), :])

    rc = min(ROW_CHUNK, tq)
    chains = [(r * tq + c * rc, c * rc) for r in range(nr) for c in range(tq // rc)]

    def softmax_pv(ki, src):
        v = load_v(ki)
        for row0, _ in chains:
            _flash_rows(src[row0:row0 + rc, :], v, m_sc, l_sc, acc_sc, row0, rc)

    def last_tile(src):
        v = load_v(last)
        k0 = last * tk
        ahead = (k0 + lax.broadcasted_iota(I32, (tq, tk), 1)) > (q0 + lax.broadcasted_iota(I32, (tq, tk), 0))
        for row0, qrow in chains:
            s2 = jnp.where(ahead[qrow:qrow + rc], NEG_INF, src[row0:row0 + rc, :])
            _flash_rows(s2, v, m_sc, l_sc, acc_sc, row0, rc)

    _pipelined_sweep(last, issue, softmax_pv, last_tile, sa_sc, sb_sc)
    o = acc_sc[...] / l_sc[...]
    o_ref[...] = jnp.concatenate([o[r * tq:(r + 1) * tq] for r in range(nr)], axis=1)


def _nsa_slc(main, sel, slopes, bsz, seq, *, tq=256, tk=512):
    tq = min(tq, seq)
    tk = min(tk, seq)
    nq = seq // tq
    n = bsz * seq
    ng, nr, hd = NSA_KV_HEADS, NSA_HEADS // NSA_KV_HEADS, NSA_HEAD_DIM
    qw = nr * hd
    kb = MAIN_AKV // hd + 2 * ng
    vb = MAIN_AKV // hd + 3 * ng
    smem = pl.BlockSpec(memory_space=pltpu.SMEM)
    return pl.pallas_call(
        functools.partial(_nsa_slc_kernel, tq=tq, tk=tk, nr=nr),
        grid=(bsz, ng, nq),
        in_specs=[smem,
                  pl.BlockSpec((tq, qw), lambda b, g, i: (b * nq + i, g)),
                  pl.BlockSpec((seq, hd), lambda b, g, i: (b, kb + g)),
                  pl.BlockSpec((seq, hd), lambda b, g, i: (b, vb + g)),
                  pl.BlockSpec((tq, LANES), lambda b, g, i: ((b * ng + g) * nq + i, 0))],
        out_specs=pl.BlockSpec((tq, qw), lambda b, g, i: (b * nq + i, g)),
        out_shape=jax.ShapeDtypeStruct((n, ng * qw), F32),
        scratch_shapes=[pltpu.VMEM((nr * tq, tk), F32), pltpu.VMEM((nr * tq, tk), F32),
                        pltpu.VMEM((nr * tq, LANES), F32), pltpu.VMEM((nr * tq, LANES), F32),
                        pltpu.VMEM((nr * tq, hd), F32)],
        compiler_params=_params(("parallel", "parallel", "arbitrary")),
        name="nsa_slc",
    )(slopes, main, main, main, sel)


def _nsa_gate_kernel(c_ref, s_ref, w_ref, g_ref, o_ref):
    gates = jax.nn.sigmoid(g_ref[...])
    hd = NSA_HEAD_DIM
    for h in range(NSA_HEADS):
        sl = slice(h * hd, (h + 1) * hd)
        lane = MISC_GATE_LANE + h
        o_ref[:, sl] = (gates[:, lane:lane + 1] * c_ref[:, sl]
                        + gates[:, lane + NSA_HEADS:lane + NSA_HEADS + 1] * s_ref[:, sl]
                        + gates[:, lane + 2 * NSA_HEADS:lane + 2 * NSA_HEADS + 1] * w_ref[:, sl])


def _nsa_gate(o_cmp, o_slc, o_win, small, *, tm=512):
    n, w = o_cmp.shape
    tm = min(tm, n)
    row = pl.BlockSpec((tm, w), lambda i: (i, 0))
    return pl.pallas_call(
        _nsa_gate_kernel,
        grid=(n // tm,),
        in_specs=[row, row, row, pl.BlockSpec((tm, LANES), lambda i: (i, SMALL_MISC // LANES))],
        out_specs=row,
        out_shape=jax.ShapeDtypeStruct((n, w), F32),
        compiler_params=_params(("parallel",)),
        name="nsa_gate",
    )(o_cmp, o_slc, o_win, small)


def _router_kernel(x_ref, g_ref, wh_ref, wm_ref, br_ref, xn_ref, route_ref, gate_ref, *, tm):
    x = x_ref[...]
    ms = jnp.mean(x * x, axis=-1, keepdims=True)
    xn = x * lax.rsqrt(ms + NORM_EPS) * g_ref[...]
    xn_ref[...] = xn
    xh = xn.astype(BF16)
    xm = (xn - xh.astype(F32)).astype(BF16)
    wh = wh_ref[...]
    lg_rows = _dot(xh, wh) + (_dot(xm, wh) + _dot(xh, wm_ref[...]))
    logits = jnp.concatenate([lg_rows[i * LANES:(i + 1) * LANES, :].T for i in range(tm // LANES)],
                             axis=1) + br_ref[:, 0:1]
    ng, ne = N_GROUPS, EXPERTS_PER_GROUP
    lg = logits[0:ng, :]
    sub = lax.broadcasted_iota(I32, (ng, tm), 0)
    mg = jnp.max(lg, axis=0, keepdims=True)
    eg = jnp.exp(lg - mg)
    pg = eg / jnp.sum(eg, axis=0, keepdims=True)
    pg_top = jnp.max(pg, axis=0, keepdims=True)
    g_idx = jnp.min(jnp.where(pg == pg_top, sub, ng), axis=0, keepdims=True)
    le = jnp.zeros((ne, tm), F32)
    for gi in range(ng):
        le = jnp.where(g_idx == gi, logits[ng + gi * ne:ng + (gi + 1) * ne, :], le)
    v1 = jnp.max(le, axis=0, keepdims=True)
    i1 = jnp.min(jnp.where(le == v1, sub, ne), axis=0, keepdims=True)
    rest = jnp.where(sub == i1, -jnp.inf, le)
    v2 = jnp.max(rest, axis=0, keepdims=True)
    i2 = jnp.min(jnp.where(rest == v2, sub, ne), axis=0, keepdims=True)
    e2 = jnp.exp(v2 - v1)
    den = 1.0 + e2
    w1 = pg_top * (1.0 / den)
    w2 = pg_top * (e2 / den)
    ex1 = (g_idx * ne + i1).astype(F32)
    ex2 = (g_idx * ne + i2).astype(F32)
    zero = jnp.zeros((1, tm), F32)
    route = jnp.concatenate([ex1, ex2, w1, w2, zero, zero, zero, zero], axis=0)
    route_ref[...] = route
    pad = jnp.concatenate([route, jnp.zeros((LANES - 8, tm), F32)], axis=0)
    gate_ref[...] = jnp.concatenate([pad[:, i * LANES:(i + 1) * LANES].T for i in range(tm // LANES)], axis=0)


def _router(x, gain, rg_w, rg_b, re_w, re_b, *, tm=256):
    n, d = x.shape
    tm = min(tm, n)
    nl = N_GROUPS + N_EXPERTS
    wr = jnp.pad(jnp.concatenate([rg_w, re_w], axis=1).astype(F32), ((0, 0), (0, LANES - nl)))
    br = jnp.zeros((LANES, LANES), F32).at[:nl, :].set(
        jnp.concatenate([rg_b, re_b]).astype(F32)[:, None] * jnp.ones((1, LANES), F32))
    wr_hi = wr.astype(BF16)
    wr_mid = (wr - wr_hi.astype(F32)).astype(BF16)
    return pl.pallas_call(
        functools.partial(_router_kernel, tm=tm),
        grid=(n // tm,),
        in_specs=[pl.BlockSpec((tm, d), lambda i: (i, 0)),
                  pl.BlockSpec((1, d), lambda i: (0, 0)),
                  pl.BlockSpec((d, LANES), lambda i: (0, 0)),
                  pl.BlockSpec((d, LANES), lambda i: (0, 0)),
                  pl.BlockSpec((LANES, LANES), lambda i: (0, 0))],
        out_specs=[pl.BlockSpec((tm, d), lambda i: (i, 0)),
                   pl.BlockSpec((8, tm), lambda i: (0, i)),
                   pl.BlockSpec((tm, LANES), lambda i: (i, 0))],
        out_shape=[jax.ShapeDtypeStruct((n, d), F32),
                   jax.ShapeDtypeStruct((8, n), F32),
                   jax.ShapeDtypeStruct((n, LANES), F32)],
        compiler_params=_params(("parallel",), 48),
        name="moe_router",
    )(x, gain.reshape(1, d).astype(F32), wr_hi, wr_mid, br)


def _slot_kernel(route_ref, pos_ref, cnt_ref, start_ref, carry_ref, *, tm):
    phase = pl.program_id(0)
    i = pl.program_id(1)
    e1 = route_ref[0:1, :].astype(I32)
    e2 = route_ref[1:2, :].astype(I32)
    sub = lax.broadcasted_iota(I32, (N_EXPERTS, tm), 0)
    oh1 = jnp.where(sub == e1, 1.0, 0.0)
    oh2 = jnp.where(sub == e2, 1.0, 0.0)
    ohs = oh1 + oh2

    @pl.when((phase == 0) & (i == 0))
    def _():
        carry_ref[...] = jnp.zeros_like(carry_ref)

    @pl.when(phase == 0)
    def _():
        carry_ref[...] = carry_ref[...] + jnp.sum(ohs, axis=1, keepdims=True)
        pos_ref[...] = jnp.zeros(pos_ref.shape, I32)

    @pl.when((phase == 1) & (i == 0))
    def _():
        cnt = carry_ref[...]
        cnt_ref[...] = cnt
        padded = jnp.floor((cnt + (MOE_ROWS - 1)) / MOE_ROWS) * MOE_ROWS
        row = lax.broadcasted_iota(I32, (N_EXPERTS, N_EXPERTS), 0)
        col = lax.broadcasted_iota(I32, (N_EXPERTS, N_EXPERTS), 1)
        lower = jnp.where(col < row, 1.0, 0.0).astype(BF16)
        hi, mid, lo = _split3(padded)
        start_ref[...] = (_dot(lower, hi) + _dot(lower, mid)) + _dot(lower, lo)
        carry_ref[...] = jnp.zeros_like(carry_ref)

    @pl.when(phase == 1)
    def _():
        row = lax.broadcasted_iota(I32, (tm, tm), 0)
        col = lax.broadcasted_iota(I32, (tm, tm), 1)
        upper = jnp.where(row < col, 1.0, 0.0).astype(BF16)
        before = _dot(ohs.astype(BF16), upper) + (carry_ref[:, 0:1] + start_ref[:, 0:1])
        p1 = jnp.sum(oh1 * before, axis=0, keepdims=True)
        p2 = jnp.sum(oh2 * before, axis=0, keepdims=True)
        zero = jnp.zeros((1, tm), F32)
        pos_ref[...] = jnp.concatenate([p1, p2, zero, zero, zero, zero, zero, zero], axis=0).astype(I32)
        carry_ref[...] = carry_ref[...] + jnp.sum(ohs, axis=1, keepdims=True)


def _slots(route, *, tm=512):
    n = route.shape[1]
    tm = min(tm, n)
    const = pl.BlockSpec((N_EXPERTS, LANES), lambda p, i: (0, 0))
    return pl.pallas_call(
        functools.partial(_slot_kernel, tm=tm),
        grid=(2, n // tm),
        in_specs=[pl.BlockSpec((8, tm), lambda p, i: (0, i))],
        out_specs=[pl.BlockSpec((8, tm), lambda p, i: (0, i * p)), const, const],
        out_shape=[jax.ShapeDtypeStruct((8, n), I32), jax.ShapeDtypeStruct((N_EXPERTS, LANES), F32),
                   jax.ShapeDtypeStruct((N_EXPERTS, LANES), F32)],
        scratch_shapes=[pltpu.VMEM((N_EXPERTS, LANES), F32)],
        compiler_params=_params(("arbitrary", "arbitrary")),
        name="moe_slots",
    )(route)


def _experts_kernel(be_ref, br_ref, dst_ref, dstn_ref, x_hbm, wg_ref, wu_ref, wd_ref, y_hbm,
                    xbuf, xb16, wgu16, wd16, acc, sem_in, sem_out, *, nblk, n_tok):
    i = pl.program_id(0)
    c = pl.program_id(1)
    nch = pl.num_programs(1)
    slot = i % 2
    rows = br_ref[i]

    def in_copy(tok, r, s):
        return pltpu.make_async_copy(x_hbm.at[pl.ds(tok, 1), :], xbuf.at[s, pl.ds(r, 1), :], sem_in.at[s])

    def out_copy(dst, r, s):
        return pltpu.make_async_copy(acc.at[s, pl.ds(r, 1), :], y_hbm.at[pl.ds(dst, 1), :], sem_out.at[s])

    def grouped(n_rows, fn, exact):
        def group(gi, carry):
            for u in range(DMA_UNROLL):
                fn(gi * DMA_UNROLL + u)
            return carry

        def single(r, carry):
            fn(r)
            return carry

        if exact:
            full = n_rows // DMA_UNROLL
            lax.fori_loop(0, full, group, 0)
            lax.fori_loop(full * DMA_UNROLL, n_rows, single, 0)
        else:
            lax.fori_loop(0, (n_rows + DMA_UNROLL - 1) // DMA_UNROLL, group, 0)

    def start_gather(tbl, n_rows, s):
        def fn(r):
            v = tbl[0, 0, r]
            in_copy(jnp.where(v >= n_tok, v - n_tok, v), r, s).start()
        grouped(n_rows, fn, False)

    def wait_gather(n_rows, s):
        grouped(n_rows, lambda r: in_copy(0, r, s).wait(), False)

    def start_scatter(n_rows, s):
        grouped(n_rows, lambda r: out_copy(dst_ref[0, 0, r], r, s).start(), True)

    def wait_scatter(n_rows, s):
        grouped(n_rows, lambda r: out_copy(0, r, s).wait(), True)

    @pl.when((i == 0) & (c == 0))
    def _():
        xbuf[...] = jnp.zeros(xbuf.shape, F32)
        start_gather(dst_ref, rows, 0)

    @pl.when(c == 0)
    def _():
        wait_gather(rows, slot)
        xb16[...] = xbuf[slot].astype(BF16)
        acc[slot] = jnp.zeros(acc.shape[1:], F32)

    @pl.when((c == 1) & (i + 1 < nblk))
    def _():
        start_gather(dstn_ref, br_ref[jnp.minimum(i + 1, nblk - 1)], 1 - slot)

    @pl.when(rows > 0)
    def _():
        wgu16[:, :DE_CHUNK] = wg_ref[0, 0].astype(BF16)
        wgu16[:, DE_CHUNK:] = wu_ref[0, 0].astype(BF16)
        wd16[...] = wd_ref[0, 0].astype(BF16)

    def sub_blocks(n_sub):
        rss = [slice(sb * MOE_SUB_ROWS, (sb + 1) * MOE_SUB_ROWS) for sb in range(n_sub)]
        hgus = [_dot(xb16[rs, :], wgu16[...]) for rs in rss]
        hs = [(jax.nn.silu(hgu[:, :DE_CHUNK]) * hgu[:, DE_CHUNK:]).astype(BF16) for hgu in hgus]
        for rs, h in zip(rss, hs):
            acc[slot, rs, :] += _dot(h, wd16[...])

    n_sub_max = MOE_ROWS // MOE_SUB_ROWS
    for n_sub in range(1, n_sub_max + 1):
        lo = (n_sub - 1) * MOE_SUB_ROWS
        cond = (rows > lo) if n_sub == n_sub_max else ((rows > lo) & (rows <= lo + MOE_SUB_ROWS))
        pl.when(cond)(functools.partial(sub_blocks, n_sub))

    @pl.when(c == nch - 1)
    def _():
        @pl.when(i > 0)
        def _():
            wait_scatter(br_ref[jnp.maximum(i - 1, 0)], 1 - slot)

        start_scatter(rows, slot)

        @pl.when(i == nblk - 1)
        def _():
            wait_scatter(rows, slot)


def _experts(xn, dst_row, blk_e, blk_rows, w_gate, w_up, w_down, layer):
    n, d = xn.shape
    nblk = blk_e.shape[0]
    nch = D_EXPERT // DE_CHUNK
    dst3 = dst_row.reshape(nblk, 1, MOE_ROWS)

    def chunk_of(i, c, br):
        return jnp.where(br[i] > 0, c, nch - 1)

    grid_spec = pltpu.PrefetchScalarGridSpec(
        num_scalar_prefetch=2,
        grid=(nblk, nch),
        in_specs=[
            pl.BlockSpec((1, 1, MOE_ROWS), lambda i, c, be, br: (i, 0, 0), memory_space=pltpu.SMEM),
            pl.BlockSpec((1, 1, MOE_ROWS), lambda i, c, be, br: (jnp.minimum(i + 1, nblk - 1), 0, 0),
                         memory_space=pltpu.SMEM),
            pl.BlockSpec(memory_space=pl.ANY),
            pl.BlockSpec((1, 1, d, DE_CHUNK), lambda i, c, be, br: (layer, be[i], 0, chunk_of(i, c, br))),
            pl.BlockSpec((1, 1, d, DE_CHUNK), lambda i, c, be, br: (layer, be[i], 0, chunk_of(i, c, br))),
            pl.BlockSpec((1, 1, DE_CHUNK, d), lambda i, c, be, br: (layer, be[i], chunk_of(i, c, br), 0)),
        ],
        out_specs=pl.BlockSpec(memory_space=pl.ANY),
        scratch_shapes=[pltpu.VMEM((2, MOE_ROWS, d), F32), pltpu.VMEM((MOE_ROWS, d), BF16),
                        pltpu.VMEM((d, 2 * DE_CHUNK), BF16), pltpu.VMEM((DE_CHUNK, d), BF16),
                        pltpu.VMEM((2, MOE_ROWS, d), F32),
                        pltpu.SemaphoreType.DMA((2,)), pltpu.SemaphoreType.DMA((2,))],
    )
    return pl.pallas_call(
        functools.partial(_experts_kernel, nblk=nblk, n_tok=n),
        grid_spec=grid_spec,
        out_shape=jax.ShapeDtypeStruct((TOP_K * n, d), F32),
        compiler_params=_params(("arbitrary", "arbitrary"), 60),
        name="moe_experts",
    )(blk_e, blk_rows, dst3, dst3, xn, w_gate, w_up, w_down)


def _moe_finish_kernel(x_ref, y0_ref, y1_ref, gate_ref, fg_ref, o_ref, *, final_norm):
    g = gate_ref[...]
    y = x_ref[...] + (g[:, 2:3] * y0_ref[...] + g[:, 3:4] * y1_ref[...])
    if final_norm:
        ms = jnp.mean(y * y, axis=-1, keepdims=True)
        y = y * lax.rsqrt(ms + NORM_EPS) * fg_ref[...]
    o_ref[...] = y


def _moe_finish(y, x, gates, final_gain=None, *, tm=256):
    n, d = x.shape
    tm = min(tm, n)
    nt = n // tm
    final_norm = final_gain is not None
    fg = (final_gain if final_norm else jnp.ones((d,), F32)).reshape(1, d).astype(F32)
    return pl.pallas_call(
        functools.partial(_moe_finish_kernel, final_norm=final_norm),
        grid=(nt,),
        in_specs=[pl.BlockSpec((tm, d), lambda i: (i, 0)),
                  pl.BlockSpec((tm, d), lambda i: (i, 0)),
                  pl.BlockSpec((tm, d), lambda i: (nt + i, 0)),
                  pl.BlockSpec((tm, LANES), lambda i: (i, 0)),
                  pl.BlockSpec((1, d), lambda i: (0, 0))],
        out_specs=pl.BlockSpec((tm, d), lambda i: (i, 0)),
        out_shape=jax.ShapeDtypeStruct((n, d), F32),
        compiler_params=_params(("parallel",), 48),
        name="moe_finish",
    )(x, y, y, gates, fg)


def _in_proj_relayout_kernel(w_ref, o_ref):
    off = np.concatenate([[0], np.cumsum(IN_SPLITS)])
    seg = lambda i: w_ref[0, :, int(off[i]):int(off[i + 1])]
    d9 = int(off[9])
    dkv = []
    for j in range(2 * SWA_KV_HEADS):
        piece = w_ref[0, :, d9 + j * SWA_HEAD_DIM:d9 + (j + 1) * SWA_HEAD_DIM]
        dkv += [piece] * (LANES // SWA_HEAD_DIM)
    rows = o_ref.shape[0]
    pad = jnp.zeros((rows, SMALL_COLS - (SMALL_MISC + 64 + 24 + 8)), F32)
    o_ref[...] = jnp.concatenate([seg(0), seg(1), seg(3), seg(8)] + dkv
                                 + [seg(5), seg(6), seg(7), seg(2), seg(4), pad], axis=1).astype(BF16)


def _in_proj_weights(w_in, layer, *, tr=256):
    _, d, cols = w_in.shape
    n_out = MAIN_COLS + SMALL_COLS
    return pl.pallas_call(
        _in_proj_relayout_kernel,
        grid=(d // tr,),
        in_specs=[pl.BlockSpec((1, tr, cols), lambda i: (layer, i, 0))],
        out_specs=pl.BlockSpec((tr, n_out), lambda i: (i, 0)),
        out_shape=jax.ShapeDtypeStruct((d, n_out), BF16),
        compiler_params=_params(("parallel",), 48),
        name="in_proj_relayout",
    )(w_in)


def _to_bf16_kernel(w_ref, o_ref):
    o_ref[...] = w_ref[0].astype(BF16)


def _layer_to_bf16(w, layer, *, tr=512):
    _, d, cols = w.shape
    return pl.pallas_call(
        _to_bf16_kernel,
        grid=(d // tr,),
        in_specs=[pl.BlockSpec((1, tr, cols), lambda i: (layer, i, 0))],
        out_specs=pl.BlockSpec((tr, cols), lambda i: (i, 0)),
        out_shape=jax.ShapeDtypeStruct((d, cols), BF16),
        compiler_params=_params(("parallel",), 48),
        name="weights_to_bf16",
    )(w)


def _mla_weights(w_uq, w_ukv):
    per = MLA_NOPE_DIM + MLA_ROPE_DIM
    w3 = w_uq.reshape(MLA_Q_RANK, MLA_HEADS, per)
    nope = w3[:, :, :MLA_NOPE_DIM].reshape(MLA_Q_RANK, MLA_HEADS * MLA_NOPE_DIM)
    rope = jnp.pad(w3[:, :, MLA_NOPE_DIM:], ((0, 0), (0, 0), (0, LANES - MLA_ROPE_DIM)))
    rope = rope.reshape(MLA_Q_RANK, MLA_HEADS * LANES)
    return jnp.concatenate([nope, rope], axis=1).astype(BF16), w_ukv.astype(BF16)


def _moe_tables(pos, counts, starts, n_tok):
    n_assign = n_tok * TOP_K
    nblk = (n_assign + N_EXPERTS * (MOE_ROWS - 1)) // MOE_ROWS
    cnt = counts[:, 0].astype(I32)
    pstart = starts[:, 0].astype(I32)
    pend = pstart + (cnt + MOE_ROWS - 1) // MOE_ROWS * MOE_ROWS
    dst_row = jnp.zeros((nblk * MOE_ROWS,), I32).at[pos[0:TOP_K].reshape(-1)].set(jnp.arange(n_assign, dtype=I32))
    row0 = jnp.arange(nblk, dtype=I32) * MOE_ROWS
    blk_e = jnp.minimum(jnp.sum((pend[None, :] <= row0[:, None]).astype(I32), axis=1), N_EXPERTS - 1)
    blk_rows = jnp.clip(cnt[blk_e] - (row0 - pstart[blk_e]), 0, MOE_ROWS).astype(I32)
    return dst_row, blk_e, blk_rows


def kernel(x, norm_mix_g, w_in, nsa_kc_pos, nsa_kc_w1, nsa_kc_w2, nsa_vc_pos, nsa_vc_w1, nsa_vc_w2, fox_f_bias,
           mla_q_norm_g, mla_kv_norm_g, mla_w_uq, mla_w_ukv, swa_sinks, out_norm_g, w_out, norm_ffn_g,
           router_group_w, router_group_b, router_expert_w, router_expert_b, exp_w_gate, exp_w_up, exp_w_down,
           final_norm_g):
    bsz, seq, d_model = x.shape
    n = bsz * seq
    depth = w_in.shape[0]
    xs = x.reshape(n, d_model).astype(F32)
    nsa_slopes = _alibi_slopes(NSA_HEADS)
    swa_slopes = _alibi_slopes(SWA_HEADS)
    nsa_nr = NSA_HEADS // NSA_KV_HEADS
    for l in range(depth):
        w_proj = _in_proj_weights(w_in, l)
        main = _norm_matmul(xs, norm_mix_g[l], w_proj, out_dtype=BF16, n=MAIN_COLS, tn=1024, name="in_proj_main")
        small = _norm_matmul(xs, norm_mix_g[l], w_proj, out_dtype=F32, w_col0=MAIN_COLS, n=SMALL_COLS, tn=512,
                             name="in_proj_small")

        pos = jnp.stack([nsa_kc_pos[l], nsa_vc_pos[l]]).astype(F32)
        w1 = jnp.stack([nsa_kc_w1[l], nsa_vc_w1[l]]).astype(BF16)
        w2 = jnp.stack([nsa_kc_w2[l], nsa_vc_w2[l]]).astype(BF16)
        kvc = _compress(main, pos, w1, w2, bsz, seq)
        o_cmp, sel = _nsa_cmp(main, kvc, nsa_slopes, bsz, seq)
        o_slc = _nsa_slc(main, sel, nsa_slopes, bsz, seq)
        o_win = _banded_attn(main, nsa_slopes, None, bsz, seq, ng=NSA_KV_HEADS, nr=nsa_nr, hd=NSA_HEAD_DIM,
                             window=NSA_WINDOW, q_col=MAIN_AQ, k_col=MAIN_AKV + 8 * NSA_HEAD_DIM,
                             v_col=MAIN_AKV + 10 * NSA_HEAD_DIM, name="nsa_win", tq=256)
        out_a = _nsa_gate(o_cmp, o_slc, o_win, small)

        cumt = _fox_prep(small, fox_f_bias[l], bsz, seq)
        out_b = _fox_attn(main, cumt, bsz, seq)

        w_uq, w_ukv = _mla_weights(mla_w_uq[l], mla_w_ukv[l])
        qup = _norm_matmul(small, mla_q_norm_g[l], w_uq, out_dtype=F32, col_off=SMALL_CQ, k=MLA_Q_RANK,
                           tm=1024, tn=w_uq.shape[1], name="mla_q_up")
        kvup = _norm_matmul(small, mla_kv_norm_g[l], w_ukv, out_dtype=BF16, col_off=SMALL_CKV, k=MLA_KV_RANK,
                            tm=1024, tn=w_ukv.shape[1], name="mla_kv_up")
        qrope, krope = _rope(qup, small, bsz, seq)
        out_c = _mla_attn(qup, qrope, kvup, krope, bsz, seq)

        out_d = _banded_attn(main, swa_slopes, swa_sinks[l], bsz, seq, ng=SWA_KV_HEADS,
                             nr=SWA_HEADS // SWA_KV_HEADS, hd=SWA_HEAD_DIM, window=SWA_WINDOW, q_col=MAIN_DQ,
                             k_col=MAIN_DKV, v_col=MAIN_DKV + SWA_KV_HEADS * LANES, name="swa")

        xs = _out_proj((out_a, out_b, out_c, out_d), out_norm_g[l], _layer_to_bf16(w_out, l), xs)

        xn, route, gates = _router(xs, norm_ffn_g[l], router_group_w[l], router_group_b[l],
                                   router_expert_w[l], router_expert_b[l])
        pos_rows, counts, starts = _slots(route)
        dst_row, blk_e, blk_rows = _moe_tables(pos_rows, counts, starts, n)
        y = _experts(xn, dst_row, blk_e, blk_rows, exp_w_gate, exp_w_up, exp_w_down, l)
        xs = _moe_finish(y, xs, gates, final_norm_g if l == depth - 1 else None)
    return xs.reshape(bsz, seq, d_model)
```

```python
import functools
import math

import numpy as np
import jax
import jax.numpy as jnp
from jax import lax
from jax.experimental import pallas as pl
from jax.experimental.pallas import tpu as pltpu

F32 = jnp.float32
BF16 = jnp.bfloat16
I32 = jnp.int32

NEG_INF = -1e30
NORM_EPS = 1e-6
LANES = 128
ROW_CHUNK = 128
SCORE_ROWS = 512
LOG2E = 1.4426950408889634

Q_BLOCK = 128
GROUP_WIDTH = 1024
NSA_HEADS, NSA_KV_HEADS, NSA_HEAD_DIM = 8, 2, 128
NSA_CMP_STRIDE, NSA_CMP_LEN, NSA_SEL_BLOCK, NSA_TOP_N, NSA_WINDOW = 16, 32, 64, 16, 512
FOX_HEADS, FOX_HEAD_DIM = 8, 128
MLA_HEADS, MLA_Q_RANK, MLA_KV_RANK, MLA_NOPE_DIM, MLA_ROPE_DIM, MLA_V_DIM = 8, 768, 512, 128, 64, 128
ROPE_THETA = 10000.0
SWA_HEADS, SWA_KV_HEADS, SWA_HEAD_DIM, SWA_WINDOW = 16, 2, 64, 128
N_GROUPS, EXPERTS_PER_GROUP, TOP_K, D_EXPERT = 8, 8, 2, 384
N_EXPERTS = N_GROUPS * EXPERTS_PER_GROUP
IN_SPLITS = (1024, 1536, 24, 3072, 8, 768, 512, 64, 1024, 256)

MAIN_AQ, MAIN_AKV, MAIN_BQKV, MAIN_DQ, MAIN_DKV, MAIN_COLS = 0, 1024, 2560, 5632, 6656, 7168
SMALL_CQ, SMALL_CKV, SMALL_MISC, SMALL_COLS = 0, 768, 1280, 1536
MISC_GATE_LANE, MISC_FORGET_LANE = 64, 88

MOE_ROWS = 512
MOE_SUB_ROWS = 128
DE_CHUNK = 128
DMA_UNROLL = 8


def _dot(a, b):
    return jnp.dot(a, b, preferred_element_type=F32)


def _dot_nt(a, b):
    return lax.dot_general(a, b, (((1,), (1,)), ((), ())), preferred_element_type=F32)


def _alibi_slopes(n_heads):
    return jnp.exp2(-8.0 * jnp.arange(1, n_heads + 1, dtype=F32) / n_heads)


def _split3(x):
    hi = x.astype(BF16)
    r1 = x - hi.astype(F32)
    mid = r1.astype(BF16)
    lo = (r1 - mid.astype(F32)).astype(BF16)
    return hi, mid, lo


def _params(sem, vmem_mb=None):
    kw = dict(dimension_semantics=sem)
    if vmem_mb is not None:
        kw["vmem_limit_bytes"] = vmem_mb * 1024 * 1024
    return pltpu.CompilerParams(**kw)


def _norm_matmul_kernel(x_ref, g_ref, w_ref, o_ref, xn_ref, *, col_off, k):
    @pl.when(pl.program_id(1) == 0)
    def _():
        x = x_ref[:, col_off:col_off + k].astype(F32)
        ms = jnp.mean(x * x, axis=-1, keepdims=True)
        xn_ref[...] = (x * lax.rsqrt(ms + NORM_EPS) * g_ref[...]).astype(BF16)

    o_ref[...] = _dot(xn_ref[...], w_ref[...]).astype(o_ref.dtype)


def _norm_matmul(x, gain, w, *, out_dtype, col_off=0, k=None, w_col0=0, n=None, tm=512, tn=512,
                 name="norm_matmul"):
    m, kfull = x.shape
    k = kfull if k is None else k
    n = w.shape[1] if n is None else n
    tm = min(tm, m)
    assert m % tm == 0 and n % tn == 0 and w_col0 % tn == 0 and w.shape[0] == k
    wb0 = w_col0 // tn
    return pl.pallas_call(
        functools.partial(_norm_matmul_kernel, col_off=col_off, k=k),
        grid=(m // tm, n // tn),
        in_specs=[
            pl.BlockSpec((tm, kfull), lambda i, j: (i, 0)),
            pl.BlockSpec((1, k), lambda i, j: (0, 0)),
            pl.BlockSpec((k, tn), lambda i, j: (0, wb0 + j)),
        ],
        out_specs=pl.BlockSpec((tm, tn), lambda i, j: (i, j)),
        out_shape=jax.ShapeDtypeStruct((m, n), out_dtype),
        scratch_shapes=[pltpu.VMEM((tm, k), BF16)],
        compiler_params=_params(("parallel", "arbitrary"), 56),
        name=name,
    )(x, gain.reshape(1, k).astype(F32), w)


def _out_proj_kernel(a_ref, b_ref, c_ref, d_ref, g_ref, w_ref, r_ref, o_ref, xn_ref):
    @pl.when(pl.program_id(1) == 0)
    def _():
        for i, ref in enumerate((a_ref, b_ref, c_ref, d_ref)):
            x = ref[...]
            ms = jnp.mean(x * x, axis=-1, keepdims=True)
            g = g_ref[:, i * GROUP_WIDTH:(i + 1) * GROUP_WIDTH]
            xn_ref[:, i * GROUP_WIDTH:(i + 1) * GROUP_WIDTH] = (x * lax.rsqrt(ms + NORM_EPS) * g).astype(BF16)

    o_ref[...] = r_ref[...] + _dot(xn_ref[...], w_ref[...])


def _out_proj(outs, gain, w, resid, *, tm=512, tn=1024):
    m, d = resid.shape
    tm = min(tm, m)
    k = 4 * GROUP_WIDTH
    grp = pl.BlockSpec((tm, GROUP_WIDTH), lambda i, j: (i, 0))
    return pl.pallas_call(
        _out_proj_kernel,
        grid=(m // tm, d // tn),
        in_specs=[grp, grp, grp, grp,
                  pl.BlockSpec((1, k), lambda i, j: (0, 0)),
                  pl.BlockSpec((k, tn), lambda i, j: (0, j)),
                  pl.BlockSpec((tm, tn), lambda i, j: (i, j))],
        out_specs=pl.BlockSpec((tm, tn), lambda i, j: (i, j)),
        out_shape=jax.ShapeDtypeStruct((m, d), F32),
        scratch_shapes=[pltpu.VMEM((tm, k), BF16)],
        compiler_params=_params(("parallel", "arbitrary"), 56),
        name="out_proj",
    )(*outs, gain.reshape(1, k).astype(F32), w, resid)


def _fox_prep_kernel(x_ref, b_ref, cumt_ref, carry_ref, *, ts):
    @pl.when(pl.program_id(1) == 0)
    def _():
        carry_ref[...] = jnp.zeros_like(carry_ref)

    z = x_ref[...] + b_ref[...]
    lf = jnp.minimum(z, 0.0) - jnp.log1p(jnp.exp(-jnp.abs(z)))
    row = lax.broadcasted_iota(I32, (ts, ts), 0)
    col = lax.broadcasted_iota(I32, (ts, ts), 1)
    tri = jnp.where(col <= row, 1.0, 0.0).astype(BF16)
    hi, mid, lo = _split3(lf)
    cum = (_dot(tri, hi) + _dot(tri, mid)) + _dot(tri, lo) + carry_ref[...]
    carry_ref[...] = cum[ts - 1:ts, :]
    cum_t = jnp.concatenate([cum[i * LANES:(i + 1) * LANES, :].T for i in range(ts // LANES)], axis=1)
    cumt_ref[0] = cum_t[MISC_FORGET_LANE:MISC_FORGET_LANE + FOX_HEADS, :]


def _fox_prep(small, f_bias, bsz, seq, *, ts=512):
    ts = min(ts, seq)
    n = bsz * seq
    ns = seq // ts
    bias = jnp.zeros((1, LANES), F32).at[0, MISC_FORGET_LANE:MISC_FORGET_LANE + FOX_HEADS].set(f_bias.astype(F32))
    misc_blk = SMALL_MISC // LANES
    return pl.pallas_call(
        functools.partial(_fox_prep_kernel, ts=ts),
        grid=(bsz, ns),
        in_specs=[pl.BlockSpec((ts, LANES), lambda b, i: (b * ns + i, misc_blk)),
                  pl.BlockSpec((1, LANES), lambda b, i: (0, 0))],
        out_specs=pl.BlockSpec((1, FOX_HEADS, ts), lambda b, i: (b, 0, i)),
        out_shape=jax.ShapeDtypeStruct((bsz, FOX_HEADS, seq), F32),
        scratch_shapes=[pltpu.VMEM((1, LANES), F32)],
        compiler_params=_params(("parallel", "arbitrary")),
        name="fox_prep",
    )(small, bias)


def _with_ones(v):
    return jnp.concatenate([v, jnp.ones((v.shape[0], LANES), v.dtype)], axis=1)


def _flash_rows(s2, v1, m_sc, l_sc, acc_sc, r0, rows):
    sl = slice(r0, r0 + rows)
    dv = v1.shape[1] - LANES
    m_prev = m_sc[sl]
    m_new = jnp.maximum(m_prev, jnp.max(s2, axis=1, keepdims=True))
    alpha = jnp.exp2(m_prev - m_new)
    p = jnp.concatenate([jnp.exp2(s2[:, j * LANES:(j + 1) * LANES] - m_new)
                         for j in range(s2.shape[1] // LANES)], axis=1)
    pv = _dot(p.astype(BF16), v1)
    l_sc[sl] = alpha * l_sc[sl] + pv[:, dv:]
    acc_sc[sl] = alpha * acc_sc[sl] + pv[:, :dv]
    m_sc[sl] = m_new


def _flash_init(m_sc, l_sc, acc_sc):
    m_sc[...] = jnp.full(m_sc.shape, NEG_INF, F32)
    l_sc[...] = jnp.zeros(l_sc.shape, F32)
    acc_sc[...] = jnp.zeros(acc_sc.shape, F32)


def _pipelined_sweep(n_full, issue, softmax_pv, last_tile, sa_sc, sb_sc):
    def full_tile(ki, src, dst):
        issue(ki + 1, dst)
        softmax_pv(ki, src)

    issue(0, sa_sc)

    def pair(j, carry):
        full_tile(2 * j, sa_sc, sb_sc)
        full_tile(2 * j + 1, sb_sc, sa_sc)
        return carry

    lax.fori_loop(0, n_full // 2, pair, 0)

    @pl.when(n_full % 2 == 1)
    def _():
        full_tile(n_full - 1, sa_sc, sb_sc)
        last_tile(sb_sc)

    @pl.when(n_full % 2 == 0)
    def _():
        last_tile(sa_sc)


def _causal_sweep(qi, tq, score_fn, load_v, sa_sc, sb_sc, m_sc, l_sc, acc_sc):
    nch = tq // ROW_CHUNK
    chunk = lambda c: slice(c * ROW_CHUNK, (c + 1) * ROW_CHUNK)
    score_rows = min(SCORE_ROWS, tq)

    def issue(ki, dst):
        for c in range(tq // score_rows):
            rows = slice(c * score_rows, (c + 1) * score_rows)
            dst[rows, :] = score_fn(rows, ki)

    def softmax_pv(ki, src):
        v = _with_ones(load_v(ki, tq))
        for c in range(nch):
            _flash_rows(src[chunk(c), :], v, m_sc, l_sc, acc_sc, c * ROW_CHUNK, ROW_CHUNK)

    def diag_tile(src):
        for c in range(nch):
            ncols = (c + 1) * ROW_CHUNK
            row = lax.broadcasted_iota(I32, (ROW_CHUNK, ncols), 0) + c * ROW_CHUNK
            col = lax.broadcasted_iota(I32, (ROW_CHUNK, ncols), 1)
            s2 = jnp.where(col <= row, src[chunk(c), 0:ncols], NEG_INF)
            _flash_rows(s2, _with_ones(load_v(qi, ncols)), m_sc, l_sc, acc_sc, c * ROW_CHUNK, ROW_CHUNK)

    _pipelined_sweep(qi, issue, softmax_pv, diag_tile, sa_sc, sb_sc)


def _fox_attn_kernel(q_ref, k_ref, v_ref, ck_ref, o_ref, sa_sc, sb_sc, m_sc, l_sc, acc_sc, *, tq, scale):
    qi = pl.program_id(2)
    _flash_init(m_sc, l_sc, acc_sc)
    c0 = ck_ref[0, pl.ds(qi, 1), :][:, 0:1]

    def score(rows, ki):
        k0 = pl.multiple_of(ki * tq, tq)
        k = k_ref[pl.ds(k0, tq), :]
        ck2 = (ck_ref[0, pl.ds(ki, 1), :] - c0) * LOG2E
        return _dot_nt(q_ref[rows, :], k) * (scale * LOG2E) - ck2

    def load_v(ki, ncols):
        return v_ref[pl.ds(pl.multiple_of(ki * tq, tq), ncols), :]

    _causal_sweep(qi, tq, score, load_v, sa_sc, sb_sc, m_sc, l_sc, acc_sc)
    o_ref[...] = acc_sc[...] / l_sc[...]


def _fox_attn(main, cumt, bsz, seq, *, tq=512):
    tq = min(tq, seq)
    nq = seq // tq
    n = bsz * seq
    d = FOX_HEAD_DIM
    qb, kb, vb = MAIN_BQKV // d, MAIN_BQKV // d + FOX_HEADS, MAIN_BQKV // d + 2 * FOX_HEADS
    cumt3 = cumt.reshape(bsz * FOX_HEADS, nq, tq)
    return pl.pallas_call(
        functools.partial(_fox_attn_kernel, tq=tq, scale=d ** -0.5),
        grid=(bsz, FOX_HEADS, nq),
        in_specs=[pl.BlockSpec((tq, d), lambda b, h, i: (b * nq + i, qb + h)),
                  pl.BlockSpec((seq, d), lambda b, h, i: (b, kb + h)),
                  pl.BlockSpec((seq, d), lambda b, h, i: (b, vb + h)),
                  pl.BlockSpec((1, nq, tq), lambda b, h, i: (b * FOX_HEADS + h, 0, 0))],
        out_specs=pl.BlockSpec((tq, d), lambda b, h, i: (b * nq + i, h)),
        out_shape=jax.ShapeDtypeStruct((n, FOX_HEADS * d), F32),
        scratch_shapes=[pltpu.VMEM((tq, tq), F32), pltpu.VMEM((tq, tq), F32),
                        pltpu.VMEM((tq, LANES), F32), pltpu.VMEM((tq, LANES), F32), pltpu.VMEM((tq, d), F32)],
        compiler_params=_params(("parallel", "parallel", "arbitrary")),
        name="fox_attn",
    )(main, main, main, cumt3)


def _mla_attn_kernel(qn_ref, qr_ref, kn_ref, kr_ref, v_ref, o_ref, q_sc, sa_sc, sb_sc, m_sc, l_sc, acc_sc,
                     *, tq, scale):
    qi = pl.program_id(2)
    _flash_init(m_sc, l_sc, acc_sc)
    q_sc[:, :LANES] = qn_ref[...].astype(BF16)
    q_sc[:, LANES:] = qr_ref[...]

    def score(rows, ki):
        k0 = pl.multiple_of(ki * tq, tq)
        k = jnp.concatenate([kn_ref[pl.ds(k0, tq), :], kr_ref[pl.ds(k0, tq), :]], axis=1)
        return _dot_nt(q_sc[rows, :], k) * (scale * LOG2E)

    def load_v(ki, ncols):
        return v_ref[pl.ds(pl.multiple_of(ki * tq, tq), ncols), :]

    _causal_sweep(qi, tq, score, load_v, sa_sc, sb_sc, m_sc, l_sc, acc_sc)
    o_ref[...] = acc_sc[...] / l_sc[...]


def _mla_attn(qup, qrope, kvup, krope, bsz, seq, *, tq=512):
    tq = min(tq, seq)
    nq = seq // tq
    n = bsz * seq
    d = LANES
    return pl.pallas_call(
        functools.partial(_mla_attn_kernel, tq=tq, scale=(MLA_NOPE_DIM + MLA_ROPE_DIM) ** -0.5),
        grid=(bsz, MLA_HEADS, nq),
        in_specs=[pl.BlockSpec((tq, d), lambda b, h, i: (b * nq + i, h)),
                  pl.BlockSpec((tq, d), lambda b, h, i: (b * nq + i, h)),
                  pl.BlockSpec((seq, d), lambda b, h, i: (b, 2 * h)),
                  pl.BlockSpec((seq, d), lambda b, h, i: (b, 0)),
                  pl.BlockSpec((seq, d), lambda b, h, i: (b, 2 * h + 1))],
        out_specs=pl.BlockSpec((tq, d), lambda b, h, i: (b * nq + i, h)),
        out_shape=jax.ShapeDtypeStruct((n, MLA_HEADS * MLA_V_DIM), F32),
        scratch_shapes=[pltpu.VMEM((tq, 2 * LANES), BF16),
                        pltpu.VMEM((tq, tq), F32), pltpu.VMEM((tq, tq), F32),
                        pltpu.VMEM((tq, LANES), F32), pltpu.VMEM((tq, LANES), F32), pltpu.VMEM((tq, d), F32)],
        compiler_params=_params(("parallel", "parallel", "arbitrary")),
        name="mla_attn",
    )(qup, qrope, kvup, krope, kvup)


def _rope_kernel(q_ref, k_ref, cos_ref, sin_ref, qo_ref, ko_ref):
    half = MLA_ROPE_DIM // 2

    def rope(x):
        lane = lax.broadcasted_iota(I32, x.shape, 1)
        first = (lane % MLA_ROPE_DIM) < half
        swapped = jnp.where(first, pltpu.roll(x, LANES - half, 1), pltpu.roll(x, half, 1))
        return x * cos_ref[...] + swapped * sin_ref[...]

    for t in range(q_ref.shape[1] // LANES):
        qo_ref[:, t * LANES:(t + 1) * LANES] = rope(q_ref[:, t * LANES:(t + 1) * LANES]).astype(BF16)
    k = rope(k_ref[...])
    lane = lax.broadcasted_iota(I32, k.shape, 1)
    ko_ref[...] = jnp.where(lane < MLA_ROPE_DIM, k, 0.0).astype(BF16)


def _rope(qup, small, bsz, seq, *, ts=512):
    ts = min(ts, seq)
    ns = seq // ts
    n = bsz * seq
    pos = jnp.arange(seq, dtype=F32)
    inv = ROPE_THETA ** (-jnp.arange(0, MLA_ROPE_DIM, 2, dtype=F32) / MLA_ROPE_DIM)
    ang = pos[:, None] * inv[None, :]
    cos, sin = jnp.cos(ang), jnp.sin(ang)
    cos_t = jnp.concatenate([cos, cos, cos, cos], axis=1)
    sin_t = jnp.concatenate([-sin, sin, -sin, sin], axis=1)
    w = MLA_HEADS * LANES
    return pl.pallas_call(
        _rope_kernel,
        grid=(bsz, ns),
        in_specs=[pl.BlockSpec((ts, w), lambda b, i: (b * ns + i, 1)),
                  pl.BlockSpec((ts, LANES), lambda b, i: (b * ns + i, SMALL_MISC // LANES)),
                  pl.BlockSpec((ts, LANES), lambda b, i: (i, 0)),
                  pl.BlockSpec((ts, LANES), lambda b, i: (i, 0))],
        out_specs=[pl.BlockSpec((ts, w), lambda b, i: (b * ns + i, 0)),
                   pl.BlockSpec((ts, LANES), lambda b, i: (b * ns + i, 0))],
        out_shape=[jax.ShapeDtypeStruct((n, w), BF16), jax.ShapeDtypeStruct((n, LANES), BF16)],
        compiler_params=_params(("parallel", "parallel")),
        name="mla_rope",
    )(qup, small, cos_t, sin_t)


def _banded_kernel(*refs, tq, nr, hd, window, slab, scale, has_sinks, seq):
    if has_sinks:
        slopes_ref, sinks_ref, q_ref, k_ref, v_ref, o_ref = refs
    else:
        slopes_ref, q_ref, k_ref, v_ref, o_ref = refs
    g = pl.program_id(1)
    qi = pl.program_id(2)
    q0 = qi * tq
    start = pl.multiple_of(jnp.minimum(jnp.maximum(q0 + tq - slab, 0), seq - slab), tq)
    kt = k_ref[pl.ds(start, slab), :]
    v1 = _with_ones(v_ref[pl.ds(start, slab), :])
    q = q_ref[...]
    low = lax.broadcasted_iota(I32, (tq, LANES), 1) < hd
    if hd == LANES:
        qs = jnp.concatenate([q[:, r * hd:(r + 1) * hd] for r in range(nr)], axis=0)
    else:
        zero = jnp.zeros((), q.dtype)
        parts = []
        for j in range(nr // 2):
            pair = q[:, j * LANES:(j + 1) * LANES]
            parts += [jnp.where(low, pair, zero), jnp.where(low, zero, pair)]
        qs = jnp.concatenate(parts, axis=0)
    s = _dot_nt(qs, kt) * (scale * LOG2E)
    qpos = q0 + lax.broadcasted_iota(I32, (tq, slab), 0)
    kpos = start + lax.broadcasted_iota(I32, (tq, slab), 1)
    dist = qpos - kpos
    valid = (dist >= 0) & (dist < window)
    krel = (start - q0 + lax.broadcasted_iota(I32, (1, slab), 1)).astype(F32)
    qrel = lax.broadcasted_iota(I32, (tq, 1), 0).astype(F32)
    es, sinks_e = [], []
    for r in range(nr):
        slope2 = slopes_ref[g * nr + r] * LOG2E
        sr = jnp.where(valid, s[r * tq:(r + 1) * tq] + slope2 * krel, NEG_INF)
        m = jnp.max(sr, axis=1, keepdims=True)
        if has_sinks:
            sk = sinks_ref[g * nr + r] * LOG2E + slope2 * qrel
            m = jnp.maximum(m, sk)
            sinks_e.append(jnp.exp2(sk - m))
        es.append(jnp.exp2(sr - m).astype(BF16))
    pv = _dot(jnp.concatenate(es, axis=0), v1)
    den = pv[:, LANES:]
    if has_sinks:
        den = den + jnp.concatenate(sinks_e, axis=0)
    o = pv[:, :LANES] / den
    if hd == LANES:
        o_ref[...] = jnp.concatenate([o[r * tq:(r + 1) * tq] for r in range(nr)], axis=1)
    else:
        o_ref[...] = jnp.concatenate(
            [jnp.where(low, o[2 * j * tq:(2 * j + 1) * tq], o[(2 * j + 1) * tq:(2 * j + 2) * tq])
             for j in range(nr // 2)], axis=1)


def _banded_attn(main, slopes, sinks, bsz, seq, *, ng, nr, hd, window, q_col, k_col, v_col, name, tq=128):
    tq = min(tq, seq)
    nq = seq // tq
    n = bsz * seq
    slab = min(-(-(window - 1) // tq) * tq + tq, seq)
    qw = nr * hd
    has_sinks = sinks is not None
    smem = pl.BlockSpec(memory_space=pltpu.SMEM)
    in_specs = [smem] + ([smem] if has_sinks else []) + [
        pl.BlockSpec((tq, qw), lambda b, g, i: (b * nq + i, q_col // qw + g)),
        pl.BlockSpec((seq, LANES), lambda b, g, i: (b, k_col // LANES + g)),
        pl.BlockSpec((seq, LANES), lambda b, g, i: (b, v_col // LANES + g))]
    args = [slopes] + ([sinks.astype(F32)] if has_sinks else []) + [main, main, main]
    return pl.pallas_call(
        functools.partial(_banded_kernel, tq=tq, nr=nr, hd=hd, window=window, slab=slab,
                          scale=hd ** -0.5, has_sinks=has_sinks, seq=seq),
        grid=(bsz, ng, nq),
        in_specs=in_specs,
        out_specs=pl.BlockSpec((tq, qw), lambda b, g, i: (b * nq + i, g)),
        out_shape=jax.ShapeDtypeStruct((n, ng * qw), F32),
        compiler_params=_params(("parallel", "parallel", "parallel")),
        name=name,
    )(*args)


def _compress_kernel(x_ref, pos_ref, w1_ref, w2_ref, o_ref, xf_ref, *, seq):
    nc = seq // NSA_CMP_STRIDE
    hd = NSA_HEAD_DIM
    xf_ref[0:seq, :] = x_ref[...].astype(F32)
    xf_ref[seq:seq + NSA_CMP_STRIDE, :] = jnp.zeros((NSA_CMP_STRIDE, hd), F32)
    acc = jnp.zeros((nc, hd), F32)
    for j in range(NSA_CMP_LEN):
        rows = xf_ref[pl.ds(j, nc, stride=NSA_CMP_STRIDE), :] + pos_ref[0, j:j + 1, :]
        acc = acc + _dot(rows.astype(BF16), w1_ref[0, j * hd:(j + 1) * hd, :])
    hid = jax.nn.gelu(acc)
    o_ref[0, 0, 0] = _dot(hid.astype(BF16), w2_ref[0]).astype(BF16)


def _compress(main, pos, w1, w2, bsz, seq):
    nc = seq // NSA_CMP_STRIDE
    hd = NSA_HEAD_DIM
    ng = NSA_KV_HEADS
    col0 = MAIN_AKV // hd
    return pl.pallas_call(
        functools.partial(_compress_kernel, seq=seq),
        grid=(bsz, ng, 2),
        in_specs=[pl.BlockSpec((seq, hd), lambda b, g, t: (b, col0 + t * ng + g)),
                  pl.BlockSpec((1, NSA_CMP_LEN, hd), lambda b, g, t: (t, 0, 0)),
                  pl.BlockSpec((1, NSA_CMP_LEN * hd, hd), lambda b, g, t: (t, 0, 0)),
                  pl.BlockSpec((1, hd, hd), lambda b, g, t: (t, 0, 0))],
        out_specs=pl.BlockSpec((1, 1, 1, nc, hd), lambda b, g, t: (b, g, t, 0, 0)),
        out_shape=jax.ShapeDtypeStruct((bsz, ng, 2, nc, hd), BF16),
        scratch_shapes=[pltpu.VMEM((seq + NSA_CMP_STRIDE, hd), F32)],
        compiler_params=_params(("parallel", "parallel", "parallel")),
        name="nsa_compress",
    )(main, pos, w1, w2)


def _nsa_cmp_kernel(slopes_ref, q_ref, kc_ref, vc_ref, o_ref, sel_ref, *, tq, nc, nr):
    g = pl.program_id(1)
    qi = pl.program_id(2)
    hd = NSA_HEAD_DIM
    q0 = qi * tq
    q = q_ref[...]
    qs = jnp.concatenate([q[:, r * hd:(r + 1) * hd] for r in range(nr)], axis=0)
    kc = kc_ref[0, 0, 0]
    vc = vc_ref[0, 0, 0]
    s = _dot_nt(qs, kc) * (hd ** -0.5)
    tpos = q0 + lax.broadcasted_iota(I32, (tq, nc), 0)
    cend = lax.broadcasted_iota(I32, (tq, nc), 1) * NSA_CMP_STRIDE + (NSA_CMP_LEN - 1)
    dist = tpos - cend
    valid = dist >= 0
    distf = dist.astype(F32)
    ps = []
    psum = jnp.zeros((tq, nc), F32)
    for r in range(nr):
        sr = s[r * tq:(r + 1) * tq] - slopes_ref[g * nr + r] * distf
        sr = jnp.where(valid, sr, NEG_INF)
        m = jnp.max(sr, axis=1, keepdims=True)
        e = jnp.exp(sr - m)
        p = jnp.where(valid, e / jnp.sum(e, axis=1, keepdims=True), 0.0)
        psum = psum + p
        ps.append(p.astype(BF16))
    o = _dot(jnp.concatenate(ps, axis=0), vc)
    o_ref[...] = jnp.concatenate([o[r * tq:(r + 1) * tq] for r in range(nr)], axis=1)

    nb = LANES
    n_slc = nc * NSA_CMP_STRIDE // NSA_SEL_BLOCK
    per = NSA_SEL_BLOCK // NSA_CMP_STRIDE
    blk = lax.broadcasted_iota(I32, (nb, nc), 0)
    cidx = lax.broadcasted_iota(I32, (nb, nc), 1)
    overlap = (cidx <= per * blk + per - 1) & (cidx >= per * blk - 1) & (cidx < nc - 1) & (blk < n_slc)
    ov = jnp.where(overlap, 1.0, 0.0).astype(BF16)
    hi, mid, lo = _split3(psum)
    imp = (_dot_nt(ov, hi) + _dot_nt(ov, mid)) + _dot_nt(ov, lo)
    j = lax.broadcasted_iota(I32, (nb, tq), 0)
    cur = (q0 + lax.broadcasted_iota(I32, (nb, tq), 1)) // NSA_SEL_BLOCK
    forced = (j == 0) | (j == cur) | (j == cur - 1)
    imp = jnp.where(forced, 1e6, imp)
    imp = jnp.where(j > cur, -1e6, imp)
    imp = jnp.where(j >= n_slc, -3e38, imp)
    rank = jnp.zeros((nb, tq), F32)
    for i in range(n_slc):
        vi = imp[i:i + 1, :]
        ahead = (vi > imp) | ((vi == imp) & (j > i))
        rank = rank + jnp.where(ahead, 1.0, 0.0)
    sel = jnp.where(rank < float(min(NSA_TOP_N, n_slc)), 0.0, NEG_INF)
    sel_ref[...] = sel.T.astype(BF16)


def _nsa_cmp(main, kvc, slopes, bsz, seq, *, tq=128):
    tq = min(tq, seq)
    nq = seq // tq
    n = bsz * seq
    ng, nr, hd = NSA_KV_HEADS, NSA_HEADS // NSA_KV_HEADS, NSA_HEAD_DIM
    nc = seq // NSA_CMP_STRIDE
    qw = nr * hd
    smem = pl.BlockSpec(memory_space=pltpu.SMEM)
    return pl.pallas_call(
        functools.partial(_nsa_cmp_kernel, tq=tq, nc=nc, nr=nr),
        grid=(bsz, ng, nq),
        in_specs=[smem,
                  pl.BlockSpec((tq, qw), lambda b, g, i: (b * nq + i, g)),
                  pl.BlockSpec((1, 1, 1, nc, hd), lambda b, g, i: (b, g, 0, 0, 0)),
                  pl.BlockSpec((1, 1, 1, nc, hd), lambda b, g, i: (b, g, 1, 0, 0))],
        out_specs=[pl.BlockSpec((tq, qw), lambda b, g, i: (b * nq + i, g)),
                   pl.BlockSpec((tq, LANES), lambda b, g, i: ((b * ng + g) * nq + i, 0))],
        out_shape=[jax.ShapeDtypeStruct((n, ng * qw), F32),
                   jax.ShapeDtypeStruct((bsz * ng * seq, LANES), BF16)],
        compiler_params=_params(("parallel", "parallel", "parallel")),
        name="nsa_cmp",
    )(slopes, main, kvc, kvc)


def _nsa_slc_kernel(slopes_ref, q_ref, k_ref, v_ref, sel_ref, o_ref, sa_sc, sb_sc, m_sc, l_sc, acc_sc,
                    *, tq, tk, nr):
    g = pl.program_id(1)
    qi = pl.program_id(2)
    hd = NSA_HEAD_DIM
    q0 = qi * tq
    _flash_init(m_sc, l_sc, acc_sc)
    last = q0 // tk
    c2 = (hd ** -0.5) * LOG2E
    head = lambda r: slice(r * tq, (r + 1) * tq)

    shift = int(math.log2(NSA_SEL_BLOCK))
    lane_minus_blk = (lax.broadcasted_iota(I32, (tk, LANES), 1)
                      - lax.shift_right_logical(lax.broadcasted_iota(I32, (tk, LANES), 0), shift))

    def issue(ki, dst):
        k0 = pl.multiple_of(ki * tk, tk)
        onehot = jnp.where(lane_minus_blk == ki * (tk // NSA_SEL_BLOCK), 1.0, 0.0).astype(BF16)
        ka = jnp.concatenate([k_ref[pl.ds(k0, tk), :], onehot], axis=1)
        rel = (k0 - q0 + lax.broadcasted_iota(I32, (1, tk), 1)).astype(F32)
        for r in range(nr):
            qa = jnp.concatenate([q_ref[:, r * hd:(r + 1) * hd], sel_ref[...]], axis=1)
            dst[head(r), :] = _dot_nt(qa, ka) * c2 + (slopes_ref[g * nr + r] * LOG2E) * rel

    def load_v(ki):
        return _with_ones(v_ref[pl.ds(pl.multiple_of(ki * tk, tk), tk), :])

    rc = min(ROW_CHUNK, tq)
    chains = [(r * tq + c * rc, c * rc) for r in range(nr) for c in range(tq // rc)]

    def softmax_pv(ki, src):
        v = load_v(ki)
        for row0, _ in chains:
            _flash_rows(src[row0:row0 + rc, :], v, m_sc, l_sc, acc_sc, row0, rc)

    def last_tile(src):
        v = load_v(last)
        k0 = last * tk
        ahead = (k0 + lax.broadcasted_iota(I32, (tq, tk), 1)) > (q0 + lax.broadcasted_iota(I32, (tq, tk), 0))
        for row0, qrow in chains:
            s2 = jnp.where(ahead[qrow:qrow + rc], NEG_INF, src[row0:row0 + rc, :])
            _flash_rows(s2, v, m_sc, l_sc, acc_sc, row0, rc)

    _pipelined_sweep(last, issue, softmax_pv, last_tile, sa_sc, sb_sc)
    o = acc_sc[...] / l_sc[...]
    o_ref[...] = jnp.concatenate([o[r * tq:(r + 1) * tq] for r in range(nr)], axis=1)


def _nsa_slc(main, sel, slopes, bsz, seq, *, tq=256, tk=512):
    tq = min(tq, seq)
    tk = min(tk, seq)
    nq = seq // tq
    n = bsz * seq
    ng, nr, hd = NSA_KV_HEADS, NSA_HEADS // NSA_KV_HEADS, NSA_HEAD_DIM
    qw = nr * hd
    kb = MAIN_AKV // hd + 2 * ng
    vb = MAIN_AKV // hd + 3 * ng
    smem = pl.BlockSpec(memory_space=pltpu.SMEM)
    return pl.pallas_call(
        functools.partial(_nsa_slc_kernel, tq=tq, tk=tk, nr=nr),
        grid=(bsz, ng, nq),
        in_specs=[smem,
                  pl.BlockSpec((tq, qw), lambda b, g, i: (b * nq + i, g)),
                  pl.BlockSpec((seq, hd), lambda b, g, i: (b, kb + g)),
                  pl.BlockSpec((seq, hd), lambda b, g, i: (b, vb + g)),
                  pl.BlockSpec((tq, LANES), lambda b, g, i: ((b * ng + g) * nq + i, 0))],
        out_specs=pl.BlockSpec((tq, qw), lambda b, g, i: (b * nq + i, g)),
        out_shape=jax.ShapeDtypeStruct((n, ng * qw), F32),
        scratch_shapes=[pltpu.VMEM((nr * tq, tk), F32), pltpu.VMEM((nr * tq, tk), F32),
                        pltpu.VMEM((nr * tq, LANES), F32), pltpu.VMEM((nr * tq, LANES), F32),
                        pltpu.VMEM((nr * tq, hd), F32)],
        compiler_params=_params(("parallel", "parallel", "arbitrary")),
        name="nsa_slc",
    )(slopes, main, main, main, sel)


def _nsa_gate_kernel(c_ref, s_ref, w_ref, g_ref, o_ref):
    gates = jax.nn.sigmoid(g_ref[...])
    hd = NSA_HEAD_DIM
    for h in range(NSA_HEADS):
        sl = slice(h * hd, (h + 1) * hd)
        lane = MISC_GATE_LANE + h
        o_ref[:, sl] = (gates[:, lane:lane + 1] * c_ref[:, sl]
                        + gates[:, lane + NSA_HEADS:lane + NSA_HEADS + 1] * s_ref[:, sl]
                        + gates[:, lane + 2 * NSA_HEADS:lane + 2 * NSA_HEADS + 1] * w_ref[:, sl])


def _nsa_gate(o_cmp, o_slc, o_win, small, *, tm=512):
    n, w = o_cmp.shape
    tm = min(tm, n)
    row = pl.BlockSpec((tm, w), lambda i: (i, 0))
    return pl.pallas_call(
        _nsa_gate_kernel,
        grid=(n // tm,),
        in_specs=[row, row, row, pl.BlockSpec((tm, LANES), lambda i: (i, SMALL_MISC // LANES))],
        out_specs=row,
        out_shape=jax.ShapeDtypeStruct((n, w), F32),
        compiler_params=_params(("parallel",)),
        name="nsa_gate",
    )(o_cmp, o_slc, o_win, small)


def _router_kernel(x_ref, g_ref, wh_ref, wm_ref, br_ref, xn_ref, route_ref, gate_ref, *, tm):
    x = x_ref[...]
    ms = jnp.mean(x * x, axis=-1, keepdims=True)
    xn = x * lax.rsqrt(ms + NORM_EPS) * g_ref[...]
    xn_ref[...] = xn
    xh = xn.astype(BF16)
    xm = (xn - xh.astype(F32)).astype(BF16)
    wh = wh_ref[...]
    lg_rows = _dot(xh, wh) + (_dot(xm, wh) + _dot(xh, wm_ref[...]))
    logits = jnp.concatenate([lg_rows[i * LANES:(i + 1) * LANES, :].T for i in range(tm // LANES)],
                             axis=1) + br_ref[:, 0:1]
    ng, ne = N_GROUPS, EXPERTS_PER_GROUP
    lg = logits[0:ng, :]
    sub = lax.broadcasted_iota(I32, (ng, tm), 0)
    mg = jnp.max(lg, axis=0, keepdims=True)
    eg = jnp.exp(lg - mg)
    pg = eg / jnp.sum(eg, axis=0, keepdims=True)
    pg_top = jnp.max(pg, axis=0, keepdims=True)
    g_idx = jnp.min(jnp.where(pg == pg_top, sub, ng), axis=0, keepdims=True)
    le = jnp.zeros((ne, tm), F32)
    for gi in range(ng):
        le = jnp.where(g_idx == gi, logits[ng + gi * ne:ng + (gi + 1) * ne, :], le)
    v1 = jnp.max(le, axis=0, keepdims=True)
    i1 = jnp.min(jnp.where(le == v1, sub, ne), axis=0, keepdims=True)
    rest = jnp.where(sub == i1, -jnp.inf, le)
    v2 = jnp.max(rest, axis=0, keepdims=True)
    i2 = jnp.min(jnp.where(rest == v2, sub, ne), axis=0, keepdims=True)
    e2 = jnp.exp(v2 - v1)
    den = 1.0 + e2
    w1 = pg_top * (1.0 / den)
    w2 = pg_top * (e2 / den)
    ex1 = (g_idx * ne + i1).astype(F32)
    ex2 = (g_idx * ne + i2).astype(F32)
    zero = jnp.zeros((1, tm), F32)
    route = jnp.concatenate([ex1, ex2, w1, w2, zero, zero, zero, zero], axis=0)
    route_ref[...] = route
    pad = jnp.concatenate([route, jnp.zeros((LANES - 8, tm), F32)], axis=0)
    gate_ref[...] = jnp.concatenate([pad[:, i * LANES:(i + 1) * LANES].T for i in range(tm // LANES)], axis=0)


def _router(x, gain, rg_w, rg_b, re_w, re_b, *, tm=256):
    n, d = x.shape
    tm = min(tm, n)
    nl = N_GROUPS + N_EXPERTS
    wr = jnp.pad(jnp.concatenate([rg_w, re_w], axis=1).astype(F32), ((0, 0), (0, LANES - nl)))
    br = jnp.zeros((LANES, LANES), F32).at[:nl, :].set(
        jnp.concatenate([rg_b, re_b]).astype(F32)[:, None] * jnp.ones((1, LANES), F32))
    wr_hi = wr.astype(BF16)
    wr_mid = (wr - wr_hi.astype(F32)).astype(BF16)
    return pl.pallas_call(
        functools.partial(_router_kernel, tm=tm),
        grid=(n // tm,),
        in_specs=[pl.BlockSpec((tm, d), lambda i: (i, 0)),
                  pl.BlockSpec((1, d), lambda i: (0, 0)),
                  pl.BlockSpec((d, LANES), lambda i: (0, 0)),
                  pl.BlockSpec((d, LANES), lambda i: (0, 0)),
                  pl.BlockSpec((LANES, LANES), lambda i: (0, 0))],
        out_specs=[pl.BlockSpec((tm, d), lambda i: (i, 0)),
                   pl.BlockSpec((8, tm), lambda i: (0, i)),
                   pl.BlockSpec((tm, LANES), lambda i: (i, 0))],
        out_shape=[jax.ShapeDtypeStruct((n, d), F32),
                   jax.ShapeDtypeStruct((8, n), F32),
                   jax.ShapeDtypeStruct((n, LANES), F32)],
        compiler_params=_params(("parallel",), 48),
        name="moe_router",
    )(x, gain.reshape(1, d).astype(F32), wr_hi, wr_mid, br)


def _slot_kernel(route_ref, pos_ref, cnt_ref, start_ref, carry_ref, *, tm):
    phase = pl.program_id(0)
    i = pl.program_id(1)
    e1 = route_ref[0:1, :].astype(I32)
    e2 = route_ref[1:2, :].astype(I32)
    sub = lax.broadcasted_iota(I32, (N_EXPERTS, tm), 0)
    oh1 = jnp.where(sub == e1, 1.0, 0.0)
    oh2 = jnp.where(sub == e2, 1.0, 0.0)
    ohs = oh1 + oh2

    @pl.when((phase == 0) & (i == 0))
    def _():
        carry_ref[...] = jnp.zeros_like(carry_ref)

    @pl.when(phase == 0)
    def _():
        carry_ref[...] = carry_ref[...] + jnp.sum(ohs, axis=1, keepdims=True)
        pos_ref[...] = jnp.zeros(pos_ref.shape, I32)

    @pl.when((phase == 1) & (i == 0))
    def _():
        cnt = carry_ref[...]
        cnt_ref[...] = cnt
        padded = jnp.floor((cnt + (MOE_ROWS - 1)) / MOE_ROWS) * MOE_ROWS
        row = lax.broadcasted_iota(I32, (N_EXPERTS, N_EXPERTS), 0)
        col = lax.broadcasted_iota(I32, (N_EXPERTS, N_EXPERTS), 1)
        lower = jnp.where(col < row, 1.0, 0.0).astype(BF16)
        hi, mid, lo = _split3(padded)
        start_ref[...] = (_dot(lower, hi) + _dot(lower, mid)) + _dot(lower, lo)
        carry_ref[...] = jnp.zeros_like(carry_ref)

    @pl.when(phase == 1)
    def _():
        row = lax.broadcasted_iota(I32, (tm, tm), 0)
        col = lax.broadcasted_iota(I32, (tm, tm), 1)
        upper = jnp.where(row < col, 1.0, 0.0).astype(BF16)
        before = _dot(ohs.astype(BF16), upper) + (carry_ref[:, 0:1] + start_ref[:, 0:1])
        p1 = jnp.sum(oh1 * before, axis=0, keepdims=True)
        p2 = jnp.sum(oh2 * before, axis=0, keepdims=True)
        zero = jnp.zeros((1, tm), F32)
        pos_ref[...] = jnp.concatenate([p1, p2, zero, zero, zero, zero, zero, zero], axis=0).astype(I32)
        carry_ref[...] = carry_ref[...] + jnp.sum(ohs, axis=1, keepdims=True)


def _slots(route, *, tm=512):
    n = route.shape[1]
    tm = min(tm, n)
    const = pl.BlockSpec((N_EXPERTS, LANES), lambda p, i: (0, 0))
    return pl.pallas_call(
        functools.partial(_slot_kernel, tm=tm),
        grid=(2, n // tm),
        in_specs=[pl.BlockSpec((8, tm), lambda p, i: (0, i))],
        out_specs=[pl.BlockSpec((8, tm), lambda p, i: (0, i * p)), const, const],
        out_shape=[jax.ShapeDtypeStruct((8, n), I32), jax.ShapeDtypeStruct((N_EXPERTS, LANES), F32),
                   jax.ShapeDtypeStruct((N_EXPERTS, LANES), F32)],
        scratch_shapes=[pltpu.VMEM((N_EXPERTS, LANES), F32)],
        compiler_params=_params(("arbitrary", "arbitrary")),
        name="moe_slots",
    )(route)


def _experts_kernel(be_ref, br_ref, dst_ref, dstn_ref, x_hbm, wg_ref, wu_ref, wd_ref, y_hbm,
                    xbuf, xb16, wgu16, wd16, acc, sem_in, sem_out, *, nblk, n_tok):
    i = pl.program_id(0)
    c = pl.program_id(1)
    nch = pl.num_programs(1)
    slot = i % 2
    rows = br_ref[i]

    def in_copy(tok, r, s):
        return pltpu.make_async_copy(x_hbm.at[pl.ds(tok, 1), :], xbuf.at[s, pl.ds(r, 1), :], sem_in.at[s])

    def out_copy(dst, r, s):
        return pltpu.make_async_copy(acc.at[s, pl.ds(r, 1), :], y_hbm.at[pl.ds(dst, 1), :], sem_out.at[s])

    def grouped(n_rows, fn, exact):
        def group(gi, carry):
            for u in range(DMA_UNROLL):
                fn(gi * DMA_UNROLL + u)
            return carry

        def single(r, carry):
            fn(r)
            return carry

        if exact:
            full = n_rows // DMA_UNROLL
            lax.fori_loop(0, full, group, 0)
            lax.fori_loop(full * DMA_UNROLL, n_rows, single, 0)
        else:
            lax.fori_loop(0, (n_rows + DMA_UNROLL - 1) // DMA_UNROLL, group, 0)

    def start_gather(tbl, n_rows, s):
        def fn(r):
            v = tbl[0, 0, r]
            in_copy(jnp.where(v >= n_tok, v - n_tok, v), r, s).start()
        grouped(n_rows, fn, False)

    def wait_gather(n_rows, s):
        grouped(n_rows, lambda r: in_copy(0, r, s).wait(), False)

    def start_scatter(n_rows, s):
        grouped(n_rows, lambda r: out_copy(dst_ref[0, 0, r], r, s).start(), True)

    def wait_scatter(n_rows, s):
        grouped(n_rows, lambda r: out_copy(0, r, s).wait(), True)

    @pl.when((i == 0) & (c == 0))
    def _():
        xbuf[...] = jnp.zeros(xbuf.shape, F32)
        start_gather(dst_ref, rows, 0)

    @pl.when(c == 0)
    def _():
        wait_gather(rows, slot)
        xb16[...] = xbuf[slot].astype(BF16)
        acc[slot] = jnp.zeros(acc.shape[1:], F32)

    @pl.when((c == 1) & (i + 1 < nblk))
    def _():
        start_gather(dstn_ref, br_ref[jnp.minimum(i + 1, nblk - 1)], 1 - slot)

    @pl.when(rows > 0)
    def _():
        wgu16[:, :DE_CHUNK] = wg_ref[0, 0].astype(BF16)
        wgu16[:, DE_CHUNK:] = wu_ref[0, 0].astype(BF16)
        wd16[...] = wd_ref[0, 0].astype(BF16)

    def sub_blocks(n_sub):
        rss = [slice(sb * MOE_SUB_ROWS, (sb + 1) * MOE_SUB_ROWS) for sb in range(n_sub)]
        hgus = [_dot(xb16[rs, :], wgu16[...]) for rs in rss]
        hs = [(jax.nn.silu(hgu[:, :DE_CHUNK]) * hgu[:, DE_CHUNK:]).astype(BF16) for hgu in hgus]
        for rs, h in zip(rss, hs):
            acc[slot, rs, :] += _dot(h, wd16[...])

    n_sub_max = MOE_ROWS // MOE_SUB_ROWS
    for n_sub in range(1, n_sub_max + 1):
        lo = (n_sub - 1) * MOE_SUB_ROWS
        cond = (rows > lo) if n_sub == n_sub_max else ((rows > lo) & (rows <= lo + MOE_SUB_ROWS))
        pl.when(cond)(functools.partial(sub_blocks, n_sub))

    @pl.when(c == nch - 1)
    def _():
        @pl.when(i > 0)
        def _():
            wait_scatter(br_ref[jnp.maximum(i - 1, 0)], 1 - slot)

        start_scatter(rows, slot)

        @pl.when(i == nblk - 1)
        def _():
            wait_scatter(rows, slot)


def _experts(xn, dst_row, blk_e, blk_rows, w_gate, w_up, w_down, layer):
    n, d = xn.shape
    nblk = blk_e.shape[0]
    nch = D_EXPERT // DE_CHUNK
    dst3 = dst_row.reshape(nblk, 1, MOE_ROWS)

    def chunk_of(i, c, br):
        return jnp.where(br[i] > 0, c, nch - 1)

    grid_spec = pltpu.PrefetchScalarGridSpec(
        num_scalar_prefetch=2,
        grid=(nblk, nch),
        in_specs=[
            pl.BlockSpec((1, 1, MOE_ROWS), lambda i, c, be, br: (i, 0, 0), memory_space=pltpu.SMEM),
            pl.BlockSpec((1, 1, MOE_ROWS), lambda i, c, be, br: (jnp.minimum(i + 1, nblk - 1), 0, 0),
                         memory_space=pltpu.SMEM),
            pl.BlockSpec(memory_space=pl.ANY),
            pl.BlockSpec((1, 1, d, DE_CHUNK), lambda i, c, be, br: (layer, be[i], 0, chunk_of(i, c, br))),
            pl.BlockSpec((1, 1, d, DE_CHUNK), lambda i, c, be, br: (layer, be[i], 0, chunk_of(i, c, br))),
            pl.BlockSpec((1, 1, DE_CHUNK, d), lambda i, c, be, br: (layer, be[i], chunk_of(i, c, br), 0)),
        ],
        out_specs=pl.BlockSpec(memory_space=pl.ANY),
        scratch_shapes=[pltpu.VMEM((2, MOE_ROWS, d), F32), pltpu.VMEM((MOE_ROWS, d), BF16),
                        pltpu.VMEM((d, 2 * DE_CHUNK), BF16), pltpu.VMEM((DE_CHUNK, d), BF16),
                        pltpu.VMEM((2, MOE_ROWS, d), F32),
                        pltpu.SemaphoreType.DMA((2,)), pltpu.SemaphoreType.DMA((2,))],
    )
    return pl.pallas_call(
        functools.partial(_experts_kernel, nblk=nblk, n_tok=n),
        grid_spec=grid_spec,
        out_shape=jax.ShapeDtypeStruct((TOP_K * n, d), F32),
        compiler_params=_params(("arbitrary", "arbitrary"), 60),
        name="moe_experts",
    )(blk_e, blk_rows, dst3, dst3, xn, w_gate, w_up, w_down)


def _moe_finish_kernel(x_ref, y0_ref, y1_ref, gate_ref, fg_ref, o_ref, *, final_norm):
    g = gate_ref[...]
    y = x_ref[...] + (g[:, 2:3] * y0_ref[...] + g[:, 3:4] * y1_ref[...])
    if final_norm:
        ms = jnp.mean(y * y, axis=-1, keepdims=True)
        y = y * lax.rsqrt(ms + NORM_EPS) * fg_ref[...]
    o_ref[...] = y


def _moe_finish(y, x, gates, final_gain=None, *, tm=256):
    n, d = x.shape
    tm = min(tm, n)
    nt = n // tm
    final_norm = final_gain is not None
    fg = (final_gain if final_norm else jnp.ones((d,), F32)).reshape(1, d).astype(F32)
    return pl.pallas_call(
        functools.partial(_moe_finish_kernel, final_norm=final_norm),
        grid=(nt,),
        in_specs=[pl.BlockSpec((tm, d), lambda i: (i, 0)),
                  pl.BlockSpec((tm, d), lambda i: (i, 0)),
                  pl.BlockSpec((tm, d), lambda i: (nt + i, 0)),
                  pl.BlockSpec((tm, LANES), lambda i: (i, 0)),
                  pl.BlockSpec((1, d), lambda i: (0, 0))],
        out_specs=pl.BlockSpec((tm, d), lambda i: (i, 0)),
        out_shape=jax.ShapeDtypeStruct((n, d), F32),
        compiler_params=_params(("parallel",), 48),
        name="moe_finish",
    )(x, y, y, gates, fg)


def _in_proj_relayout_kernel(w_ref, o_ref):
    off = np.concatenate([[0], np.cumsum(IN_SPLITS)])
    seg = lambda i: w_ref[:, int(off[i]):int(off[i + 1])]
    d9 = int(off[9])
    dkv = []
    for j in range(2 * SWA_KV_HEADS):
        piece = w_ref[:, d9 + j * SWA_HEAD_DIM:d9 + (j + 1) * SWA_HEAD_DIM]
        dkv += [piece] * (LANES // SWA_HEAD_DIM)
    rows = o_ref.shape[0]
    pad = jnp.zeros((rows, SMALL_COLS - (SMALL_MISC + 64 + 24 + 8)), F32)
    o_ref[...] = jnp.concatenate([seg(0), seg(1), seg(3), seg(8)] + dkv
                                 + [seg(5), seg(6), seg(7), seg(2), seg(4), pad], axis=1).astype(BF16)


def _in_proj_weights(w_in, layer, *, tr=256):
    _, d, cols = w_in.shape
    n_out = MAIN_COLS + SMALL_COLS
    return pl.pallas_call(
        _in_proj_relayout_kernel,
        grid=(d // tr,),
        in_specs=[pl.BlockSpec((tr, cols), lambda i: (layer * (d // tr) + i, 0))],
        out_specs=pl.BlockSpec((tr, n_out), lambda i: (i, 0)),
        out_shape=jax.ShapeDtypeStruct((d, n_out), BF16),
        compiler_params=_params(("parallel",), 48),
        name="in_proj_relayout",
    )(w_in.reshape(-1, cols))


def _to_bf16_kernel(w_ref, o_ref):
    o_ref[...] = w_ref[0].astype(BF16)


def _layer_to_bf16(w, layer, *, tr=512):
    _, d, cols = w.shape
    return pl.pallas_call(
        _to_bf16_kernel,
        grid=(d // tr,),
        in_specs=[pl.BlockSpec((1, tr, cols), lambda i: (layer, i, 0))],
        out_specs=pl.BlockSpec((tr, cols), lambda i: (i, 0)),
        out_shape=jax.ShapeDtypeStruct((d, cols), BF16),
        compiler_params=_params(("parallel",), 48),
        name="weights_to_bf16",
    )(w)


def _mla_weights(w_uq, w_ukv):
    per = MLA_NOPE_DIM + MLA_ROPE_DIM
    w3 = w_uq.reshape(MLA_Q_RANK, MLA_HEADS, per)
    nope = w3[:, :, :MLA_NOPE_DIM].reshape(MLA_Q_RANK, MLA_HEADS * MLA_NOPE_DIM)
    rope = jnp.pad(w3[:, :, MLA_NOPE_DIM:], ((0, 0), (0, 0), (0, LANES - MLA_ROPE_DIM)))
    rope = rope.reshape(MLA_Q_RANK, MLA_HEADS * LANES)
    return jnp.concatenate([nope, rope], axis=1).astype(BF16), w_ukv.astype(BF16)


def _moe_tables(pos, counts, starts, n_tok):
    n_assign = n_tok * TOP_K
    nblk = (n_assign + N_EXPERTS * (MOE_ROWS - 1)) // MOE_ROWS
    cnt = counts[:, 0].astype(I32)
    pstart = starts[:, 0].astype(I32)
    pend = pstart + (cnt + MOE_ROWS - 1) // MOE_ROWS * MOE_ROWS
    dst_row = jnp.zeros((nblk * MOE_ROWS,), I32).at[pos[0:TOP_K].reshape(-1)].set(jnp.arange(n_assign, dtype=I32))
    row0 = jnp.arange(nblk, dtype=I32) * MOE_ROWS
    blk_e = jnp.minimum(jnp.sum((pend[None, :] <= row0[:, None]).astype(I32), axis=1), N_EXPERTS - 1)
    blk_rows = jnp.clip(cnt[blk_e] - (row0 - pstart[blk_e]), 0, MOE_ROWS).astype(I32)
    return dst_row, blk_e, blk_rows


def kernel(x, norm_mix_g, w_in, nsa_kc_pos, nsa_kc_w1, nsa_kc_w2, nsa_vc_pos, nsa_vc_w1, nsa_vc_w2, fox_f_bias,
           mla_q_norm_g, mla_kv_norm_g, mla_w_uq, mla_w_ukv, swa_sinks, out_norm_g, w_out, norm_ffn_g,
           router_group_w, router_group_b, router_expert_w, router_expert_b, exp_w_gate, exp_w_up, exp_w_down,
           final_norm_g):
    bsz, seq, d_model = x.shape
    n = bsz * seq
    depth = w_in.shape[0]
    xs = x.reshape(n, d_model).astype(F32)
    nsa_slopes = _alibi_slopes(NSA_HEADS)
    swa_slopes = _alibi_slopes(SWA_HEADS)
    nsa_nr = NSA_HEADS // NSA_KV_HEADS
    for l in range(depth):
        w_proj = _in_proj_weights(w_in, l)
        main = _norm_matmul(xs, norm_mix_g[l], w_proj, out_dtype=BF16, n=MAIN_COLS, tn=1024, name="in_proj_main")
        small = _norm_matmul(xs, norm_mix_g[l], w_proj, out_dtype=F32, w_col0=MAIN_COLS, n=SMALL_COLS, tn=512,
                             name="in_proj_small")

        pos = jnp.stack([nsa_kc_pos[l], nsa_vc_pos[l]]).astype(F32)
        w1 = jnp.stack([nsa_kc_w1[l], nsa_vc_w1[l]]).astype(BF16)
        w2 = jnp.stack([nsa_kc_w2[l], nsa_vc_w2[l]]).astype(BF16)
        kvc = _compress(main, pos, w1, w2, bsz, seq)
        o_cmp, sel = _nsa_cmp(main, kvc, nsa_slopes, bsz, seq)
        o_slc = _nsa_slc(main, sel, nsa_slopes, bsz, seq)
        o_win = _banded_attn(main, nsa_slopes, None, bsz, seq, ng=NSA_KV_HEADS, nr=nsa_nr, hd=NSA_HEAD_DIM,
                             window=NSA_WINDOW, q_col=MAIN_AQ, k_col=MAIN_AKV + 8 * NSA_HEAD_DIM,
                             v_col=MAIN_AKV + 10 * NSA_HEAD_DIM, name="nsa_win", tq=256)
        out_a = _nsa_gate(o_cmp, o_slc, o_win, small)

        cumt = _fox_prep(small, fox_f_bias[l], bsz, seq)
        out_b = _fox_attn(main, cumt, bsz, seq)

        w_uq, w_ukv = _mla_weights(mla_w_uq[l], mla_w_ukv[l])
        qup = _norm_matmul(small, mla_q_norm_g[l], w_uq, out_dtype=F32, col_off=SMALL_CQ, k=MLA_Q_RANK,
                           tm=1024, tn=w_uq.shape[1], name="mla_q_up")
        kvup = _norm_matmul(small, mla_kv_norm_g[l], w_ukv, out_dtype=BF16, col_off=SMALL_CKV, k=MLA_KV_RANK,
                            tm=1024, tn=w_ukv.shape[1], name="mla_kv_up")
        qrope, krope = _rope(qup, small, bsz, seq)
        out_c = _mla_attn(qup, qrope, kvup, krope, bsz, seq)

        out_d = _banded_attn(main, swa_slopes, swa_sinks[l], bsz, seq, ng=SWA_KV_HEADS,
                             nr=SWA_HEADS // SWA_KV_HEADS, hd=SWA_HEAD_DIM, window=SWA_WINDOW, q_col=MAIN_DQ,
                             k_col=MAIN_DKV, v_col=MAIN_DKV + SWA_KV_HEADS * LANES, name="swa")

        xs = _out_proj((out_a, out_b, out_c, out_d), out_norm_g[l], _layer_to_bf16(w_out, l), xs)

        xn, route, gates = _router(xs, norm_ffn_g[l], router_group_w[l], router_group_b[l],
                                   router_expert_w[l], router_expert_b[l])
        pos_rows, counts, starts = _slots(route)
        dst_row, blk_e, blk_rows = _moe_tables(pos_rows, counts, starts, n)
        y = _experts(xn, dst_row, blk_e, blk_rows, exp_w_gate, exp_w_up, exp_w_down, l)
        xs = _moe_finish(y, xs, gates, final_norm_g if l == depth - 1 else None)
    return xs.reshape(bsz, seq, d_model)
```
